```python
import math
import jax
import jax.numpy as jnp
from jax import lax
import numpy as np

D_MODEL = 1024
BATCH = 16
SEQ = 256
DEPTH = 4
DEC_BATCH = 4
DEC_SEQ = 2048
PAST_LEN = 512

GRID_W = 64
HEAD_DIM = 64
NA_HEADS = 8
NA_KR = 8
NA_KC = 16
SSD_HEADS = 8
SSD_HEAD_DIM = 64
SSD_D_INNER = SSD_HEADS * SSD_HEAD_DIM
SSD_STATE = 128
SSD_GROUPS = 2
SSD_CHUNK = 128
SSD_CONV_W = 5
SWA_HEADS = 16
SWA_KV_HEADS = 4
SWA_WINDOW = 128
SWA_BLOCK = 128
ROPE_BASE = 10000.0
N_EXPERTS = 16
EXPERT_FF = 2048
EC_CAPACITY = 2
NORM_EPS = 1e-6

N_EVEN = (DEPTH + 1) // 2
N_ODD = DEPTH // 2
NA_WIDTH = NA_HEADS * HEAD_DIM
SSD_BC = SSD_GROUPS * SSD_STATE
SSD_CONV_CH = SSD_D_INNER + 2 * SSD_BC
EVEN_IN = 3 * NA_WIDTH + SSD_D_INNER + SSD_CONV_CH + 2 * SSD_HEADS
EVEN_MIX = NA_WIDTH + SSD_D_INNER
SWA_Q = SWA_HEADS * HEAD_DIM
SWA_KV = SWA_KV_HEADS * HEAD_DIM
ODD_IN = SWA_Q + 2 * SWA_KV

kernel_name = 'hybrid_diffusion_na_ssd_swa_ec_step'

F32 = jnp.float32


def rmsnorm(x, g):
    xf = x.astype(F32)
    y = xf * lax.rsqrt(jnp.mean(xf * xf, axis=-1, keepdims=True) + NORM_EPS)
    return (y * g.astype(F32)).astype(x.dtype)


def adaln_params(cond, w_mod, b_mod):
    m = jax.nn.silu(cond) @ w_mod + b_mod
    return [t[:, None, :] for t in jnp.split(m, 6, axis=-1)]


def modulate(h, shift, scale):
    return h * (1 + scale) + shift


def ctx_attention(q, k, v):
    s = jnp.einsum('bqhd,bkhd->bhqk', q, k).astype(F32) * (HEAD_DIM ** -0.5)
    p = jax.nn.softmax(s, axis=-1).astype(v.dtype)
    return jnp.einsum('bhqk,bkhd->bqhd', p, v)


def na_latent(q, k, v, ck, cv, rpb):
    b, t, nh, dh = q.shape
    rows = t // GRID_W
    kr = min(NA_KR, rows)
    scale = dh ** -0.5
    nnb = kr * NA_KC
    qg = q.reshape(b, rows, GRID_W, nh, dh)
    kg = k.reshape(b, rows, GRID_W, nh, dh)
    vg = v.reshape(b, rows, GRID_W, nh, dh)
    col = jnp.arange(GRID_W)
    col_idx = jnp.clip(col - NA_KC // 2, 0, GRID_W - NA_KC)[:, None] + jnp.arange(NA_KC)[None, :]
    col_bias = rpb[:, :, col_idx - col[:, None] + NA_KC - 1]

    def one_row(r):
        rs = jnp.clip(r - kr // 2, 0, rows - kr)
        k_nb = lax.dynamic_slice_in_dim(kg, rs, kr, axis=1)[:, :, col_idx]
        v_nb = lax.dynamic_slice_in_dim(vg, rs, kr, axis=1)[:, :, col_idx]
        q_r = lax.dynamic_index_in_dim(qg, r, axis=1, keepdims=False)
        bias = jnp.take(col_bias, rs + jnp.arange(kr) - r + NA_KR - 1, axis=1).transpose(2, 0, 1, 3)
        s_nb = jnp.einsum('bwhd,brwjhd->bwhrj', q_r, k_nb).astype(F32) * scale + bias.astype(F32)[None]
        s_ctx = jnp.einsum('bwhd,blhd->bwhl', q_r, ck).astype(F32) * scale
        s = jnp.concatenate([s_nb.reshape(b, GRID_W, nh, nnb), s_ctx], axis=-1)
        p = jax.nn.softmax(s, axis=-1).astype(v.dtype)
        p_nb = p[..., :nnb].reshape(b, GRID_W, nh, kr, NA_KC)
        return (jnp.einsum('bwhrj,brwjhd->bwhd', p_nb, v_nb)
                + jnp.einsum('bwhl,blhd->bwhd', p[..., nnb:], cv))

    out = lax.map(one_row, jnp.arange(rows))
    return out.transpose(1, 0, 2, 3, 4).reshape(b, t, nh, dh)


def dwconv_centred(u, w, bias):
    ch = u.shape[-1]
    pad = SSD_CONV_W // 2
    y = lax.conv_general_dilated(u, w[:, None, :].astype(u.dtype), window_strides=(1,),
                                 padding=[(pad, pad)], dimension_numbers=('NWC', 'WIO', 'NWC'),
                                 feature_group_count=ch)
    return y + bias


def segsum(a):
    n = a.shape[-1]
    cs = jnp.cumsum(a, axis=-1)
    diff = cs[..., :, None] - cs[..., None, :]
    return jnp.where(jnp.tril(jnp.ones((n, n), dtype=bool)), diff, -jnp.inf)


def ssd_scan(x, dt_raw, a_log, dt_bias, bmat, cmat, h0):
    b, t, nh, hp = x.shape
    nc = t // SSD_CHUNK
    rep = nh // bmat.shape[2]
    dt = jax.nn.softplus(dt_raw.astype(F32) + dt_bias.astype(F32))
    a = dt * (-jnp.exp(a_log.astype(F32)))
    xc = (x.astype(F32) * dt[..., None]).reshape(b, nc, SSD_CHUNK, nh, hp)
    bc = jnp.repeat(bmat.astype(F32), rep, axis=2).reshape(b, nc, SSD_CHUNK, nh, -1)
    cc = jnp.repeat(cmat.astype(F32), rep, axis=2).reshape(b, nc, SSD_CHUNK, nh, -1)
    ac = a.reshape(b, nc, SSD_CHUNK, nh).transpose(0, 3, 1, 2)
    a_cs = jnp.cumsum(ac, axis=-1)
    cb = jnp.einsum('bclhn,bcshn->bhcls', cc, bc) * jnp.exp(segsum(ac))
    y_diag = jnp.einsum('bhcls,bcshp->bclhp', cb, xc)
    decay_states = jnp.exp(a_cs[..., -1:] - a_cs)
    states = jnp.einsum('bclhn,bhcl,bclhp->bchpn', bc, decay_states, xc)
    states = jnp.concatenate([h0.astype(F32)[:, None], states], axis=1)
    chunk_decay = jnp.exp(segsum(jnp.pad(a_cs[..., -1], ((0, 0), (0, 0), (1, 0)))))
    states = jnp.einsum('bhzc,bchpn->bzhpn', chunk_decay, states)
    y_off = jnp.einsum('bclhn,bchpn,bhcl->bclhp', cc, states[:, :-1], jnp.exp(a_cs))
    y = (y_diag + y_off).reshape(b, t, nh, hp)
    return y, states[:, -1]


def flip_seq(u):
    return jnp.flip(u, axis=1)


def ssd_mixer(z, xbc, dt, ssd_p, h0_f, h0_b):
    conv_w, conv_b, a_log, dt_bias, d_skip, norm_g = ssd_p
    b, t, _ = z.shape
    xbc = jax.nn.silu(dwconv_centred(xbc, conv_w, conv_b))
    xs, bm, cm = jnp.split(xbc, [SSD_D_INNER, SSD_D_INNER + SSD_BC], axis=-1)
    xs = xs.reshape(b, t, SSD_HEADS, SSD_HEAD_DIM)
    bm = bm.reshape(b, t, SSD_GROUPS, SSD_STATE)
    cm = cm.reshape(b, t, SSD_GROUPS, SSD_STATE)
    y_f, h_f = ssd_scan(xs, dt[..., :SSD_HEADS], a_log[0], dt_bias[0], bm, cm, h0_f)
    y_b, h_b = ssd_scan(flip_seq(xs), flip_seq(dt[..., SSD_HEADS:]), a_log[1], dt_bias[1],
                        flip_seq(bm), flip_seq(cm), h0_b)
    y = y_f + flip_seq(y_b) + d_skip.astype(F32)[:, None] * xs.astype(F32)
    y = y.reshape(b, t, SSD_D_INNER).astype(z.dtype)
    return rmsnorm(y * jax.nn.silu(z), norm_g), h_f, h_b


def axial_rope(u):
    t, dh = u.shape[1], u.shape[-1]
    half = dh // 2
    quarter = half // 2
    pos = jnp.arange(t)
    inv_freq = 1.0 / (ROPE_BASE ** (jnp.arange(quarter, dtype=F32) * 2.0 / half))

    def rotate(part, p):
        ang = p.astype(F32)[:, None] * inv_freq[None, :]
        cos = jnp.cos(ang)[None, :, None, :]
        sin = jnp.sin(ang)[None, :, None, :]
        x1 = part[..., :quarter].astype(F32)
        x2 = part[..., quarter:].astype(F32)
        return jnp.concatenate([x1 * cos - x2 * sin, x1 * sin + x2 * cos], axis=-1)

    out = jnp.concatenate([rotate(u[..., :half], pos // GRID_W), rotate(u[..., half:], pos % GRID_W)], axis=-1)
    return out.astype(u.dtype)


def ctx_gqa_sink(q, k, v, sink):
    b, L, hq, dh = q.shape
    kvh = k.shape[2]
    g = hq // kvh
    qg = q.reshape(b, L, kvh, g, dh)
    s = jnp.einsum('bqkgd,blkd->bkgql', qg, k).astype(F32) * (dh ** -0.5)
    sk = jnp.broadcast_to(sink.astype(F32).reshape(kvh, g)[None, :, :, None, None], s.shape[:-1] + (1,))
    p = jax.nn.softmax(jnp.concatenate([sk, s], axis=-1), axis=-1)[..., 1:].astype(v.dtype)
    return jnp.einsum('bkgql,blkd->bqkgd', p, v).reshape(b, L, hq, dh)


def swa_latent(q, k, v, ck, cv, sink):
    b, t, hq, dh = q.shape
    kvh = k.shape[2]
    g = hq // kvh
    nb = t // SWA_BLOCK
    nloc = 3 * SWA_BLOCK
    scale = dh ** -0.5
    qb = q.reshape(b, nb, SWA_BLOCK, kvh, g, dh)

    def band(u):
        up = jnp.pad(u, ((0, 0), (SWA_BLOCK, SWA_BLOCK), (0, 0), (0, 0))).reshape(b, nb + 2, SWA_BLOCK, kvh, dh)
        return jnp.concatenate([up[:, :-2], up[:, 1:-1], up[:, 2:]], axis=2)

    kb = band(k)
    vb = band(v)
    qpos = jnp.arange(t).reshape(nb, SWA_BLOCK)
    kpos = (jnp.arange(nb)[:, None] - 1) * SWA_BLOCK + jnp.arange(nloc)[None, :]
    valid = ((kpos[:, None, :] >= 0) & (kpos[:, None, :] < t)
             & (jnp.abs(qpos[:, :, None] - kpos[:, None, :]) <= SWA_WINDOW))
    s_loc = jnp.einsum('bnqkgd,bnjkd->bnkgqj', qb, kb).astype(F32) * scale
    s_loc = jnp.where(valid[None, :, None, None], s_loc, -jnp.inf)
    s_ctx = jnp.einsum('bnqkgd,blkd->bnkgql', qb, ck).astype(F32) * scale
    sk = jnp.broadcast_to(sink.astype(F32).reshape(kvh, g)[None, None, :, :, None, None], s_loc.shape[:-1] + (1,))
    p = jax.nn.softmax(jnp.concatenate([sk, s_loc, s_ctx], axis=-1), axis=-1).astype(v.dtype)
    o = (jnp.einsum('bnkgqj,bnjkd->bnqkgd', p[..., 1:1 + nloc], vb)
         + jnp.einsum('bnkgql,blkd->bnqkgd', p[..., 1 + nloc:], cv))
    return o.reshape(b, t, hq, dh)


def even_split(h, w_in):
    b, t, _ = h.shape
    cuts = [NA_WIDTH, 2 * NA_WIDTH, 3 * NA_WIDTH, 3 * NA_WIDTH + SSD_D_INNER,
            3 * NA_WIDTH + SSD_D_INNER + SSD_CONV_CH]
    q, k, v, z, xbc, dt = jnp.split(h @ w_in, cuts, axis=-1)
    heads = (b, t, NA_HEADS, HEAD_DIM)
    return q.reshape(heads), k.reshape(heads), v.reshape(heads), z, xbc, dt


def even_mixer_context(h, w_in, w_out, ssd_p):
    b, t, _ = h.shape
    q, k, v, z, xbc, dt = even_split(h, w_in)
    o_na = ctx_attention(q, k, v).reshape(b, t, NA_WIDTH)
    h0 = jnp.zeros((b, SSD_HEADS, SSD_HEAD_DIM, SSD_STATE), F32)
    y_ssd, h_f, h_b = ssd_mixer(z, xbc, dt, ssd_p, h0, h0)
    out = jnp.concatenate([o_na, y_ssd], axis=-1) @ w_out
    return out, k, v, jnp.stack([h_f, h_b], axis=1).astype(h.dtype)


def even_mixer_latent(h, ck, cv, st, w_in, w_out, rpb, ssd_p):
    b, t, _ = h.shape
    q, k, v, z, xbc, dt = even_split(h, w_in)
    o_na = na_latent(q, k, v, ck, cv, rpb).reshape(b, t, NA_WIDTH)
    y_ssd, _, _ = ssd_mixer(z, xbc, dt, ssd_p, st[:, 0], st[:, 1])
    return jnp.concatenate([o_na, y_ssd], axis=-1) @ w_out


def odd_split(h, w_in):
    b, t, _ = h.shape
    q, k, v = jnp.split(h @ w_in, [SWA_Q, SWA_Q + SWA_KV], axis=-1)
    return (q.reshape(b, t, SWA_HEADS, HEAD_DIM), k.reshape(b, t, SWA_KV_HEADS, HEAD_DIM),
            v.reshape(b, t, SWA_KV_HEADS, HEAD_DIM))


def odd_mixer_context(h, w_in, w_out, sink):
    b, t, _ = h.shape
    q, k, v = odd_split(h, w_in)
    return ctx_gqa_sink(q, k, v, sink).reshape(b, t, SWA_Q) @ w_out, k, v


def odd_mixer_latent(h, ck, cv, w_in, w_out, sink):
    b, t, _ = h.shape
    q, k, v = odd_split(h, w_in)
    o = swa_latent(axial_rope(q), axial_rope(k), v, ck, cv, sink)
    return o.reshape(b, t, SWA_Q) @ w_out


def ec_moe(h, w_router, w_gate, w_up, w_down):
    b, t, d = h.shape
    n = b * t
    cap = EC_CAPACITY * n // N_EXPERTS
    xt = h.reshape(n, d)
    aff = jax.nn.softmax((xt @ w_router).astype(F32), axis=-1)
    gate, idx = lax.top_k(aff.T, cap)
    xe = xt[idx]
    hid = jax.nn.silu(jnp.einsum('ecd,edf->ecf', xe, w_gate)) * jnp.einsum('ecd,edf->ecf', xe, w_up)
    ye = jnp.einsum('ecf,efd->ecd', hid, w_down) * gate[..., None].astype(h.dtype)
    out = jnp.zeros_like(xt).at[idx.reshape(-1)].add(ye.reshape(-1, d))
    return out.reshape(b, t, d)


def setup_inputs(seed: int = 0) -> dict:
    key = jax.random.key(seed)
    ks = jax.random.split(key, 32)
    D = D_MODEL

    def nrm(i, shape, s):
        return jax.random.normal(ks[i], shape, F32) * s

    def unif(i, shape, lo, hi):
        return jax.random.uniform(ks[i], shape, F32, lo, hi)

    dt0 = jnp.exp(unif(14, (N_EVEN, 2, SSD_HEADS), math.log(1e-3), math.log(1e-1)))
    return {
        'x_prompt': nrm(0, (BATCH, SEQ, D), 1.0),
        'x_sample': nrm(1, (DEC_BATCH, DEC_SEQ, D), 1.0),
        'cache_na_k': nrm(2, (DEC_BATCH, N_EVEN, PAST_LEN, NA_HEADS, HEAD_DIM), 1.0),
        'cache_na_v': nrm(3, (DEC_BATCH, N_EVEN, PAST_LEN, NA_HEADS, HEAD_DIM), 1.0),
        'state_ssd': nrm(4, (DEC_BATCH, N_EVEN, 2, SSD_HEADS, SSD_HEAD_DIM, SSD_STATE), 0.5),
        'cache_swa_k': nrm(5, (DEC_BATCH, N_ODD, PAST_LEN, SWA_KV_HEADS, HEAD_DIM), 1.0),
        'cache_swa_v': nrm(6, (DEC_BATCH, N_ODD, PAST_LEN, SWA_KV_HEADS, HEAD_DIM), 1.0),
        'c': nrm(7, (DEC_BATCH, D), 1.0),
        'c_ctx': nrm(8, (D,), 1.0),
        'norm_mix': 1.0 + nrm(9, (DEPTH, D), 0.02),
        'norm_ffn': 1.0 + nrm(10, (DEPTH, D), 0.02),
        'w_mod': nrm(11, (DEPTH, D, 6 * D), 0.5 * D ** -0.5),
        'b_mod': nrm(12, (DEPTH, 6 * D), 0.02),
        'w_in_even': nrm(13, (N_EVEN, D, EVEN_IN), D ** -0.5),
        'na_rpb': nrm(15, (N_EVEN, NA_HEADS, 2 * NA_KR - 1, 2 * NA_KC - 1), 0.1),
        'ssd_conv_w': nrm(16, (N_EVEN, SSD_CONV_W, SSD_CONV_CH), SSD_CONV_W ** -0.5),
        'ssd_conv_b': nrm(17, (N_EVEN, SSD_CONV_CH), 0.02),
        'ssd_a_log': jnp.log(unif(18, (N_EVEN, 2, SSD_HEADS), 1.0, 16.0)),
        'ssd_dt_bias': dt0 + jnp.log(-jnp.expm1(-dt0)),
        'ssd_d': 1.0 + nrm(19, (N_EVEN, SSD_HEADS), 0.1),
        'ssd_norm': 1.0 + nrm(20, (N_EVEN, SSD_D_INNER), 0.02),
        'w_out_even': nrm(21, (N_EVEN, EVEN_MIX, D), EVEN_MIX ** -0.5),
        'w_in_odd': nrm(22, (N_ODD, D, ODD_IN), D ** -0.5),
        'swa_sink': nrm(23, (N_ODD, SWA_HEADS), 1.0),
        'w_out_odd': nrm(24, (N_ODD, SWA_Q, D), SWA_Q ** -0.5),
        'w_router': nrm(25, (DEPTH, D, N_EXPERTS), D ** -0.5),
        'w_gate': nrm(26, (DEPTH, N_EXPERTS, D, EXPERT_FF), D ** -0.5),
        'w_up': nrm(27, (DEPTH, N_EXPERTS, D, EXPERT_FF), D ** -0.5),
        'w_down': nrm(28, (DEPTH, N_EXPERTS, EXPERT_FF, D), EXPERT_FF ** -0.5),
        'final_norm': 1.0 + nrm(29, (D,), 0.02),
    }


def reference(x_prompt, x_sample, cache_na_k, cache_na_v, state_ssd, cache_swa_k, cache_swa_v, c, c_ctx,
              norm_mix, norm_ffn, w_mod, b_mod, w_in_even, na_rpb, ssd_conv_w, ssd_conv_b, ssd_a_log,
              ssd_dt_bias, ssd_d, ssd_norm, w_out_even, w_in_odd, swa_sink, w_out_odd,
              w_router, w_gate, w_up, w_down, final_norm):
    xp = x_prompt
    xs = x_sample
    new_na_k, new_na_v, new_ssd, new_swa_k, new_swa_v = [], [], [], [], []
    for l in range(DEPTH):
        j = l // 2
        mp = adaln_params(c_ctx[None, :], w_mod[l], b_mod[l])
        ms = adaln_params(c, w_mod[l], b_mod[l])
        hp = modulate(rmsnorm(xp, norm_mix[l]), mp[0], mp[1])
        hs = modulate(rmsnorm(xs, norm_mix[l]), ms[0], ms[1])
        if l % 2 == 0:
            ssd_p = (ssd_conv_w[j], ssd_conv_b[j], ssd_a_log[j], ssd_dt_bias[j], ssd_d[j], ssd_norm[j])
            op, kc, vc, st = even_mixer_context(hp, w_in_even[j], w_out_even[j], ssd_p)
            os_ = even_mixer_latent(hs, cache_na_k[:, j], cache_na_v[:, j], state_ssd[:, j],
                                    w_in_even[j], w_out_even[j], na_rpb[j], ssd_p)
            new_na_k.append(kc)
            new_na_v.append(vc)
            new_ssd.append(st)
        else:
            op, kc, vc = odd_mixer_context(hp, w_in_odd[j], w_out_odd[j], swa_sink[j])
            os_ = odd_mixer_latent(hs, cache_swa_k[:, j], cache_swa_v[:, j], w_in_odd[j], w_out_odd[j], swa_sink[j])
            new_swa_k.append(kc)
            new_swa_v.append(vc)
        xp = xp + mp[2] * op
        xs = xs + ms[2] * os_
        hp = modulate(rmsnorm(xp, norm_ffn[l]), mp[3], mp[4])
        hs = modulate(rmsnorm(xs, norm_ffn[l]), ms[3], ms[4])
        xp = xp + mp[5] * ec_moe(hp, w_router[l], w_gate[l], w_up[l], w_down[l])
        xs = xs + ms[5] * ec_moe(hs, w_router[l], w_gate[l], w_up[l], w_down[l])
    y_prompt = rmsnorm(xp, final_norm)
    y_sample = rmsnorm(xs, final_norm)
    return (y_prompt, y_sample, jnp.stack(new_na_k, axis=1), jnp.stack(new_na_v, axis=1),
            jnp.stack(new_ssd, axis=1), jnp.stack(new_swa_k, axis=1), jnp.stack(new_swa_v, axis=1))
```

```python
import functools
import math

import jax
import jax.numpy as jnp
import numpy as np
from jax import lax
from jax.experimental import pallas as pl
from jax.experimental.pallas import tpu as pltpu

F32 = jnp.float32
BF16 = jnp.bfloat16
I32 = jnp.int32
HIGHEST = lax.Precision.HIGHEST

D_MODEL = 1024
BATCH = 16
SEQ = 256
DEPTH = 4
DEC_BATCH = 4
DEC_SEQ = 2048
PAST_LEN = 512
GRID_W = 64
HEAD_DIM = 64
NA_HEADS = 8
NA_KR = 8
NA_KC = 16
SSD_HEADS = 8
SSD_HEAD_DIM = 64
SSD_D_INNER = SSD_HEADS * SSD_HEAD_DIM
SSD_STATE = 128
SSD_GROUPS = 2
SSD_CHUNK = 128
SSD_CONV_W = 5
SWA_HEADS = 16
SWA_KV_HEADS = 4
SWA_WINDOW = 128
SWA_BLOCK = 128
ROPE_BASE = 10000.0
N_EXPERTS = 16
EXPERT_FF = 2048
EC_CAPACITY = 2
NORM_EPS = 1e-6

N_CTX = BATCH * SEQ
N_LAT = DEC_BATCH * DEC_SEQ
N_ALL = N_CTX + N_LAT
NA_WIDTH = NA_HEADS * HEAD_DIM
SSD_BC = SSD_GROUPS * SSD_STATE
SSD_CONV_CH = SSD_D_INNER + 2 * SSD_BC
EVEN_IN = 3 * NA_WIDTH + SSD_D_INNER + SSD_CONV_CH + 2 * SSD_HEADS
EVEN_IN_PAD = 3200
SWA_Q = SWA_HEADS * HEAD_DIM
SWA_KV = SWA_KV_HEADS * HEAD_DIM
ODD_IN = SWA_Q + 2 * SWA_KV
CAP_CTX = EC_CAPACITY * N_CTX // N_EXPERTS
CAP_LAT = EC_CAPACITY * N_LAT // N_EXPERTS
CAP_ALL = CAP_CTX + CAP_LAT
LANES = 128
NEG_BIG = -1e30
ATT_SCALE = HEAD_DIM ** -0.5
VMEM_LIMIT = 56 * 1024 * 1024


def _params(sem):
    return pltpu.CompilerParams(dimension_semantics=sem, vmem_limit_bytes=VMEM_LIMIT)


def _group_of_block(m, tm):
    ctx_blocks = N_CTX // tm
    return lax.select(m < ctx_blocks, jnp.int32(0), 1 + (m - ctx_blocks) // (DEC_SEQ // tm))


def _norm_mod(x, g, shift, scale):
    ms = jnp.mean(x * x, axis=-1, keepdims=True)
    y = x * lax.rsqrt(ms + NORM_EPS) * g
    return y * (1.0 + scale) + shift


def _silu(x):
    return x * jax.nn.sigmoid(x)


def _softplus(x):
    return jnp.maximum(x, 0.0) + jnp.log1p(jnp.exp(-jnp.abs(x)))


def _dot(a, b, **kw):
    return jnp.dot(a, b, preferred_element_type=F32, **kw)


def _dot_nt(a, b, **kw):
    return lax.dot_general(a, b, (((1,), (1,)), ((), ())), preferred_element_type=F32, **kw)


def _adaln_kernel(c_ref, w_ref, b_ref, o_ref):
    o_ref[0] = _dot(_silu(c_ref[...]), w_ref[0], precision=HIGHEST) + b_ref[0]


def _adaln(cond8, w_mod, b_mod):
    d = D_MODEL
    return pl.pallas_call(
        _adaln_kernel,
        grid=(DEPTH, 6),
        in_specs=[pl.BlockSpec((8, d), lambda l, n: (0, 0)),
                  pl.BlockSpec((1, d, d), lambda l, n: (l, 0, n)),
                  pl.BlockSpec((1, 1, d), lambda l, n: (l, 0, n))],
        out_specs=pl.BlockSpec((1, 8, d), lambda l, n: (l, 0, n)),
        out_shape=jax.ShapeDtypeStruct((DEPTH, 8, 6 * d), F32),
        compiler_params=_params(("arbitrary", "arbitrary")),
        name="adaln",
    )(cond8, w_mod, b_mod.reshape(DEPTH, 1, 6 * d))


def _proj_in_kernel(x_ref, g_ref, mod_ref, w_ref, o_ref, *, tm):
    grp = _group_of_block(pl.program_id(0), tm)
    shift = mod_ref[pl.ds(grp, 1), 0:D_MODEL]
    scale = mod_ref[pl.ds(grp, 1), D_MODEL:2 * D_MODEL]
    h = _norm_mod(x_ref[...], g_ref[...], shift, scale).astype(BF16)
    o_ref[...] = _dot(h, w_ref[...])


def _proj_in(x, g, mod_l, w_bf16, tm=512):
    n = w_bf16.shape[1]
    return pl.pallas_call(
        functools.partial(_proj_in_kernel, tm=tm),
        grid=(N_ALL // tm,),
        in_specs=[pl.BlockSpec((tm, D_MODEL), lambda m: (m, 0)),
                  pl.BlockSpec((1, D_MODEL), lambda m: (0, 0)),
                  pl.BlockSpec((8, 6 * D_MODEL), lambda m: (0, 0)),
                  pl.BlockSpec((D_MODEL, n), lambda m: (0, 0))],
        out_specs=pl.BlockSpec((tm, n), lambda m: (m, 0)),
        out_shape=jax.ShapeDtypeStruct((N_ALL, n), F32),
        compiler_params=_params(("arbitrary",)),
        name="proj_in",
    )(x, g, mod_l, w_bf16)


def _out_proj_kernel(a_ref, b_ref, w_ref, x_ref, mod_ref, o_ref, *, tm):
    grp = _group_of_block(pl.program_id(0), tm)
    gate = mod_ref[pl.ds(grp, 1), 2 * D_MODEL:3 * D_MODEL]
    half = D_MODEL // 2
    acc = _dot(a_ref[...].astype(BF16), w_ref[0:half, :]) + _dot(b_ref[...].astype(BF16), w_ref[half:, :])
    o_ref[...] = x_ref[...] + gate * acc


def _out_proj(a, acol, b, bcol, w_bf16, x, mod_l, tm=512):
    half = D_MODEL // 2
    return pl.pallas_call(
        functools.partial(_out_proj_kernel, tm=tm),
        grid=(N_ALL // tm,),
        in_specs=[pl.BlockSpec((tm, half), lambda m: (m, acol)),
                  pl.BlockSpec((tm, half), lambda m: (m, bcol)),
                  pl.BlockSpec((D_MODEL, D_MODEL), lambda m: (0, 0)),
                  pl.BlockSpec((tm, D_MODEL), lambda m: (m, 0)),
                  pl.BlockSpec((8, 6 * D_MODEL), lambda m: (0, 0))],
        out_specs=pl.BlockSpec((tm, D_MODEL), lambda m: (m, 0)),
        out_shape=jax.ShapeDtypeStruct((N_ALL, D_MODEL), F32),
        compiler_params=_params(("arbitrary",)),
        name="out_proj",
    )(a, b, w_bf16, x, mod_l)


def _ctx_na_kernel(q_ref, k_ref, v_ref, o_ref):
    for h in range(NA_HEADS):
        sl = slice(h * HEAD_DIM, (h + 1) * HEAD_DIM)
        q = q_ref[:, sl].astype(BF16)
        k = k_ref[:, sl].astype(BF16)
        v = v_ref[:, sl].astype(BF16)
        s = _dot_nt(q, k) * ATT_SCALE
        p = jnp.exp(s - jnp.max(s, axis=-1, keepdims=True))
        p = p / jnp.sum(p, axis=-1, keepdims=True)
        o_ref[:, sl] = _dot(p.astype(BF16), v)


def _ctx_na(proj):
    w = NA_WIDTH
    return pl.pallas_call(
        _ctx_na_kernel,
        grid=(BATCH,),
        in_specs=[pl.BlockSpec((SEQ, w), lambda b: (b, 0)),
                  pl.BlockSpec((SEQ, w), lambda b: (b, 1)),
                  pl.BlockSpec((SEQ, w), lambda b: (b, 2))],
        out_specs=pl.BlockSpec((SEQ, w), lambda b: (b, 0)),
        out_shape=jax.ShapeDtypeStruct((N_CTX, w), F32),
        compiler_params=_params(("arbitrary",)),
        name="ctx_na",
    )(proj, proj, proj)


NA_QROWS = 4
NA_KROWS = 12
NA_ROWS = DEC_SEQ // GRID_W
NA_GROUPS = NA_ROWS // NA_QROWS


def _na_key_start(g):
    return jnp.clip(NA_QROWS * g - NA_KR // 2, 0, NA_ROWS - NA_KROWS)


def _na_row_geometry(g):
    start = int(np.clip(NA_QROWS * g - NA_KR // 2, 0, NA_ROWS - NA_KROWS))
    rows = [NA_QROWS * g + qr for qr in range(NA_QROWS)]
    return start, [(r, int(np.clip(r - NA_KR // 2, 0, NA_ROWS - NA_KR))) for r in rows]


def _na_fill_bias(ct_ref, bias_scr, g):
    start, rows = _na_row_geometry(g)
    masked = jnp.full((GRID_W, GRID_W), NEG_BIG, F32)
    for i in range(2):
        for qr, (r, rs) in enumerate(rows):
            for kr in range(NA_KROWS):
                keyrow = start + kr
                inside = rs <= keyrow < rs + NA_KR
                tile = ct_ref[i, keyrow - r + NA_KR - 1] if inside else masked
                bias_scr[i, qr * GRID_W:(qr + 1) * GRID_W, kr * GRID_W:(kr + 1) * GRID_W] = tile


def _lat_na_kernel(q_ref, k_ref, v_ref, ck_ref, cv_ref, ct_ref, o_ref, bias_scr):
    g = pl.program_id(2)
    for g_build in (0, 1, NA_GROUPS - 1):
        @pl.when(g == g_build)
        def _(g_build=g_build):
            _na_fill_bias(ct_ref, bias_scr, g_build)

    start = pl.multiple_of(_na_key_start(g) * GRID_W, GRID_W)
    nk = NA_KROWS * GRID_W
    for i in range(2):
        sl = slice(i * HEAD_DIM, (i + 1) * HEAD_DIM)
        q = q_ref[:, sl].astype(BF16)
        kl = k_ref[pl.ds(start, nk), sl].astype(BF16)
        vl = v_ref[pl.ds(start, nk), sl].astype(BF16)
        s_nb = _dot_nt(q, kl) * ATT_SCALE + bias_scr[i]
        s_ctx = _dot_nt(q, ck_ref[:, sl].astype(BF16)) * ATT_SCALE
        m = jnp.maximum(jnp.max(s_nb, axis=-1, keepdims=True), jnp.max(s_ctx, axis=-1, keepdims=True))
        p_nb = jnp.exp(s_nb - m)
        p_ctx = jnp.exp(s_ctx - m)
        inv = 1.0 / (jnp.sum(p_nb, axis=-1, keepdims=True) + jnp.sum(p_ctx, axis=-1, keepdims=True))
        o = _dot((p_nb * inv).astype(BF16), vl) + _dot((p_ctx * inv).astype(BF16), cv_ref[:, sl].astype(BF16))
        o_ref[:, sl] = o


def _na_col_table(rpb):
    w = np.arange(GRID_W)[:, None]
    cc = np.arange(GRID_W)[None, :]
    cs = np.clip(w - NA_KC // 2, 0, GRID_W - NA_KC)
    valid = (cc >= cs) & (cc < cs + NA_KC)
    dc = cc - w + NA_KC - 1
    onehot = ((dc[..., None] == np.arange(2 * NA_KC - 1)) & valid[..., None]).astype(np.float32)
    ct = jnp.einsum('hrd,wcd->hrwc', rpb, jnp.asarray(onehot), precision=HIGHEST)
    return jnp.where(jnp.asarray(valid)[None, None], ct, NEG_BIG)


def _lat_na(proj, ck, cv, col_tab):
    nq = NA_QROWS * GRID_W
    nk = NA_KROWS * GRID_W
    lat_q0 = N_CTX // nq
    lat_b0 = N_CTX // DEC_SEQ
    return pl.pallas_call(
        _lat_na_kernel,
        grid=(NA_HEADS // 2, DEC_BATCH, NA_GROUPS),
        in_specs=[pl.BlockSpec((nq, LANES), lambda hp, b, g: (lat_q0 + b * NA_GROUPS + g, hp)),
                  pl.BlockSpec((DEC_SEQ, LANES), lambda hp, b, g: (lat_b0 + b, 4 + hp)),
                  pl.BlockSpec((DEC_SEQ, LANES), lambda hp, b, g: (lat_b0 + b, 8 + hp)),
                  pl.BlockSpec((None, PAST_LEN, LANES), lambda hp, b, g: (b, 0, hp)),
                  pl.BlockSpec((None, PAST_LEN, LANES), lambda hp, b, g: (b, 0, hp)),
                  pl.BlockSpec((2, 2 * NA_KR - 1, GRID_W, GRID_W), lambda hp, b, g: (hp, 0, 0, 0))],
        out_specs=pl.BlockSpec((nq, LANES), lambda hp, b, g: (b * NA_GROUPS + g, hp)),
        out_shape=jax.ShapeDtypeStruct((N_LAT, NA_WIDTH), F32),
        scratch_shapes=[pltpu.VMEM((2, nq, nk), F32)],
        compiler_params=_params(("arbitrary", "arbitrary", "arbitrary")),
        name="lat_na",
    )(proj, proj, proj, ck, cv, col_tab)


def _ssd_kernel(*refs, seq, has_h0, want_state):
    it = iter(refs)
    z_ref, xbc_ref, dt_ref = next(it), next(it), next(it)
    h0_ref = next(it) if has_h0 else None
    cw_ref, cb_ref, a_ref, dtb_ref, dsk_ref, ng_ref = (next(it) for _ in range(6))
    y_ref = next(it)
    st_ref = next(it) if want_state else None
    xc_scr, y_scr, ht_scr = next(it), next(it), next(it)

    nc = seq // SSD_CHUNK
    ch = SSD_CHUNK
    row = lax.broadcasted_iota(I32, (ch, ch), 0)
    col = lax.broadcasted_iota(I32, (ch, ch), 1)
    erow = lax.broadcasted_iota(I32, (LANES, SSD_D_INNER), 0)
    ecol = lax.broadcasted_iota(I32, (LANES, SSD_D_INNER), 1) // SSD_HEAD_DIM
    cbias = cb_ref[...]

    def conv_chunk(c, carry):
        base = pl.multiple_of(c * ch, ch)
        cur = xbc_ref[pl.ds(base, ch), :]
        pbase = pl.multiple_of(jnp.maximum(base - 8, 0), 8)
        nbase = pl.multiple_of(jnp.minimum(base + ch, seq - 8), 8)
        prev = jnp.where(c > 0, xbc_ref[pl.ds(pbase, 8), :], 0.0)
        nxt = jnp.where(c < nc - 1, xbc_ref[pl.ds(nbase, 8), :], 0.0)
        win = jnp.concatenate([prev, cur, nxt], axis=0)
        acc = jnp.broadcast_to(cbias, (ch, SSD_CONV_CH))
        pad = SSD_CONV_W // 2
        for k in range(SSD_CONV_W):
            off = 8 - pad + k
            acc = acc + win[off:off + ch, :] * cw_ref[k:k + 1, :]
        xc_scr[pl.ds(base, ch), :] = _silu(acc)
        return carry

    lax.fori_loop(0, nc, conv_chunk, 0)

    def run_direction(d):
        lane0 = d * SSD_HEADS
        expand = jnp.where(erow == ecol + lane0, 1.0, 0.0)
        if d == 0:
            tri = jnp.where(col <= row, 1.0, 0.0)
            lmask = col <= row
        else:
            tri = jnp.where(col >= row, 1.0, 0.0)
            lmask = col >= row
        if has_h0:
            ht_scr[...] = h0_ref[d]
        else:
            ht_scr[...] = jnp.zeros((SSD_STATE, SSD_D_INNER), F32)

        def chunk(step, carry):
            c = step if d == 0 else nc - 1 - step
            base = pl.multiple_of(c * ch, ch)
            xs = xc_scr[pl.ds(base, ch), 0:SSD_D_INNER]
            dt = _softplus(dt_ref[pl.ds(base, ch), :] + dtb_ref[...])
            a = dt * a_ref[...]
            cum = _dot(tri, a, precision=HIGHEST)
            tot = cum[ch - 1:ch, :] if d == 0 else cum[0:1, :]
            cum_t = cum.T
            dt_x = _dot(dt, expand, precision=HIGHEST)
            cum_x = _dot(cum, expand, precision=HIGHEST)
            tot_x = _dot(jnp.broadcast_to(tot, (8, LANES)), expand, precision=HIGHEST)[0:1, :]
            xt = xs * dt_x
            xd = (xt * jnp.exp(tot_x - cum_x)).astype(BF16)
            xt_b = xt.astype(BF16)
            y_parts = []
            for grp in range(SSD_GROUPS):
                bsl = slice(SSD_D_INNER + grp * SSD_STATE, SSD_D_INNER + (grp + 1) * SSD_STATE)
                csl = slice(SSD_D_INNER + SSD_BC + grp * SSD_STATE, SSD_D_INNER + SSD_BC + (grp + 1) * SSD_STATE)
                b_f = xc_scr[pl.ds(base, ch), bsl]
                b_g = b_f.astype(BF16)
                c_g = xc_scr[pl.ds(base, ch), csl].astype(BF16)
                cb = _dot_nt(c_g, b_g)
                hsl = slice(grp * 4 * SSD_HEAD_DIM, (grp + 1) * 4 * SSD_HEAD_DIM)
                ht_g = ht_scr[:, hsl]
                y_off = _dot(c_g, ht_g.astype(BF16))
                for hh in range(4):
                    head = grp * 4 + hh
                    lane = lane0 + head
                    cum_col = cum[:, lane:lane + 1]
                    cum_row = cum_t[lane:lane + 1, :]
                    ldec = jnp.exp(jnp.where(lmask, cum_col - cum_row, NEG_BIG))
                    psl = slice(head * SSD_HEAD_DIM, (head + 1) * SSD_HEAD_DIM)
                    y_d = _dot((cb * ldec).astype(BF16), xt_b[:, psl])
                    y_parts.append(y_d + y_off[:, hh * SSD_HEAD_DIM:(hh + 1) * SSD_HEAD_DIM] * jnp.exp(cum_col))
                ht_scr[:, hsl] = ht_g * jnp.exp(tot_x[:, hsl]) + _dot(b_f.T.astype(BF16), xd[:, hsl])
            y = jnp.concatenate(y_parts, axis=1)
            if d == 0:
                y_scr[pl.ds(base, ch), :] = y
            else:
                y = y + y_scr[pl.ds(base, ch), :] + dsk_ref[...] * xs
                u = y * _silu(z_ref[pl.ds(base, ch), :])
                ms = jnp.mean(u * u, axis=-1, keepdims=True)
                y_ref[pl.ds(base, ch), :] = u * lax.rsqrt(ms + NORM_EPS) * ng_ref[...]
            return carry

        lax.fori_loop(0, nc, chunk, 0)
        if want_state:
            st_ref[d] = ht_scr[...]

    run_direction(0)
    run_direction(1)


def _ssd(proj, h0t, consts, *, seq, nbatch, row_blk0, want_state):
    has_h0 = h0t is not None
    in_specs = [pl.BlockSpec((seq, SSD_D_INNER), lambda b: (row_blk0 + b, 3)),
                pl.BlockSpec((seq, SSD_CONV_CH), lambda b: (row_blk0 + b, 2)),
                pl.BlockSpec((seq, LANES), lambda b: (row_blk0 + b, 24))]
    args = [proj, proj, proj]
    if has_h0:
        in_specs.append(pl.BlockSpec((None, 2, SSD_STATE, SSD_D_INNER), lambda b: (b, 0, 0, 0)))
        args.append(h0t)
    for cst in consts:
        in_specs.append(pl.BlockSpec(cst.shape, lambda b: (0, 0)))
        args.append(cst)
    out_specs = [pl.BlockSpec((seq, SSD_D_INNER), lambda b: (b, 0))]
    out_shape = [jax.ShapeDtypeStruct((nbatch * seq, SSD_D_INNER), F32)]
    if want_state:
        out_specs.append(pl.BlockSpec((None, 2, SSD_STATE, SSD_D_INNER), lambda b: (b, 0, 0, 0)))
        out_shape.append(jax.ShapeDtypeStruct((nbatch, 2, SSD_STATE, SSD_D_INNER), F32))
    res = pl.pallas_call(
        functools.partial(_ssd_kernel, seq=seq, has_h0=has_h0, want_state=want_state),
        grid=(nbatch,),
        in_specs=in_specs,
        out_specs=out_specs,
        out_shape=out_shape,
        scratch_shapes=[pltpu.VMEM((seq, SSD_CONV_CH), F32),
                        pltpu.VMEM((seq, SSD_D_INNER), F32),
                        pltpu.VMEM((SSD_STATE, SSD_D_INNER), F32)],
        compiler_params=_params(("arbitrary",)),
        name="ssd_seq%d" % seq,
    )(*args)
    return res if want_state else (res[0], None)


def _ctx_swa_kernel(q_ref, k_ref, v_ref, sink_ref, o_ref):
    grp = SWA_HEADS // SWA_KV_HEADS
    for kv in range(SWA_KV_HEADS):
        ksl = slice(kv * HEAD_DIM, (kv + 1) * HEAD_DIM)
        k = k_ref[:, ksl].astype(BF16)
        v = v_ref[:, ksl].astype(BF16)
        q4 = jnp.concatenate(
            [q_ref[:, (kv * grp + j) * HEAD_DIM:(kv * grp + j + 1) * HEAD_DIM] for j in range(grp)],
            axis=0).astype(BF16)
        s = _dot_nt(q4, k) * ATT_SCALE
        sink = sink_ref[kv]
        m = jnp.maximum(jnp.max(s, axis=-1, keepdims=True), sink)
        p = jnp.exp(s - m)
        inv = 1.0 / (jnp.sum(p, axis=-1, keepdims=True) + jnp.exp(sink - m))
        o = _dot((p * inv).astype(BF16), v)
        for j in range(grp):
            o_ref[:, (kv * grp + j) * HEAD_DIM:(kv * grp + j + 1) * HEAD_DIM] = o[j * SEQ:(j + 1) * SEQ]


def _ctx_swa(proj, sink_col):
    return pl.pallas_call(
        _ctx_swa_kernel,
        grid=(BATCH,),
        in_specs=[pl.BlockSpec((SEQ, SWA_Q), lambda b: (b, 0)),
                  pl.BlockSpec((SEQ, SWA_KV), lambda b: (b, 4)),
                  pl.BlockSpec((SEQ, SWA_KV), lambda b: (b, 5)),
                  pl.BlockSpec(sink_col.shape, lambda b: (0, 0, 0))],
        out_specs=pl.BlockSpec((SEQ, SWA_Q), lambda b: (b, 0)),
        out_shape=jax.ShapeDtypeStruct((N_CTX, SWA_Q), F32),
        compiler_params=_params(("arbitrary",)),
        name="ctx_swa",
    )(proj, proj, proj, sink_col)


SWA_NLOC = 3 * SWA_BLOCK


def _rope(x, cos, sin_signed, first):
    n = x.shape[-1]
    partner = jnp.where(first, pltpu.roll(x, n - 16, 1), pltpu.roll(x, 16, 1))
    return x * cos + partner * sin_signed


def _lat_swa_kernel(q_ref, k_ref, v_ref, ck_ref, cv_ref, cosk_ref, sink_k_ref, cosq_ref, sinq_ref, sink_ref,
                    o_ref, kr_scr):
    i = pl.program_id(1)
    grp = SWA_HEADS // SWA_KV_HEADS
    lane = lax.broadcasted_iota(I32, (1, SWA_KV), 1)
    first = (lane % 32) < 16

    @pl.when(i == 0)
    def _():
        kr_scr[...] = _rope(k_ref[...], cosk_ref[...], sink_k_ref[...], first).astype(BF16)

    kstart = pl.multiple_of(jnp.clip((i - 1) * SWA_BLOCK, 0, DEC_SEQ - SWA_NLOC), SWA_BLOCK)
    kloc = kr_scr[pl.ds(kstart, SWA_NLOC), :]
    vloc = v_ref[pl.ds(kstart, SWA_NLOC), :].astype(BF16)
    qpos = i * SWA_BLOCK + lax.broadcasted_iota(I32, (SWA_BLOCK, SWA_NLOC), 0)
    kpos = kstart + lax.broadcasted_iota(I32, (SWA_BLOCK, SWA_NLOC), 1)
    mask1 = jnp.where(jnp.abs(qpos - kpos) <= SWA_WINDOW, 0.0, NEG_BIG)
    mask = jnp.concatenate([mask1] * grp, axis=0)
    cosq = cosq_ref[...]
    sinq = sinq_ref[...]
    for kv in range(SWA_KV_HEADS):
        ksl = slice(kv * HEAD_DIM, (kv + 1) * HEAD_DIM)
        qr = _rope(q_ref[:, kv * SWA_KV:(kv + 1) * SWA_KV], cosq, sinq, first)
        q4 = jnp.concatenate([qr[:, j * HEAD_DIM:(j + 1) * HEAD_DIM] for j in range(grp)], axis=0).astype(BF16)
        s_loc = _dot_nt(q4, kloc[:, ksl]) * ATT_SCALE + mask
        s_ctx = _dot_nt(q4, ck_ref[:, ksl].astype(BF16)) * ATT_SCALE
        sink = sink_ref[kv]
        m = jnp.maximum(jnp.maximum(jnp.max(s_loc, axis=-1, keepdims=True),
                                    jnp.max(s_ctx, axis=-1, keepdims=True)), sink)
        p_loc = jnp.exp(s_loc - m)
        p_ctx = jnp.exp(s_ctx - m)
        inv = 1.0 / (jnp.sum(p_loc, axis=-1, keepdims=True) + jnp.sum(p_ctx, axis=-1, keepdims=True)
                     + jnp.exp(sink - m))
        o = (_dot((p_loc * inv).astype(BF16), vloc[:, ksl])
             + _dot((p_ctx * inv).astype(BF16), cv_ref[:, ksl].astype(BF16)))
        for j in range(grp):
            o_ref[:, (kv * grp + j) * HEAD_DIM:(kv * grp + j + 1) * HEAD_DIM] = o[j * SWA_BLOCK:(j + 1) * SWA_BLOCK]


def _rope_tables():
    half = HEAD_DIM // 2
    quarter = half // 2
    pos = jnp.arange(DEC_SEQ)
    inv_freq = 1.0 / (ROPE_BASE ** (jnp.arange(quarter, dtype=F32) * 2.0 / half))
    d = np.arange(HEAD_DIM)
    use_col = jnp.asarray(d >= half)
    p = jnp.where(use_col[None, :], (pos % GRID_W)[:, None], (pos // GRID_W)[:, None]).astype(F32)
    ang = p * inv_freq[d % quarter][None, :]
    sign = jnp.asarray(np.where((d % half) < quarter, -1.0, 1.0), F32)
    cos = jnp.cos(ang)
    sin_signed = jnp.sin(ang) * sign[None, :]
    reps = SWA_KV // HEAD_DIM
    return jnp.tile(cos, (1, reps)), jnp.tile(sin_signed, (1, reps))


def _lat_swa(proj, ck, cv, cos_t, sin_t, sink_col):
    nb = DEC_SEQ // SWA_BLOCK
    q0 = N_CTX // SWA_BLOCK
    b0 = N_CTX // DEC_SEQ
    return pl.pallas_call(
        _lat_swa_kernel,
        grid=(DEC_BATCH, nb),
        in_specs=[pl.BlockSpec((SWA_BLOCK, SWA_Q), lambda b, i: (q0 + b * nb + i, 0)),
                  pl.BlockSpec((DEC_SEQ, SWA_KV), lambda b, i: (b0 + b, 4)),
                  pl.BlockSpec((DEC_SEQ, SWA_KV), lambda b, i: (b0 + b, 5)),
                  pl.BlockSpec((None, PAST_LEN, SWA_KV), lambda b, i: (b, 0, 0)),
                  pl.BlockSpec((None, PAST_LEN, SWA_KV), lambda b, i: (b, 0, 0)),
                  pl.BlockSpec((DEC_SEQ, SWA_KV), lambda b, i: (0, 0)),
                  pl.BlockSpec((DEC_SEQ, SWA_KV), lambda b, i: (0, 0)),
                  pl.BlockSpec((SWA_BLOCK, SWA_KV), lambda b, i: (i, 0)),
                  pl.BlockSpec((SWA_BLOCK, SWA_KV), lambda b, i: (i, 0)),
                  pl.BlockSpec(sink_col.shape, lambda b, i: (0, 0, 0))],
        out_specs=pl.BlockSpec((SWA_BLOCK, SWA_Q), lambda b, i: (b * nb + i, 0)),
        out_shape=jax.ShapeDtypeStruct((N_LAT, SWA_Q), F32),
        scratch_shapes=[pltpu.VMEM((DEC_SEQ, SWA_KV), BF16)],
        compiler_params=_params(("arbitrary", "arbitrary")),
        name="lat_swa",
    )(proj, proj, proj, ck, cv, cos_t, sin_t, cos_t, sin_t, sink_col)


def _ffn_prep_kernel(x_ref, g_ref, mod_ref, wr_ref, h_ref, aff_ref, *, tm):
    grp = _group_of_block(pl.program_id(0), tm)
    shift = mod_ref[pl.ds(grp, 1), 3 * D_MODEL:4 * D_MODEL]
    scale = mod_ref[pl.ds(grp, 1), 4 * D_MODEL:5 * D_MODEL]
    h = _norm_mod(x_ref[...], g_ref[...], shift, scale)
    h_ref[...] = h
    logits = _dot_nt(wr_ref[...], h, precision=HIGHEST)
    e = jnp.exp(logits - jnp.max(logits, axis=0, keepdims=True))
    aff_ref[...] = e / jnp.sum(e, axis=0, keepdims=True)


def _ffn_prep(x, g, mod_l, w_router_t, tm=512):
    return pl.pallas_call(
        functools.partial(_ffn_prep_kernel, tm=tm),
        grid=(N_ALL // tm,),
        in_specs=[pl.BlockSpec((tm, D_MODEL), lambda m: (m, 0)),
                  pl.BlockSpec((1, D_MODEL), lambda m: (0, 0)),
                  pl.BlockSpec((8, 6 * D_MODEL), lambda m: (0, 0)),
                  pl.BlockSpec((N_EXPERTS, D_MODEL), lambda m: (0, 0))],
        out_specs=[pl.BlockSpec((tm, D_MODEL), lambda m: (m, 0)),
                   pl.BlockSpec((N_EXPERTS, tm), lambda m: (0, m))],
        out_shape=[jax.ShapeDtypeStruct((N_ALL, D_MODEL), F32),
                   jax.ShapeDtypeStruct((N_EXPERTS, N_ALL), F32)],
        compiler_params=_params(("arbitrary",)),
        name="ffn_prep",
    )(x, g, mod_l, w_router_t)


def _route_kernel(aff_ref, idx_ref, gate_ref, cend_ref, pos_ref, t_scr, *, nb, cap):
    ne = N_EXPERTS
    capf = float(cap)

    def count_ge(e, cand):
        hit = jnp.where(aff_ref[e] >= cand, 1.0, 0.0)
        return jnp.sum(jnp.sum(hit, axis=1, keepdims=True), axis=0, keepdims=True)

    def bit_step(i, ts):
        bit = jnp.left_shift(jnp.int32(1), 30 - i)
        out = []
        for e in range(ne):
            cand = ts[e] | bit
            keep = count_ge(e, lax.bitcast_convert_type(cand, F32)) >= capf
            out.append(jnp.where(keep, cand, ts[e]))
        return tuple(out)

    ts = lax.fori_loop(0, 31, bit_step, tuple(jnp.zeros((1, 1), I32) for _ in range(ne)))
    for e in range(ne):
        t_scr[e] = jnp.broadcast_to(lax.bitcast_convert_type(ts[e], F32), (8, LANES))

    r = lax.broadcasted_iota(I32, (LANES, LANES), 0)
    c = lax.broadcasted_iota(I32, (LANES, LANES), 1)
    upper = jnp.where(r <= c, 1.0, 0.0).astype(BF16)
    lower_incl = jnp.where(c <= r, 1.0, 0.0).astype(BF16)
    lower_strict = jnp.where(c < r, 1.0, 0.0).astype(BF16)
    npad = LANES - nb
    slot = lax.broadcasted_iota(I32, (LANES, cap), 1).astype(F32)
    sub = lax.broadcasted_iota(I32, (LANES, cap), 0).astype(F32)

    def cumsum_tokens(x):
        rowc = _dot(x.astype(BF16), upper)
        tot = jnp.broadcast_to(rowc[:, LANES - 1:LANES], (LANES, LANES))
        return rowc + _dot(lower_strict, tot.astype(BF16)), tot

    def per_expert(e, carry):
        a = aff_ref[e]
        if npad:
            a = jnp.concatenate([a, jnp.full((npad, LANES), -1.0, F32)], axis=0)
        thr = t_scr[e][0:1, 0:1]
        gt = a > thr
        eq = a == thr
        gtf = jnp.where(gt, 1.0, 0.0)
        eqf = jnp.where(eq, 1.0, 0.0)
        need = capf - jnp.sum(jnp.sum(gtf, axis=1, keepdims=True), axis=0, keepdims=True)
        eq_incl, _ = cumsum_tokens(eqf)
        sel = gt | (eq & (eq_incl - eqf < need))
        self = jnp.where(sel, 1.0, 0.0)
        incl, tot = cumsum_tokens(self)
        cend = _dot(lower_incl, tot.astype(BF16))
        blk = jnp.sum(jnp.where(cend[:, 0:1] <= slot, 1.0, 0.0), axis=0, keepdims=True)
        onehot = jnp.where(sub == blk, 1.0, 0.0)
        hi = jnp.floor(incl * (1.0 / LANES))
        lo = incl - hi * LANES
        inc_s = (_dot(hi.T.astype(BF16), onehot.astype(BF16)) * LANES
                 + _dot(lo.T.astype(BF16), onehot.astype(BF16)))
        within = jnp.sum(jnp.where(inc_s <= slot, 1.0, 0.0), axis=0, keepdims=True)
        aff_s = _dot(jnp.maximum(a, 0.0).T, onehot, precision=HIGHEST)
        gate = jnp.sum(jnp.where(sub == within, aff_s, 0.0), axis=0, keepdims=True)
        idx_ref[e] = (blk * LANES + within).astype(I32)
        gate_ref[e] = gate
        cend_ref[e] = cend.T[0:8, :].astype(I32)
        pos_ref[e] = jnp.where(sel, incl - 1.0, -1.0)[0:nb, :]
        return carry

    lax.fori_loop(0, ne, per_expert, 0)


def _route(aff3, cap):
    ne, nb, _ = aff3.shape
    return pl.pallas_call(
        functools.partial(_route_kernel, nb=nb, cap=cap),
        grid=(1,),
        in_specs=[pl.BlockSpec((ne, nb, LANES), lambda i: (0, 0, 0))],
        out_specs=[pl.BlockSpec((ne, 1, cap), lambda i: (0, 0, 0)),
                   pl.BlockSpec((ne, 1, cap), lambda i: (0, 0, 0)),
                   pl.BlockSpec((ne, 8, LANES), lambda i: (0, 0, 0)),
                   pl.BlockSpec((ne, nb, LANES), lambda i: (0, 0, 0))],
        out_shape=[jax.ShapeDtypeStruct((ne, 1, cap), I32),
                   jax.ShapeDtypeStruct((ne, 1, cap), F32),
                   jax.ShapeDtypeStruct((ne, 8, LANES), I32),
                   jax.ShapeDtypeStruct((ne, nb, LANES), F32)],
        scratch_shapes=[pltpu.VMEM((ne, 8, LANES), F32)],
        compiler_params=_params(("arbitrary",)),
        name="route_cap%d" % cap,
    )(aff3)


FF_TILE = 512
COMB_PIECE = 32
CAP_PAD = CAP_ALL + COMB_PIECE


def _moe_ffn_kernel(idx_ref, h_hbm, gate_ref, wg_ref, wu_ref, wd_ref, o_ref, land, xe, sem):
    e = pl.program_id(0)
    f = pl.program_id(1)
    slot = e % 2

    def issue_rows(expert, buf):
        def issue(s, carry):
            tok = idx_ref[expert * CAP_ALL + s]
            pltpu.make_async_copy(h_hbm.at[pl.ds(tok, 1), :], land.at[buf, pl.ds(s, 1), :], sem.at[buf]).start()
            return carry

        lax.fori_loop(0, CAP_ALL, issue, 0)

    @pl.when(f == 0)
    def _():
        @pl.when(e == 0)
        def _():
            issue_rows(0, 0)

        pltpu.make_async_copy(h_hbm.at[pl.ds(0, CAP_ALL), :], land.at[slot], sem.at[slot]).wait()
        xe[0:CAP_ALL, :] = land[slot].astype(BF16)
        xe[CAP_ALL:CAP_PAD, :] = jnp.zeros((CAP_PAD - CAP_ALL, D_MODEL), BF16)

        @pl.when(e + 1 < N_EXPERTS)
        def _():
            issue_rows(e + 1, 1 - slot)

        o_ref[...] = jnp.zeros_like(o_ref)

    x = xe[...]
    g = _dot(x, wg_ref[...].astype(BF16))
    u = _dot(x, wu_ref[...].astype(BF16))
    hid = (_silu(g) * u).astype(BF16)
    o_ref[...] += _dot(hid, wd_ref[...].astype(BF16))

    @pl.when(f == pl.num_programs(1) - 1)
    def _():
        o_ref[...] = o_ref[...] * gate_ref[...]


def _moe_ffn(layer, h, idx_flat, gate_col, w_gate, w_up, w_down):
    nf = EXPERT_FF // FF_TILE
    grid_spec = pltpu.PrefetchScalarGridSpec(
        num_scalar_prefetch=1,
        grid=(N_EXPERTS, nf),
        in_specs=[pl.BlockSpec(memory_space=pl.ANY),
                  pl.BlockSpec((None, CAP_PAD, 1), lambda e, f, idx: (e, 0, 0)),
                  pl.BlockSpec((None, None, D_MODEL, FF_TILE), lambda e, f, idx: (layer, e, 0, f)),
                  pl.BlockSpec((None, None, D_MODEL, FF_TILE), lambda e, f, idx: (layer, e, 0, f)),
                  pl.BlockSpec((None, None, FF_TILE, D_MODEL), lambda e, f, idx: (layer, e, f, 0))],
        out_specs=pl.BlockSpec((None, CAP_PAD, D_MODEL), lambda e, f, idx: (e, 0, 0)),
        scratch_shapes=[pltpu.VMEM((2, CAP_ALL, D_MODEL), F32),
                        pltpu.VMEM((CAP_PAD, D_MODEL), BF16),
                        pltpu.SemaphoreType.DMA((2,))],
    )
    return pl.pallas_call(
        _moe_ffn_kernel,
        grid_spec=grid_spec,
        out_shape=jax.ShapeDtypeStruct((N_EXPERTS, CAP_PAD, D_MODEL), F32),
        compiler_params=_params(("arbitrary", "arbitrary")),
        name="moe_ffn",
    )(idx_flat, h, gate_col, w_gate, w_up, w_down)


COMB_TM = 256
COMB_NB = N_ALL // COMB_TM
COMB_HALF = N_EXPERTS // 2
COMB_CHUNK = 256
COMB_STAGE = -(-(COMB_HALF * (COMB_TM + COMB_PIECE + 8)) // COMB_CHUNK) * COMB_CHUNK
ROUTE_NB = N_ALL // LANES


def _combine_kernel(st_ref, ye_hbm, pos_ref, x_ref, mod_ref, o_ref, stage, acc, sem):
    b = pl.program_id(0)
    nst = ROUTE_NB + 1
    per = COMB_TM // LANES

    def layout(blk, half):
        out = []
        off = jnp.int32(0)
        for j in range(COMB_HALF):
            e = half * COMB_HALF + j
            s0 = st_ref[e * nst + per * blk]
            s1 = st_ref[e * nst + per * blk + per]
            a0 = (s0 // 8) * 8
            npc = jnp.where(s1 > s0, (s1 - a0 + COMB_PIECE - 1) // COMB_PIECE, 0)
            out.append((e, a0, off, npc))
            off = off + npc * COMB_PIECE
        return out, off // COMB_PIECE

    def piece_copy(e, src_row, buf, dst_row):
        return pltpu.make_async_copy(ye_hbm.at[e, pl.ds(src_row, COMB_PIECE), :],
                                     stage.at[buf, pl.ds(dst_row, COMB_PIECE), :], sem.at[buf])

    def issue_unit(blk, half):
        lay, _ = layout(blk, half)
        for e, a0, off, npc in lay:
            def issue(k, carry, e=e, a0=a0, off=off):
                piece_copy(e, pl.multiple_of(a0 + k * COMB_PIECE, 8), half,
                           pl.multiple_of(off + k * COMB_PIECE, 8)).start()
                return carry

            lax.fori_loop(0, npc, issue, 0)

    @pl.when(b == 0)
    def _():
        stage[...] = jnp.zeros_like(stage)
        issue_unit(0, 0)

    acc[...] = jnp.zeros_like(acc)
    pos = pos_ref[...]
    row_iota = lax.broadcasted_iota(I32, (COMB_CHUNK, COMB_TM), 0).astype(F32)
    for half in range(2):
        lay, npieces = layout(b, half)

        def wait(k, carry, half=half):
            piece_copy(0, 0, half, 0).wait()
            return carry

        lax.fori_loop(0, npieces, wait, 0)
        if half == 0:
            issue_unit(b, 1)
        else:
            @pl.when(b + 1 < COMB_NB)
            def _():
                issue_unit(b + 1, 0)

        srows = []
        for e, a0, off, npc in lay:
            p = pos[e:e + 1, :]
            srows.append(jnp.where(p >= 0.0, p + (off - a0).astype(F32), -1.0))

        def chunk(ci, carry, half=half, srows=srows):
            r0 = pl.multiple_of(ci * COMB_CHUNK, COMB_CHUNK)
            rid = row_iota + r0.astype(F32)
            hit = jnp.zeros((COMB_CHUNK, COMB_TM), F32)
            for sr in srows:
                hit = jnp.where(sr == rid, 1.0, hit)
            sel_t = hit.T.astype(BF16)
            y = stage[half, pl.ds(r0, COMB_CHUNK), :]
            y_hi = y.astype(BF16)
            y_lo = (y - y_hi.astype(F32)).astype(BF16)
            acc[...] += _dot(sel_t, y_hi) + _dot(sel_t, y_lo)
            return carry

        nchunks = (npieces * COMB_PIECE + COMB_CHUNK - 1) // COMB_CHUNK
        lax.fori_loop(0, nchunks, chunk, 0)

    grp = _group_of_block(b, COMB_TM)
    gate = mod_ref[pl.ds(grp, 1), 5 * D_MODEL:6 * D_MODEL]
    o_ref[...] = x_ref[...] + gate * acc[...]


def _combine(ye, starts_flat, pos, x, mod_l):
    grid_spec = pltpu.PrefetchScalarGridSpec(
        num_scalar_prefetch=1,
        grid=(COMB_NB,),
        in_specs=[pl.BlockSpec(memory_space=pl.ANY),
                  pl.BlockSpec((N_EXPERTS, COMB_TM), lambda b, st: (0, b)),
                  pl.BlockSpec((COMB_TM, D_MODEL), lambda b, st: (b, 0)),
                  pl.BlockSpec((8, 6 * D_MODEL), lambda b, st: (0, 0))],
        out_specs=pl.BlockSpec((COMB_TM, D_MODEL), lambda b, st: (b, 0)),
        scratch_shapes=[pltpu.VMEM((2, COMB_STAGE, D_MODEL), F32),
                        pltpu.VMEM((COMB_TM, D_MODEL), F32),
                        pltpu.SemaphoreType.DMA((2,))],
    )
    return pl.pallas_call(
        _combine_kernel,
        grid_spec=grid_spec,
        out_shape=jax.ShapeDtypeStruct((N_ALL, D_MODEL), F32),
        compiler_params=_params(("arbitrary",)),
        name="moe_combine",
    )(starts_flat, ye, pos, x, mod_l)


def _moe_layer(layer, x, g, mod_l, w_router, w_gate, w_up, w_down):
    h, aff = _ffn_prep(x, g, mod_l, w_router.T)
    nb_ctx = N_CTX // LANES
    nb_lat = N_LAT // LANES
    idx_c, gate_c, cend_c, pos_c = _route(aff[:, :N_CTX].reshape(N_EXPERTS, nb_ctx, LANES), CAP_CTX)
    idx_l, gate_l, cend_l, pos_l = _route(aff[:, N_CTX:].reshape(N_EXPERTS, nb_lat, LANES), CAP_LAT)
    idx = jnp.concatenate([idx_c[:, 0, :], idx_l[:, 0, :] + N_CTX], axis=1)
    gate = jnp.concatenate([gate_c[:, 0, :], gate_l[:, 0, :],
                            jnp.zeros((N_EXPERTS, CAP_PAD - CAP_ALL), F32)], axis=1)
    starts = jnp.concatenate([jnp.zeros((N_EXPERTS, 1), I32), cend_c[:, 0, :nb_ctx],
                              cend_l[:, 0, :nb_lat] + CAP_CTX], axis=1)
    pos_l = pos_l.reshape(N_EXPERTS, N_LAT)
    pos = jnp.concatenate([pos_c.reshape(N_EXPERTS, N_CTX),
                           jnp.where(pos_l >= 0.0, pos_l + CAP_CTX, -1.0)], axis=1)
    ye = _moe_ffn(layer, h, idx.reshape(-1), gate.reshape(N_EXPERTS, CAP_PAD, 1), w_gate, w_up, w_down)
    return _combine(ye, starts.reshape(-1), pos, x, mod_l)


def _final_norm_kernel(x_ref, g_ref, o_ref):
    x = x_ref[...]
    ms = jnp.mean(x * x, axis=-1, keepdims=True)
    o_ref[...] = x * lax.rsqrt(ms + NORM_EPS) * g_ref[...]


def _final_norm(x, g, tm=1024):
    return pl.pallas_call(
        _final_norm_kernel,
        grid=(N_ALL // tm,),
        in_specs=[pl.BlockSpec((tm, D_MODEL), lambda m: (m, 0)),
                  pl.BlockSpec((1, D_MODEL), lambda m: (0, 0))],
        out_specs=pl.BlockSpec((tm, D_MODEL), lambda m: (m, 0)),
        out_shape=jax.ShapeDtypeStruct((N_ALL, D_MODEL), F32),
        compiler_params=_params(("arbitrary",)),
        name="final_norm",
    )(x, g)


def _sink_column(sink, rows):
    grp = SWA_HEADS // SWA_KV_HEADS
    return jnp.repeat(sink.reshape(SWA_KV_HEADS, grp), rows, axis=1).reshape(SWA_KV_HEADS, grp * rows, 1)


def _lane_row(v, width=LANES):
    return jnp.zeros((1, width), F32).at[0, :v.shape[0]].set(v)


def kernel(x_prompt, x_sample, cache_na_k, cache_na_v, state_ssd, cache_swa_k, cache_swa_v, c, c_ctx, norm_mix, norm_ffn, w_mod, b_mod, w_in_even, na_rpb, ssd_conv_w, ssd_conv_b, ssd_a_log, ssd_dt_bias, ssd_d, ssd_norm, w_out_even, w_in_odd, swa_sink, w_out_odd, w_router, w_gate, w_up, w_down, final_norm):
    d = D_MODEL
    x = jnp.concatenate([x_prompt.reshape(N_CTX, d), x_sample.reshape(N_LAT, d)], axis=0)
    cond8 = jnp.zeros((8, d), F32).at[0].set(c_ctx).at[1:1 + DEC_BATCH].set(c)
    mod = _adaln(cond8, w_mod, b_mod)
    cos_t, sin_t = _rope_tables()
    new_na_k, new_na_v, new_ssd, new_swa_k, new_swa_v = [], [], [], [], []
    for l in range(DEPTH):
        j = l // 2
        mod_l = mod[l]
        g_mix = norm_mix[l].reshape(1, d)
        if l % 2 == 0:
            w_in = jnp.pad(w_in_even[j], ((0, 0), (0, EVEN_IN_PAD - EVEN_IN))).astype(BF16)
            proj = _proj_in(x, g_mix, mod_l, w_in)
            o_ctx = _ctx_na(proj)
            ck = cache_na_k[:, j].reshape(DEC_BATCH, PAST_LEN, NA_WIDTH)
            cv = cache_na_v[:, j].reshape(DEC_BATCH, PAST_LEN, NA_WIDTH)
            o_lat = _lat_na(proj, ck, cv, _na_col_table(na_rpb[j]))
            o_na = jnp.concatenate([o_ctx, o_lat], axis=0)
            consts = [jnp.pad(ssd_conv_w[j], ((0, 8 - SSD_CONV_W), (0, 0))),
                      ssd_conv_b[j].reshape(1, SSD_CONV_CH),
                      _lane_row(-jnp.exp(ssd_a_log[j].reshape(-1))),
                      _lane_row(ssd_dt_bias[j].reshape(-1)),
                      jnp.repeat(ssd_d[j], SSD_HEAD_DIM).reshape(1, SSD_D_INNER),
                      ssd_norm[j].reshape(1, SSD_D_INNER)]
            y_ctx, st = _ssd(proj, None, consts, seq=SEQ, nbatch=BATCH, row_blk0=0, want_state=True)
            h0t = state_ssd[:, j].transpose(0, 1, 4, 2, 3).reshape(DEC_BATCH, 2, SSD_STATE, SSD_D_INNER)
            y_lat, _ = _ssd(proj, h0t, consts, seq=DEC_SEQ, nbatch=DEC_BATCH, row_blk0=N_CTX // DEC_SEQ,
                            want_state=False)
            y_ssd = jnp.concatenate([y_ctx, y_lat], axis=0)
            x = _out_proj(o_na, 0, y_ssd, 0, w_out_even[j].astype(BF16), x, mod_l)
            new_na_k.append(proj[:N_CTX, NA_WIDTH:2 * NA_WIDTH].reshape(BATCH, SEQ, NA_HEADS, HEAD_DIM))
            new_na_v.append(proj[:N_CTX, 2 * NA_WIDTH:3 * NA_WIDTH].reshape(BATCH, SEQ, NA_HEADS, HEAD_DIM))
            new_ssd.append(st.reshape(BATCH, 2, SSD_STATE, SSD_HEADS, SSD_HEAD_DIM).transpose(0, 1, 3, 4, 2))
        else:
            proj = _proj_in(x, g_mix, mod_l, w_in_odd[j].astype(BF16))
            o_ctx = _ctx_swa(proj, _sink_column(swa_sink[j], SEQ))
            ck = cache_swa_k[:, j].reshape(DEC_BATCH, PAST_LEN, SWA_KV)
            cv = cache_swa_v[:, j].reshape(DEC_BATCH, PAST_LEN, SWA_KV)
            o_lat = _lat_swa(proj, ck, cv, cos_t, sin_t, _sink_column(swa_sink[j], SWA_BLOCK))
            o = jnp.concatenate([o_ctx, o_lat], axis=0)
            x = _out_proj(o, 0, o, 1, w_out_odd[j].astype(BF16), x, mod_l)
            new_swa_k.append(proj[:N_CTX, SWA_Q:SWA_Q + SWA_KV].reshape(BATCH, SEQ, SWA_KV_HEADS, HEAD_DIM))
            new_swa_v.append(proj[:N_CTX, SWA_Q + SWA_KV:].reshape(BATCH, SEQ, SWA_KV_HEADS, HEAD_DIM))
        x = _moe_layer(l, x, norm_ffn[l].reshape(1, d), mod_l, w_router[l], w_gate, w_up, w_down)
    y = _final_norm(x, final_norm.reshape(1, d))
    return (y[:N_CTX].reshape(BATCH, SEQ, d), y[N_CTX:].reshape(DEC_BATCH, DEC_SEQ, d),
            jnp.stack(new_na_k, axis=1), jnp.stack(new_na_v, axis=1), jnp.stack(new_ssd, axis=1),
            jnp.stack(new_swa_k, axis=1), jnp.stack(new_swa_v, axis=1))
```

```python
import functools
import math

import jax
import jax.numpy as jnp
import numpy as np
from jax import lax
from jax.experimental import pallas as pl
from jax.experimental.pallas import tpu as pltpu

F32 = jnp.float32
BF16 = jnp.bfloat16
I32 = jnp.int32
HIGHEST = lax.Precision.HIGHEST

D_MODEL = 1024
BATCH = 16
SEQ = 256
DEPTH = 4
DEC_BATCH = 4
DEC_SEQ = 2048
PAST_LEN = 512
GRID_W = 64
HEAD_DIM = 64
NA_HEADS = 8
NA_KR = 8
NA_KC = 16
SSD_HEADS = 8
SSD_HEAD_DIM = 64
SSD_D_INNER = SSD_HEADS * SSD_HEAD_DIM
SSD_STATE = 128
SSD_GROUPS = 2
SSD_CHUNK = 128
SSD_CONV_W = 5
SWA_HEADS = 16
SWA_KV_HEADS = 4
SWA_WINDOW = 128
SWA_BLOCK = 128
ROPE_BASE = 10000.0
N_EXPERTS = 16
EXPERT_FF = 2048
EC_CAPACITY = 2
NORM_EPS = 1e-6

N_CTX = BATCH * SEQ
N_LAT = DEC_BATCH * DEC_SEQ
N_ALL = N_CTX + N_LAT
NA_WIDTH = NA_HEADS * HEAD_DIM
SSD_BC = SSD_GROUPS * SSD_STATE
SSD_CONV_CH = SSD_D_INNER + 2 * SSD_BC
EVEN_IN = 3 * NA_WIDTH + SSD_D_INNER + SSD_CONV_CH + 2 * SSD_HEADS
EVEN_IN_PAD = 3200
SWA_Q = SWA_HEADS * HEAD_DIM
SWA_KV = SWA_KV_HEADS * HEAD_DIM
ODD_IN = SWA_Q + 2 * SWA_KV
CAP_CTX = EC_CAPACITY * N_CTX // N_EXPERTS
CAP_LAT = EC_CAPACITY * N_LAT // N_EXPERTS
CAP_ALL = CAP_CTX + CAP_LAT
LANES = 128
NEG_BIG = -1e30
ATT_SCALE = HEAD_DIM ** -0.5
VMEM_LIMIT = 56 * 1024 * 1024


def _params(sem):
    return pltpu.CompilerParams(dimension_semantics=sem, vmem_limit_bytes=VMEM_LIMIT)


def _group_of_block(m, tm):
    ctx_blocks = N_CTX // tm
    return lax.select(m < ctx_blocks, jnp.int32(0), 1 + (m - ctx_blocks) // (DEC_SEQ // tm))


def _norm_mod(x, g, shift, scale):
    ms = jnp.mean(x * x, axis=-1, keepdims=True)
    y = x * lax.rsqrt(ms + NORM_EPS) * g
    return y * (1.0 + scale) + shift


def _silu(x):
    return x * jax.nn.sigmoid(x)


def _softplus(x):
    return jnp.maximum(x, 0.0) + jnp.log1p(jnp.exp(-jnp.abs(x)))


def _dot(a, b, **kw):
    return jnp.dot(a, b, preferred_element_type=F32, **kw)


def _dot_nt(a, b, **kw):
    return lax.dot_general(a, b, (((1,), (1,)), ((), ())), preferred_element_type=F32, **kw)


def _adaln_kernel(c_ref, w_ref, b_ref, o_ref):
    o_ref[0] = _dot(_silu(c_ref[...]), w_ref[0], precision=HIGHEST) + b_ref[0]


def _adaln(cond8, w_mod, b_mod):
    d = D_MODEL
    return pl.pallas_call(
        _adaln_kernel,
        grid=(DEPTH, 6),
        in_specs=[pl.BlockSpec((8, d), lambda l, n: (0, 0)),
                  pl.BlockSpec((1, d, d), lambda l, n: (l, 0, n)),
                  pl.BlockSpec((1, 1, d), lambda l, n: (l, 0, n))],
        out_specs=pl.BlockSpec((1, 8, d), lambda l, n: (l, 0, n)),
        out_shape=jax.ShapeDtypeStruct((DEPTH, 8, 6 * d), F32),
        compiler_params=_params(("arbitrary", "arbitrary")),
        name="adaln",
    )(cond8, w_mod, b_mod.reshape(DEPTH, 1, 6 * d))


def _proj_in_kernel(x_ref, g_ref, mod_ref, w_ref, o_ref, *, tm):
    grp = _group_of_block(pl.program_id(0), tm)
    shift = mod_ref[pl.ds(grp, 1), 0:D_MODEL]
    scale = mod_ref[pl.ds(grp, 1), D_MODEL:2 * D_MODEL]
    h = _norm_mod(x_ref[...], g_ref[...], shift, scale).astype(BF16)
    o_ref[...] = _dot(h, w_ref[...])


def _proj_in(x, g, mod_l, w_bf16, tm=512):
    n = w_bf16.shape[1]
    return pl.pallas_call(
        functools.partial(_proj_in_kernel, tm=tm),
        grid=(N_ALL // tm,),
        in_specs=[pl.BlockSpec((tm, D_MODEL), lambda m: (m, 0)),
                  pl.BlockSpec((1, D_MODEL), lambda m: (0, 0)),
                  pl.BlockSpec((8, 6 * D_MODEL), lambda m: (0, 0)),
                  pl.BlockSpec((D_MODEL, n), lambda m: (0, 0))],
        out_specs=pl.BlockSpec((tm, n), lambda m: (m, 0)),
        out_shape=jax.ShapeDtypeStruct((N_ALL, n), F32),
        compiler_params=_params(("arbitrary",)),
        name="proj_in",
    )(x, g, mod_l, w_bf16)


def _out_proj_kernel(ac_ref, al_ref, bc_ref, bl_ref, w_ref, x_ref, mod_ref, o_ref, *, tm):
    m = pl.program_id(0)
    grp = _group_of_block(m, tm)
    gate = mod_ref[pl.ds(grp, 1), 2 * D_MODEL:3 * D_MODEL]
    half = D_MODEL // 2
    is_ctx = m < N_CTX // tm
    a = jnp.where(is_ctx, ac_ref[...], al_ref[...]).astype(BF16)
    b = jnp.where(is_ctx, bc_ref[...], bl_ref[...]).astype(BF16)
    acc = _dot(a, w_ref[0:half, :]) + _dot(b, w_ref[half:, :])
    o_ref[...] = x_ref[...] + gate * acc


def _out_proj(a_ctx, a_lat, acol, b_ctx, b_lat, bcol, w_bf16, x, mod_l, tm=512):
    half = D_MODEL // 2
    nctx = N_CTX // tm

    def ctx_map(col):
        return lambda m: (jnp.minimum(m, nctx - 1), col)

    def lat_map(col):
        return lambda m: (jnp.maximum(m - nctx, 0), col)

    return pl.pallas_call(
        functools.partial(_out_proj_kernel, tm=tm),
        grid=(N_ALL // tm,),
        in_specs=[pl.BlockSpec((tm, half), ctx_map(acol)),
                  pl.BlockSpec((tm, half), lat_map(acol)),
                  pl.BlockSpec((tm, half), ctx_map(bcol)),
                  pl.BlockSpec((tm, half), lat_map(bcol)),
                  pl.BlockSpec((D_MODEL, D_MODEL), lambda m: (0, 0)),
                  pl.BlockSpec((tm, D_MODEL), lambda m: (m, 0)),
                  pl.BlockSpec((8, 6 * D_MODEL), lambda m: (0, 0))],
        out_specs=pl.BlockSpec((tm, D_MODEL), lambda m: (m, 0)),
        out_shape=jax.ShapeDtypeStruct((N_ALL, D_MODEL), F32),
        compiler_params=_params(("arbitrary",)),
        name="out_proj",
    )(a_ctx, a_lat, b_ctx, b_lat, w_bf16, x, mod_l)


def _ctx_na_kernel(q_ref, k_ref, v_ref, o_ref):
    for h in range(NA_HEADS):
        sl = slice(h * HEAD_DIM, (h + 1) * HEAD_DIM)
        q = q_ref[:, sl].astype(BF16)
        k = k_ref[:, sl].astype(BF16)
        v = v_ref[:, sl].astype(BF16)
        s = _dot_nt(q, k) * ATT_SCALE
        p = jnp.exp(s - jnp.max(s, axis=-1, keepdims=True))
        p = p / jnp.sum(p, axis=-1, keepdims=True)
        o_ref[:, sl] = _dot(p.astype(BF16), v)


def _ctx_na(proj):
    w = NA_WIDTH
    return pl.pallas_call(
        _ctx_na_kernel,
        grid=(BATCH,),
        in_specs=[pl.BlockSpec((SEQ, w), lambda b: (b, 0)),
                  pl.BlockSpec((SEQ, w), lambda b: (b, 1)),
                  pl.BlockSpec((SEQ, w), lambda b: (b, 2))],
        out_specs=pl.BlockSpec((SEQ, w), lambda b: (b, 0)),
        out_shape=jax.ShapeDtypeStruct((N_CTX, w), F32),
        compiler_params=_params(("arbitrary",)),
        name="ctx_na",
    )(proj, proj, proj)


NA_QROWS = 4
NA_KROWS = 12
NA_ROWS = DEC_SEQ // GRID_W
NA_GROUPS = NA_ROWS // NA_QROWS


def _na_key_start(g):
    return jnp.clip(NA_QROWS * g - NA_KR // 2, 0, NA_ROWS - NA_KROWS)


def _na_row_geometry(g):
    start = int(np.clip(NA_QROWS * g - NA_KR // 2, 0, NA_ROWS - NA_KROWS))
    rows = [NA_QROWS * g + qr for qr in range(NA_QROWS)]
    return start, [(r, int(np.clip(r - NA_KR // 2, 0, NA_ROWS - NA_KR))) for r in rows]


def _na_fill_bias(ct_ref, bias_scr, g):
    start, rows = _na_row_geometry(g)
    masked = jnp.full((GRID_W, GRID_W), NEG_BIG, F32)
    for i in range(2):
        for qr, (r, rs) in enumerate(rows):
            for kr in range(NA_KROWS):
                keyrow = start + kr
                inside = rs <= keyrow < rs + NA_KR
                tile = ct_ref[i, keyrow - r + NA_KR - 1] if inside else masked
                bias_scr[i, qr * GRID_W:(qr + 1) * GRID_W, kr * GRID_W:(kr + 1) * GRID_W] = tile


def _lat_na_kernel(q_ref, k_ref, v_ref, ck_ref, cv_ref, ct_ref, o_ref, bias_scr, k_scr, v_scr, ck_scr, cv_scr):
    g = pl.program_id(2)
    for g_build in (0, 1, NA_GROUPS - 1):
        @pl.when(g == g_build)
        def _(g_build=g_build):
            _na_fill_bias(ct_ref, bias_scr, g_build)

    @pl.when(g == 0)
    def _():
        for i in range(2):
            sl = slice(i * HEAD_DIM, (i + 1) * HEAD_DIM)
            k_scr[i] = k_ref[:, sl].astype(BF16)
            v_scr[i] = v_ref[:, sl].astype(BF16)
            ck_scr[i] = ck_ref[:, sl].astype(BF16)
            cv_scr[i] = cv_ref[:, sl].astype(BF16)

    start = pl.multiple_of(_na_key_start(g) * GRID_W, GRID_W)
    nk = NA_KROWS * GRID_W
    for i in range(2):
        sl = slice(i * HEAD_DIM, (i + 1) * HEAD_DIM)
        q = (q_ref[:, sl] * ATT_SCALE).astype(BF16)
        s_nb = _dot_nt(q, k_scr[i, pl.ds(start, nk), :]) + bias_scr[i]
        s_ctx = _dot_nt(q, ck_scr[i])
        m = jnp.maximum(jnp.max(s_nb, axis=-1, keepdims=True), jnp.max(s_ctx, axis=-1, keepdims=True))
        p_nb = jnp.exp(s_nb - m)
        p_ctx = jnp.exp(s_ctx - m)
        den = jnp.sum(p_nb, axis=-1, keepdims=True) + jnp.sum(p_ctx, axis=-1, keepdims=True)
        o = _dot(p_nb.astype(BF16), v_scr[i, pl.ds(start, nk), :]) + _dot(p_ctx.astype(BF16), cv_scr[i])
        o_ref[:, sl] = o / den


def _na_col_table(rpb):
    w = np.arange(GRID_W)[:, None]
    cc = np.arange(GRID_W)[None, :]
    cs = np.clip(w - NA_KC // 2, 0, GRID_W - NA_KC)
    valid = (cc >= cs) & (cc < cs + NA_KC)
    dc = cc - w + NA_KC - 1
    onehot = ((dc[..., None] == np.arange(2 * NA_KC - 1)) & valid[..., None]).astype(np.float32)
    ct = jnp.einsum('hrd,wcd->hrwc', rpb, jnp.asarray(onehot), precision=HIGHEST)
    return jnp.where(jnp.asarray(valid)[None, None], ct, NEG_BIG)


def _lat_na(proj, ck, cv, col_tab):
    nq = NA_QROWS * GRID_W
    nk = NA_KROWS * GRID_W
    lat_q0 = N_CTX // nq
    lat_b0 = N_CTX // DEC_SEQ
    return pl.pallas_call(
        _lat_na_kernel,
        grid=(NA_HEADS // 2, DEC_BATCH, NA_GROUPS),
        in_specs=[pl.BlockSpec((nq, LANES), lambda hp, b, g: (lat_q0 + b * NA_GROUPS + g, hp)),
                  pl.BlockSpec((DEC_SEQ, LANES), lambda hp, b, g: (lat_b0 + b, 4 + hp)),
                  pl.BlockSpec((DEC_SEQ, LANES), lambda hp, b, g: (lat_b0 + b, 8 + hp)),
                  pl.BlockSpec((None, PAST_LEN, LANES), lambda hp, b, g: (b, 0, hp)),
                  pl.BlockSpec((None, PAST_LEN, LANES), lambda hp, b, g: (b, 0, hp)),
                  pl.BlockSpec((2, 2 * NA_KR - 1, GRID_W, GRID_W), lambda hp, b, g: (hp, 0, 0, 0))],
        out_specs=pl.BlockSpec((nq, LANES), lambda hp, b, g: (b * NA_GROUPS + g, hp)),
        out_shape=jax.ShapeDtypeStruct((N_LAT, NA_WIDTH), F32),
        scratch_shapes=[pltpu.VMEM((2, nq, nk), F32),
                        pltpu.VMEM((2, DEC_SEQ, HEAD_DIM), BF16),
                        pltpu.VMEM((2, DEC_SEQ, HEAD_DIM), BF16),
                        pltpu.VMEM((2, PAST_LEN, HEAD_DIM), BF16),
                        pltpu.VMEM((2, PAST_LEN, HEAD_DIM), BF16)],
        compiler_params=_params(("arbitrary", "arbitrary", "arbitrary")),
        name="lat_na",
    )(proj, proj, proj, ck, cv, col_tab)


def _ssd_kernel(*refs, seq, has_h0, want_state):
    it = iter(refs)
    z_ref, xbc_ref, dt_ref = next(it), next(it), next(it)
    h0_ref = next(it) if has_h0 else None
    cw_ref, cb_ref, a_ref, dtb_ref, dsk_ref, ng_ref = (next(it) for _ in range(6))
    y_ref = next(it)
    st_ref = next(it) if want_state else None
    xc_scr, y_scr, ht_scr = next(it), next(it), next(it)

    nc = seq // SSD_CHUNK
    ch = SSD_CHUNK
    row = lax.broadcasted_iota(I32, (ch, ch), 0)
    col = lax.broadcasted_iota(I32, (ch, ch), 1)
    erow = lax.broadcasted_iota(I32, (LANES, SSD_D_INNER), 0)
    ecol = lax.broadcasted_iota(I32, (LANES, SSD_D_INNER), 1) // SSD_HEAD_DIM
    cbias = cb_ref[...]

    def conv_chunk(c, carry):
        base = pl.multiple_of(c * ch, ch)
        cur = xbc_ref[pl.ds(base, ch), :]
        pbase = pl.multiple_of(jnp.maximum(base - 8, 0), 8)
        nbase = pl.multiple_of(jnp.minimum(base + ch, seq - 8), 8)
        prev = jnp.where(c > 0, xbc_ref[pl.ds(pbase, 8), :], 0.0)
        nxt = jnp.where(c < nc - 1, xbc_ref[pl.ds(nbase, 8), :], 0.0)
        win = jnp.concatenate([prev, cur, nxt], axis=0)
        acc = jnp.broadcast_to(cbias, (ch, SSD_CONV_CH))
        pad = SSD_CONV_W // 2
        for k in range(SSD_CONV_W):
            off = 8 - pad + k
            acc = acc + win[off:off + ch, :] * cw_ref[k:k + 1, :]
        xc_scr[pl.ds(base, ch), :] = _silu(acc)
        return carry

    lax.fori_loop(0, nc, conv_chunk, 0)

    def run_direction(d):
        lane0 = d * SSD_HEADS
        expand = jnp.where(erow == ecol + lane0, 1.0, 0.0).astype(BF16)
        lmask = (col <= row) if d == 0 else (col >= row)
        tri = jnp.where(lmask, 1.0, 0.0).astype(BF16)

        def dot_split(lhs01, x, lhs_first):
            hi = x.astype(BF16)
            lo = (x - hi.astype(F32)).astype(BF16)
            if lhs_first:
                return _dot(lhs01, hi) + _dot(lhs01, lo)
            return _dot(hi, lhs01) + _dot(lo, lhs01)
        if has_h0:
            ht_scr[...] = h0_ref[d]
        else:
            ht_scr[...] = jnp.zeros((SSD_STATE, SSD_D_INNER), F32)

        def chunk(step, carry):
            c = step if d == 0 else nc - 1 - step
            base = pl.multiple_of(c * ch, ch)
            xs = xc_scr[pl.ds(base, ch), 0:SSD_D_INNER]
            dt = _softplus(dt_ref[pl.ds(base, ch), :] + dtb_ref[...])
            a = dt * a_ref[...]
            cum = dot_split(tri, a, True)
            tot = cum[ch - 1:ch, :] if d == 0 else cum[0:1, :]
            cum_t = cum.T
            dt_x = dot_split(expand, dt, False)
            cum_x = dot_split(expand, cum, False)
            tot_x = dot_split(expand, jnp.broadcast_to(tot, (8, LANES)), False)[0:1, :]
            xt = xs * dt_x
            xd = (xt * jnp.exp(tot_x - cum_x)).astype(BF16)
            xt_b = xt.astype(BF16)
            y_parts = []
            for grp in range(SSD_GROUPS):
                bsl = slice(SSD_D_INNER + grp * SSD_STATE, SSD_D_INNER + (grp + 1) * SSD_STATE)
                csl = slice(SSD_D_INNER + SSD_BC + grp * SSD_STATE, SSD_D_INNER + SSD_BC + (grp + 1) * SSD_STATE)
                b_f = xc_scr[pl.ds(base, ch), bsl]
                b_g = b_f.astype(BF16)
                c_g = xc_scr[pl.ds(base, ch), csl].astype(BF16)
                cb = _dot_nt(c_g, b_g)
                hsl = slice(grp * 4 * SSD_HEAD_DIM, (grp + 1) * 4 * SSD_HEAD_DIM)
                ht_g = ht_scr[:, hsl]
                y_off = _dot(c_g, ht_g.astype(BF16))
                for hh in range(4):
                    head = grp * 4 + hh
                    lane = lane0 + head
                    cum_col = cum[:, lane:lane + 1]
                    cum_row = cum_t[lane:lane + 1, :]
                    ldec = jnp.exp(jnp.where(lmask, cum_col - cum_row, NEG_BIG))
                    psl = slice(head * SSD_HEAD_DIM, (head + 1) * SSD_HEAD_DIM)
                    y_d = _dot((cb * ldec).astype(BF16), xt_b[:, psl])
                    y_parts.append(y_d + y_off[:, hh * SSD_HEAD_DIM:(hh + 1) * SSD_HEAD_DIM] * jnp.exp(cum_col))
                ht_scr[:, hsl] = ht_g * jnp.exp(tot_x[:, hsl]) + _dot(b_f.T.astype(BF16), xd[:, hsl])
            y = jnp.concatenate(y_parts, axis=1)
            if d == 0:
                y_scr[pl.ds(base, ch), :] = y
            else:
                y = y + y_scr[pl.ds(base, ch), :] + dsk_ref[...] * xs
                u = y * _silu(z_ref[pl.ds(base, ch), :])
                ms = jnp.mean(u * u, axis=-1, keepdims=True)
                y_ref[pl.ds(base, ch), :] = u * lax.rsqrt(ms + NORM_EPS) * ng_ref[...]
            return carry

        lax.fori_loop(0, nc, chunk, 0)
        if want_state:
            st_ref[d] = ht_scr[...]

    run_direction(0)
    run_direction(1)


def _ssd(proj, h0t, consts, *, seq, nbatch, row_blk0, want_state):
    has_h0 = h0t is not None
    in_specs = [pl.BlockSpec((seq, SSD_D_INNER), lambda b: (row_blk0 + b, 3)),
                pl.BlockSpec((seq, SSD_CONV_CH), lambda b: (row_blk0 + b, 2)),
                pl.BlockSpec((seq, LANES), lambda b: (row_blk0 + b, 24))]
    args = [proj, proj, proj]
    if has_h0:
        in_specs.append(pl.BlockSpec((None, 2, SSD_STATE, SSD_D_INNER), lambda b: (b, 0, 0, 0)))
        args.append(h0t)
    for cst in consts:
        in_specs.append(pl.BlockSpec(cst.shape, lambda b: (0, 0)))
        args.append(cst)
    out_specs = [pl.BlockSpec((seq, SSD_D_INNER), lambda b: (b, 0))]
    out_shape = [jax.ShapeDtypeStruct((nbatch * seq, SSD_D_INNER), F32)]
    if want_state:
        out_specs.append(pl.BlockSpec((None, 2, SSD_STATE, SSD_D_INNER), lambda b: (b, 0, 0, 0)))
        out_shape.append(jax.ShapeDtypeStruct((nbatch, 2, SSD_STATE, SSD_D_INNER), F32))
    res = pl.pallas_call(
        functools.partial(_ssd_kernel, seq=seq, has_h0=has_h0, want_state=want_state),
        grid=(nbatch,),
        in_specs=in_specs,
        out_specs=out_specs,
        out_shape=out_shape,
        scratch_shapes=[pltpu.VMEM((seq, SSD_CONV_CH), F32),
                        pltpu.VMEM((seq, SSD_D_INNER), F32),
                        pltpu.VMEM((SSD_STATE, SSD_D_INNER), F32)],
        compiler_params=_params(("arbitrary",)),
        name="ssd_seq%d" % seq,
    )(*args)
    return res if want_state else (res[0], None)


def _ctx_swa_kernel(q_ref, k_ref, v_ref, sink_ref, o_ref):
    grp = SWA_HEADS // SWA_KV_HEADS
    for kv in range(SWA_KV_HEADS):
        ksl = slice(kv * HEAD_DIM, (kv + 1) * HEAD_DIM)
        k = k_ref[:, ksl].astype(BF16)
        v = v_ref[:, ksl].astype(BF16)
        q4 = jnp.concatenate(
            [q_ref[:, (kv * grp + j) * HEAD_DIM:(kv * grp + j + 1) * HEAD_DIM] for j in range(grp)],
            axis=0).astype(BF16)
        s = _dot_nt(q4, k) * ATT_SCALE
        sink = sink_ref[kv]
        m = jnp.maximum(jnp.max(s, axis=-1, keepdims=True), sink)
        p = jnp.exp(s - m)
        inv = 1.0 / (jnp.sum(p, axis=-1, keepdims=True) + jnp.exp(sink - m))
        o = _dot((p * inv).astype(BF16), v)
        for j in range(grp):
            o_ref[:, (kv * grp + j) * HEAD_DIM:(kv * grp + j + 1) * HEAD_DIM] = o[j * SEQ:(j + 1) * SEQ]


def _ctx_swa(proj, sink_col):
    return pl.pallas_call(
        _ctx_swa_kernel,
        grid=(BATCH,),
        in_specs=[pl.BlockSpec((SEQ, SWA_Q), lambda b: (b, 0)),
                  pl.BlockSpec((SEQ, SWA_KV), lambda b: (b, 4)),
                  pl.BlockSpec((SEQ, SWA_KV), lambda b: (b, 5)),
                  pl.BlockSpec(sink_col.shape, lambda b: (0, 0, 0))],
        out_specs=pl.BlockSpec((SEQ, SWA_Q), lambda b: (b, 0)),
        out_shape=jax.ShapeDtypeStruct((N_CTX, SWA_Q), F32),
        compiler_params=_params(("arbitrary",)),
        name="ctx_swa",
    )(proj, proj, proj, sink_col)


SWA_NLOC = 3 * SWA_BLOCK


def _rope(x, cos, sin_signed, first):
    n = x.shape[-1]
    partner = jnp.where(first, pltpu.roll(x, n - 16, 1), pltpu.roll(x, 16, 1))
    return x * cos + partner * sin_signed


def _lat_swa_kernel(q_ref, k_ref, v_ref, ck_ref, cv_ref, cosk_ref, sink_k_ref, cosq_ref, sinq_ref, sink_ref,
                    o_ref, kr_scr, v_scr, ck_scr, cv_scr):
    i = pl.program_id(1)
    grp = SWA_HEADS // SWA_KV_HEADS
    lane = lax.broadcasted_iota(I32, (1, SWA_KV), 1)
    first = (lane % 32) < 16

    @pl.when(i == 0)
    def _():
        kr = _rope(k_ref[...], cosk_ref[...], sink_k_ref[...], first)
        for kv in range(SWA_KV_HEADS):
            ksl = slice(kv * HEAD_DIM, (kv + 1) * HEAD_DIM)
            kr_scr[kv] = kr[:, ksl].astype(BF16)
            v_scr[kv] = v_ref[:, ksl].astype(BF16)
            ck_scr[kv] = ck_ref[:, ksl].astype(BF16)
            cv_scr[kv] = cv_ref[:, ksl].astype(BF16)

    kstart = pl.multiple_of(jnp.clip((i - 1) * SWA_BLOCK, 0, DEC_SEQ - SWA_NLOC), SWA_BLOCK)
    qpos = i * SWA_BLOCK + lax.broadcasted_iota(I32, (SWA_BLOCK, SWA_NLOC), 0)
    kpos = kstart + lax.broadcasted_iota(I32, (SWA_BLOCK, SWA_NLOC), 1)
    mask1 = jnp.where(jnp.abs(qpos - kpos) <= SWA_WINDOW, 0.0, NEG_BIG)
    mask = jnp.concatenate([mask1] * grp, axis=0)
    cosq = cosq_ref[...]
    sinq = sinq_ref[...]
    for kv in range(SWA_KV_HEADS):
        qr = _rope(q_ref[:, kv * SWA_KV:(kv + 1) * SWA_KV], cosq, sinq, first) * ATT_SCALE
        q4 = jnp.concatenate([qr[:, j * HEAD_DIM:(j + 1) * HEAD_DIM] for j in range(grp)], axis=0).astype(BF16)
        s_loc = _dot_nt(q4, kr_scr[kv, pl.ds(kstart, SWA_NLOC), :]) + mask
        s_ctx = _dot_nt(q4, ck_scr[kv])
        sink = sink_ref[kv]
        m = jnp.maximum(jnp.maximum(jnp.max(s_loc, axis=-1, keepdims=True),
                                    jnp.max(s_ctx, axis=-1, keepdims=True)), sink)
        p_loc = jnp.exp(s_loc - m)
        p_ctx = jnp.exp(s_ctx - m)
        den = (jnp.sum(p_loc, axis=-1, keepdims=True) + jnp.sum(p_ctx, axis=-1, keepdims=True)
               + jnp.exp(sink - m))
        o = (_dot(p_loc.astype(BF16), v_scr[kv, pl.ds(kstart, SWA_NLOC), :])
             + _dot(p_ctx.astype(BF16), cv_scr[kv])) / den
        for j in range(grp):
            o_ref[:, (kv * grp + j) * HEAD_DIM:(kv * grp + j + 1) * HEAD_DIM] = o[j * SWA_BLOCK:(j + 1) * SWA_BLOCK]


def _rope_tables():
    half = HEAD_DIM // 2
    quarter = half // 2
    pos = jnp.arange(DEC_SEQ)
    inv_freq = 1.0 / (ROPE_BASE ** (jnp.arange(quarter, dtype=F32) * 2.0 / half))
    d = np.arange(HEAD_DIM)
    use_col = jnp.asarray(d >= half)
    p = jnp.where(use_col[None, :], (pos % GRID_W)[:, None], (pos // GRID_W)[:, None]).astype(F32)
    ang = p * inv_freq[d % quarter][None, :]
    sign = jnp.asarray(np.where((d % half) < quarter, -1.0, 1.0), F32)
    cos = jnp.cos(ang)
    sin_signed = jnp.sin(ang) * sign[None, :]
    reps = SWA_KV // HEAD_DIM
    return jnp.tile(cos, (1, reps)), jnp.tile(sin_signed, (1, reps))


def _lat_swa(proj, ck, cv, cos_t, sin_t, sink_col):
    nb = DEC_SEQ // SWA_BLOCK
    q0 = N_CTX // SWA_BLOCK
    b0 = N_CTX // DEC_SEQ
    return pl.pallas_call(
        _lat_swa_kernel,
        grid=(DEC_BATCH, nb),
        in_specs=[pl.BlockSpec((SWA_BLOCK, SWA_Q), lambda b, i: (q0 + b * nb + i, 0)),
                  pl.BlockSpec((DEC_SEQ, SWA_KV), lambda b, i: (b0 + b, 4)),
                  pl.BlockSpec((DEC_SEQ, SWA_KV), lambda b, i: (b0 + b, 5)),
                  pl.BlockSpec((None, PAST_LEN, SWA_KV), lambda b, i: (b, 0, 0)),
                  pl.BlockSpec((None, PAST_LEN, SWA_KV), lambda b, i: (b, 0, 0)),
                  pl.BlockSpec((DEC_SEQ, SWA_KV), lambda b, i: (0, 0)),
                  pl.BlockSpec((DEC_SEQ, SWA_KV), lambda b, i: (0, 0)),
                  pl.BlockSpec((SWA_BLOCK, SWA_KV), lambda b, i: (i, 0)),
                  pl.BlockSpec((SWA_BLOCK, SWA_KV), lambda b, i: (i, 0)),
                  pl.BlockSpec(sink_col.shape, lambda b, i: (0, 0, 0))],
        out_specs=pl.BlockSpec((SWA_BLOCK, SWA_Q), lambda b, i: (b * nb + i, 0)),
        out_shape=jax.ShapeDtypeStruct((N_LAT, SWA_Q), F32),
        scratch_shapes=[pltpu.VMEM((SWA_KV_HEADS, DEC_SEQ, HEAD_DIM), BF16),
                        pltpu.VMEM((SWA_KV_HEADS, DEC_SEQ, HEAD_DIM), BF16),
                        pltpu.VMEM((SWA_KV_HEADS, PAST_LEN, HEAD_DIM), BF16),
                        pltpu.VMEM((SWA_KV_HEADS, PAST_LEN, HEAD_DIM), BF16)],
        compiler_params=_params(("arbitrary", "arbitrary")),
        name="lat_swa",
    )(proj, proj, proj, ck, cv, cos_t, sin_t, cos_t, sin_t, sink_col)


def _ffn_prep_kernel(x_ref, g_ref, mod_ref, wr_ref, h_ref, aff_ref, *, tm):
    grp = _group_of_block(pl.program_id(0), tm)
    shift = mod_ref[pl.ds(grp, 1), 3 * D_MODEL:4 * D_MODEL]
    scale = mod_ref[pl.ds(grp, 1), 4 * D_MODEL:5 * D_MODEL]
    h = _norm_mod(x_ref[...], g_ref[...], shift, scale)
    h_ref[...] = h
    logits = _dot_nt(wr_ref[...], h, precision=HIGHEST)
    e = jnp.exp(logits - jnp.max(logits, axis=0, keepdims=True))
    aff_ref[...] = e / jnp.sum(e, axis=0, keepdims=True)


def _ffn_prep(x, g, mod_l, w_router_t, tm=512):
    return pl.pallas_call(
        functools.partial(_ffn_prep_kernel, tm=tm),
        grid=(N_ALL // tm,),
        in_specs=[pl.BlockSpec((tm, D_MODEL), lambda m: (m, 0)),
                  pl.BlockSpec((1, D_MODEL), lambda m: (0, 0)),
                  pl.BlockSpec((8, 6 * D_MODEL), lambda m: (0, 0)),
                  pl.BlockSpec((N_EXPERTS, D_MODEL), lambda m: (0, 0))],
        out_specs=[pl.BlockSpec((tm, D_MODEL), lambda m: (m, 0)),
                   pl.BlockSpec((N_EXPERTS, tm), lambda m: (0, m))],
        out_shape=[jax.ShapeDtypeStruct((N_ALL, D_MODEL), F32),
                   jax.ShapeDtypeStruct((N_EXPERTS, N_ALL), F32)],
        compiler_params=_params(("arbitrary",)),
        name="ffn_prep",
    )(x, g, mod_l, w_router_t)


def _route_kernel(aff_ref, idx_ref, gate_ref, cend_ref, pos_ref, t_scr, *, nb, cap):
    ne = N_EXPERTS
    capf = float(cap)

    def count_ge(e, cand):
        hit = jnp.where(aff_ref[e] >= cand, 1.0, 0.0)
        return jnp.sum(jnp.sum(hit, axis=1, keepdims=True), axis=0, keepdims=True)

    def bit_step(i, ts):
        bit = jnp.left_shift(jnp.int32(1), 30 - i)
        out = []
        for e in range(ne):
            cand = ts[e] | bit
            keep = count_ge(e, lax.bitcast_convert_type(cand, F32)) >= capf
            out.append(jnp.where(keep, cand, ts[e]))
        return tuple(out)

    ts = lax.fori_loop(0, 31, bit_step, tuple(jnp.zeros((1, 1), I32) for _ in range(ne)))
    for e in range(ne):
        t_scr[e] = jnp.broadcast_to(lax.bitcast_convert_type(ts[e], F32), (8, LANES))

    r = lax.broadcasted_iota(I32, (LANES, LANES), 0)
    c = lax.broadcasted_iota(I32, (LANES, LANES), 1)
    upper = jnp.where(r <= c, 1.0, 0.0).astype(BF16)
    lower_incl = jnp.where(c <= r, 1.0, 0.0).astype(BF16)
    lower_strict = jnp.where(c < r, 1.0, 0.0).astype(BF16)
    npad = LANES - nb
    slot = lax.broadcasted_iota(I32, (LANES, cap), 1).astype(F32)
    sub = lax.broadcasted_iota(I32, (LANES, cap), 0).astype(F32)

    def cumsum_tokens(x):
        rowc = _dot(x.astype(BF16), upper)
        tot = jnp.broadcast_to(rowc[:, LANES - 1:LANES], (LANES, LANES))
        return rowc + _dot(lower_strict, tot.astype(BF16)), tot

    def per_expert(e, carry):
        a = aff_ref[e]
        if npad:
            a = jnp.concatenate([a, jnp.full((npad, LANES), -1.0, F32)], axis=0)
        thr = t_scr[e][0:1, 0:1]
        gt = a > thr
        eq = a == thr
        gtf = jnp.where(gt, 1.0, 0.0)
        eqf = jnp.where(eq, 1.0, 0.0)
        need = capf - jnp.sum(jnp.sum(gtf, axis=1, keepdims=True), axis=0, keepdims=True)
        eq_incl, _ = cumsum_tokens(eqf)
        sel = gt | (eq & (eq_incl - eqf < need))
        self = jnp.where(sel, 1.0, 0.0)
        incl, tot = cumsum_tokens(self)
        cend = _dot(lower_incl, tot.astype(BF16))
        blk = jnp.sum(jnp.where(cend[:, 0:1] <= slot, 1.0, 0.0), axis=0, keepdims=True)
        onehot = jnp.where(sub == blk, 1.0, 0.0)
        hi = jnp.floor(incl * (1.0 / LANES))
        lo = incl - hi * LANES
        inc_s = (_dot(hi.T.astype(BF16), onehot.astype(BF16)) * LANES
                 + _dot(lo.T.astype(BF16), onehot.astype(BF16)))
        within = jnp.sum(jnp.where(inc_s <= slot, 1.0, 0.0), axis=0, keepdims=True)
        aff_s = _dot(jnp.maximum(a, 0.0).T, onehot, precision=HIGHEST)
        gate = jnp.sum(jnp.where(sub == within, aff_s, 0.0), axis=0, keepdims=True)
        idx_ref[e] = (blk * LANES + within).astype(I32)
        gate_ref[e] = gate
        cend_ref[e] = cend.T[0:8, :].astype(I32)
        pos_ref[e] = jnp.where(sel, incl - 1.0, -1.0)[0:nb, :]
        return carry

    lax.fori_loop(0, ne, per_expert, 0)


def _route(aff3, cap):
    ne, nb, _ = aff3.shape
    return pl.pallas_call(
        functools.partial(_route_kernel, nb=nb, cap=cap),
        grid=(1,),
        in_specs=[pl.BlockSpec((ne, nb, LANES), lambda i: (0, 0, 0))],
        out_specs=[pl.BlockSpec((ne, 1, cap), lambda i: (0, 0, 0)),
                   pl.BlockSpec((ne, 1, cap), lambda i: (0, 0, 0)),
                   pl.BlockSpec((ne, 8, LANES), lambda i: (0, 0, 0)),
                   pl.BlockSpec((ne, nb, LANES), lambda i: (0, 0, 0))],
        out_shape=[jax.ShapeDtypeStruct((ne, 1, cap), I32),
                   jax.ShapeDtypeStruct((ne, 1, cap), F32),
                   jax.ShapeDtypeStruct((ne, 8, LANES), I32),
                   jax.ShapeDtypeStruct((ne, nb, LANES), F32)],
        scratch_shapes=[pltpu.VMEM((ne, 8, LANES), F32)],
        compiler_params=_params(("arbitrary",)),
        name="route_cap%d" % cap,
    )(aff3)


FF_TILE = 512
COMB_PIECE = 32
CAP_PAD = CAP_ALL + COMB_PIECE


def _moe_ffn_kernel(idx_ref, h_hbm, gate_ref, wg_ref, wu_ref, wd_ref, o_ref, land, xe, sem):
    e = pl.program_id(0)
    f = pl.program_id(1)
    nf = EXPERT_FF // FF_TILE
    slot = e % 2
    rps = CAP_ALL // nf

    def row_copy(expert, part, s, buf):
        tok = idx_ref[expert * CAP_ALL + part * rps + s]
        return pltpu.make_async_copy(h_hbm.at[pl.ds(tok, 1), :], land.at[buf, part, pl.ds(s, 1), :], sem.at[buf])

    def wait_rows(buf):
        for part in range(nf):
            pltpu.make_async_copy(h_hbm.at[pl.ds(0, rps), :], land.at[buf, part], sem.at[buf]).wait()

    @pl.when(f == 0)
    def _():
        @pl.when(e == 0)
        def _():
            for part in range(nf):
                def issue(s, carry, part=part):
                    row_copy(0, part, s, 0).start()
                    return carry

                lax.fori_loop(0, rps, issue, 0)

        wait_rows(slot)
        for part in range(nf):
            xe[part * rps:(part + 1) * rps, :] = land[slot, part].astype(BF16)
        o_ref[...] = jnp.zeros_like(o_ref)

    nxt = (e + 1) % N_EXPERTS
    for s in range(rps):
        row_copy(nxt, f, s, 1 - slot).start()

    x = xe[...]
    g = _dot(x, wg_ref[...].astype(BF16))
    u = _dot(x, wu_ref[...].astype(BF16))
    hid = (_silu(g) * u).astype(BF16)
    o_ref[0:CAP_ALL, :] += _dot(hid, wd_ref[...].astype(BF16))

    @pl.when(f == nf - 1)
    def _():
        o_ref[...] = o_ref[...] * gate_ref[...]

        @pl.when(e == N_EXPERTS - 1)
        def _():
            wait_rows(1 - slot)


def _moe_ffn(layer, h, idx_flat, gate_col, w_gate, w_up, w_down):
    nf = EXPERT_FF // FF_TILE
    grid_spec = pltpu.PrefetchScalarGridSpec(
        num_scalar_prefetch=1,
        grid=(N_EXPERTS, nf),
        in_specs=[pl.BlockSpec(memory_space=pl.ANY),
                  pl.BlockSpec((None, CAP_PAD, 1), lambda e, f, idx: (e, 0, 0)),
                  pl.BlockSpec((None, None, D_MODEL, FF_TILE), lambda e, f, idx: (layer, e, 0, f)),
                  pl.BlockSpec((None, None, D_MODEL, FF_TILE), lambda e, f, idx: (layer, e, 0, f)),
                  pl.BlockSpec((None, None, FF_TILE, D_MODEL), lambda e, f, idx: (layer, e, f, 0))],
        out_specs=pl.BlockSpec((None, CAP_PAD, D_MODEL), lambda e, f, idx: (e, 0, 0)),
        scratch_shapes=[pltpu.VMEM((2, nf, CAP_ALL // nf, D_MODEL), F32),
                        pltpu.VMEM((CAP_ALL, D_MODEL), BF16),
                        pltpu.SemaphoreType.DMA((2,))],
    )
    return pl.pallas_call(
        _moe_ffn_kernel,
        grid_spec=grid_spec,
        out_shape=jax.ShapeDtypeStruct((N_EXPERTS, CAP_PAD, D_MODEL), F32),
        compiler_params=_params(("arbitrary", "arbitrary")),
        name="moe_ffn",
    )(idx_flat, h, gate_col, w_gate, w_up, w_down)


COMB_TM = 256
COMB_NB = N_ALL // COMB_TM
COMB_HALF = N_EXPERTS // 2
COMB_CHUNK = 256
COMB_STAGE = -(-(COMB_HALF * (COMB_TM + COMB_PIECE + 8)) // COMB_CHUNK) * COMB_CHUNK
ROUTE_NB = N_ALL // LANES


def _combine_kernel(st_ref, ye_hbm, pos_ref, x_ref, mod_ref, o_ref, stage, acc, sem):
    b = pl.program_id(0)
    nst = ROUTE_NB + 1
    per = COMB_TM // LANES

    def layout(blk, half):
        out = []
        off = jnp.int32(0)
        for j in range(COMB_HALF):
            e = half * COMB_HALF + j
            s0 = st_ref[e * nst + per * blk]
            s1 = st_ref[e * nst + per * blk + per]
            a0 = (s0 // 8) * 8
            npc = jnp.where(s1 > s0, (s1 - a0 + COMB_PIECE - 1) // COMB_PIECE, 0)
            out.append((e, a0, off, npc))
            off = off + npc * COMB_PIECE
        return out, off // COMB_PIECE

    def piece_copy(e, src_row, buf, dst_row):
        return pltpu.make_async_copy(ye_hbm.at[e, pl.ds(src_row, COMB_PIECE), :],
                                     stage.at[buf, pl.ds(dst_row, COMB_PIECE), :], sem.at[buf])

    def issue_unit(blk, half):
        lay, _ = layout(blk, half)
        for e, a0, off, npc in lay:
            def issue(k, carry, e=e, a0=a0, off=off):
                piece_copy(e, pl.multiple_of(a0 + k * COMB_PIECE, 8), half,
                           pl.multiple_of(off + k * COMB_PIECE, 8)).start()
                return carry

            lax.fori_loop(0, npc, issue, 0)

    @pl.when(b == 0)
    def _():
        stage[...] = jnp.zeros_like(stage)
        issue_unit(0, 0)

    acc[...] = jnp.zeros_like(acc)
    pos = pos_ref[...]
    row_iota = lax.broadcasted_iota(I32, (COMB_CHUNK, COMB_TM), 0).astype(F32)
    for half in range(2):
        lay, npieces = layout(b, half)

        def wait(k, carry, half=half):
            piece_copy(0, 0, half, 0).wait()
            return carry

        lax.fori_loop(0, npieces, wait, 0)
        if half == 0:
            issue_unit(b, 1)
        else:
            @pl.when(b + 1 < COMB_NB)
            def _():
                issue_unit(b + 1, 0)

        srows = []
        for e, a0, off, npc in lay:
            p = pos[e:e + 1, :]
            srows.append(jnp.where(p >= 0.0, p + (off - a0).astype(F32), -1.0))

        def chunk(ci, carry, half=half, srows=srows):
            r0 = pl.multiple_of(ci * COMB_CHUNK, COMB_CHUNK)
            rid = row_iota + r0.astype(F32)
            hit = jnp.zeros((COMB_CHUNK, COMB_TM), F32)
            for sr in srows:
                hit = jnp.where(sr == rid, 1.0, hit)
            sel_t = hit.T.astype(BF16)
            y = stage[half, pl.ds(r0, COMB_CHUNK), :]
            y_hi = y.astype(BF16)
            y_lo = (y - y_hi.astype(F32)).astype(BF16)
            acc[...] += _dot(sel_t, y_hi) + _dot(sel_t, y_lo)
            return carry

        nchunks = (npieces * COMB_PIECE + COMB_CHUNK - 1) // COMB_CHUNK
        lax.fori_loop(0, nchunks, chunk, 0)

    grp = _group_of_block(b, COMB_TM)
    gate = mod_ref[pl.ds(grp, 1), 5 * D_MODEL:6 * D_MODEL]
    o_ref[...] = x_ref[...] + gate * acc[...]


def _combine(ye, starts_flat, pos, x, mod_l):
    grid_spec = pltpu.PrefetchScalarGridSpec(
        num_scalar_prefetch=1,
        grid=(COMB_NB,),
        in_specs=[pl.BlockSpec(memory_space=pl.ANY),
                  pl.BlockSpec((N_EXPERTS, COMB_TM), lambda b, st: (0, b)),
                  pl.BlockSpec((COMB_TM, D_MODEL), lambda b, st: (b, 0)),
                  pl.BlockSpec((8, 6 * D_MODEL), lambda b, st: (0, 0))],
        out_specs=pl.BlockSpec((COMB_TM, D_MODEL), lambda b, st: (b, 0)),
        scratch_shapes=[pltpu.VMEM((2, COMB_STAGE, D_MODEL), F32),
                        pltpu.VMEM((COMB_TM, D_MODEL), F32),
                        pltpu.SemaphoreType.DMA((2,))],
    )
    return pl.pallas_call(
        _combine_kernel,
        grid_spec=grid_spec,
        out_shape=jax.ShapeDtypeStruct((N_ALL, D_MODEL), F32),
        compiler_params=_params(("arbitrary",)),
        name="moe_combine",
    )(starts_flat, ye, pos, x, mod_l)


def _moe_layer(layer, x, g, mod_l, w_router, w_gate, w_up, w_down):
    h, aff = _ffn_prep(x, g, mod_l, w_router.T)
    nb_ctx = N_CTX // LANES
    nb_lat = N_LAT // LANES
    idx_c, gate_c, cend_c, pos_c = _route(aff[:, :N_CTX].reshape(N_EXPERTS, nb_ctx, LANES), CAP_CTX)
    idx_l, gate_l, cend_l, pos_l = _route(aff[:, N_CTX:].reshape(N_EXPERTS, nb_lat, LANES), CAP_LAT)
    idx = jnp.concatenate([idx_c[:, 0, :], idx_l[:, 0, :] + N_CTX], axis=1)
    gate = jnp.concatenate([gate_c[:, 0, :], gate_l[:, 0, :],
                            jnp.zeros((N_EXPERTS, CAP_PAD - CAP_ALL), F32)], axis=1)
    starts = jnp.concatenate([jnp.zeros((N_EXPERTS, 1), I32), cend_c[:, 0, :nb_ctx],
                              cend_l[:, 0, :nb_lat] + CAP_CTX], axis=1)
    pos_l = pos_l.reshape(N_EXPERTS, N_LAT)
    pos = jnp.concatenate([pos_c.reshape(N_EXPERTS, N_CTX),
                           jnp.where(pos_l >= 0.0, pos_l + CAP_CTX, -1.0)], axis=1)
    ye = _moe_ffn(layer, h, idx.reshape(-1), gate.reshape(N_EXPERTS, CAP_PAD, 1), w_gate, w_up, w_down)
    return _combine(ye, starts.reshape(-1), pos, x, mod_l)


def _final_norm_kernel(x_ref, g_ref, o_ref):
    x = x_ref[...]
    ms = jnp.mean(x * x, axis=-1, keepdims=True)
    o_ref[...] = x * lax.rsqrt(ms + NORM_EPS) * g_ref[...]


def _final_norm(x, g, tm=1024):
    return pl.pallas_call(
        _final_norm_kernel,
        grid=(N_ALL // tm,),
        in_specs=[pl.BlockSpec((tm, D_MODEL), lambda m: (m, 0)),
                  pl.BlockSpec((1, D_MODEL), lambda m: (0, 0))],
        out_specs=pl.BlockSpec((tm, D_MODEL), lambda m: (m, 0)),
        out_shape=jax.ShapeDtypeStruct((N_ALL, D_MODEL), F32),
        compiler_params=_params(("arbitrary",)),
        name="final_norm",
    )(x, g)


def _sink_column(sink, rows):
    grp = SWA_HEADS // SWA_KV_HEADS
    return jnp.repeat(sink.reshape(SWA_KV_HEADS, grp), rows, axis=1).reshape(SWA_KV_HEADS, grp * rows, 1)


def _lane_row(v, width=LANES):
    return jnp.zeros((1, width), F32).at[0, :v.shape[0]].set(v)


def kernel(x_prompt, x_sample, cache_na_k, cache_na_v, state_ssd, cache_swa_k, cache_swa_v, c, c_ctx, norm_mix, norm_ffn, w_mod, b_mod, w_in_even, na_rpb, ssd_conv_w, ssd_conv_b, ssd_a_log, ssd_dt_bias, ssd_d, ssd_norm, w_out_even, w_in_odd, swa_sink, w_out_odd, w_router, w_gate, w_up, w_down, final_norm):
    d = D_MODEL
    x = jnp.concatenate([x_prompt.reshape(N_CTX, d), x_sample.reshape(N_LAT, d)], axis=0)
    cond8 = jnp.zeros((8, d), F32).at[0].set(c_ctx).at[1:1 + DEC_BATCH].set(c)
    mod = _adaln(cond8, w_mod, b_mod)
    cos_t, sin_t = _rope_tables()
    new_na_k, new_na_v, new_ssd, new_swa_k, new_swa_v = [], [], [], [], []
    for l in range(DEPTH):
        j = l // 2
        mod_l = mod[l]
        g_mix = norm_mix[l].reshape(1, d)
        if l % 2 == 0:
            w_in = jnp.pad(w_in_even[j], ((0, 0), (0, EVEN_IN_PAD - EVEN_IN))).astype(BF16)
            proj = _proj_in(x, g_mix, mod_l, w_in)
            o_ctx = _ctx_na(proj)
            ck = cache_na_k[:, j].reshape(DEC_BATCH, PAST_LEN, NA_WIDTH)
            cv = cache_na_v[:, j].reshape(DEC_BATCH, PAST_LEN, NA_WIDTH)
            o_lat = _lat_na(proj, ck, cv, _na_col_table(na_rpb[j]))
            consts = [jnp.pad(ssd_conv_w[j], ((0, 8 - SSD_CONV_W), (0, 0))),
                      ssd_conv_b[j].reshape(1, SSD_CONV_CH),
                      _lane_row(-jnp.exp(ssd_a_log[j].reshape(-1))),
                      _lane_row(ssd_dt_bias[j].reshape(-1)),
                      jnp.repeat(ssd_d[j], SSD_HEAD_DIM).reshape(1, SSD_D_INNER),
                      ssd_norm[j].reshape(1, SSD_D_INNER)]
            y_ctx, st = _ssd(proj, None, consts, seq=SEQ, nbatch=BATCH, row_blk0=0, want_state=True)
            h0t = state_ssd[:, j].transpose(0, 1, 4, 2, 3).reshape(DEC_BATCH, 2, SSD_STATE, SSD_D_INNER)
            y_lat, _ = _ssd(proj, h0t, consts, seq=DEC_SEQ, nbatch=DEC_BATCH, row_blk0=N_CTX // DEC_SEQ,
                            want_state=False)
            x = _out_proj(o_ctx, o_lat, 0, y_ctx, y_lat, 0, w_out_even[j].astype(BF16), x, mod_l)
            new_na_k.append(proj[:N_CTX, NA_WIDTH:2 * NA_WIDTH].reshape(BATCH, SEQ, NA_HEADS, HEAD_DIM))
            new_na_v.append(proj[:N_CTX, 2 * NA_WIDTH:3 * NA_WIDTH].reshape(BATCH, SEQ, NA_HEADS, HEAD_DIM))
            new_ssd.append(st.reshape(BATCH, 2, SSD_STATE, SSD_HEADS, SSD_HEAD_DIM).transpose(0, 1, 3, 4, 2))
        else:
            proj = _proj_in(x, g_mix, mod_l, w_in_odd[j].astype(BF16))
            o_ctx = _ctx_swa(proj, _sink_column(swa_sink[j], SEQ))
            ck = cache_swa_k[:, j].reshape(DEC_BATCH, PAST_LEN, SWA_KV)
            cv = cache_swa_v[:, j].reshape(DEC_BATCH, PAST_LEN, SWA_KV)
            o_lat = _lat_swa(proj, ck, cv, cos_t, sin_t, _sink_column(swa_sink[j], SWA_BLOCK))
            x = _out_proj(o_ctx, o_lat, 0, o_ctx, o_lat, 1, w_out_odd[j].astype(BF16), x, mod_l)
            new_swa_k.append(proj[:N_CTX, SWA_Q:SWA_Q + SWA_KV].reshape(BATCH, SEQ, SWA_KV_HEADS, HEAD_DIM))
            new_swa_v.append(proj[:N_CTX, SWA_Q + SWA_KV:].reshape(BATCH, SEQ, SWA_KV_HEADS, HEAD_DIM))
        x = _moe_layer(l, x, norm_ffn[l].reshape(1, d), mod_l, w_router[l], w_gate, w_up, w_down)
    y = _final_norm(x, final_norm.reshape(1, d))
    return (y[:N_CTX].reshape(BATCH, SEQ, d), y[N_CTX:].reshape(DEC_BATCH, DEC_SEQ, d),
            jnp.stack(new_na_k, axis=1), jnp.stack(new_na_v, axis=1), jnp.stack(new_ssd, axis=1),
            jnp.stack(new_swa_k, axis=1), jnp.stack(new_swa_v, axis=1))
```

```python
import functools
import math

import jax
import jax.numpy as jnp
import numpy as np
from jax import lax
from jax.experimental import pallas as pl
from jax.experimental.pallas import tpu as pltpu

F32 = jnp.float32
BF16 = jnp.bfloat16
I32 = jnp.int32
HIGHEST = lax.Precision.HIGHEST

D_MODEL = 1024
BATCH = 16
SEQ = 256
DEPTH = 4
DEC_BATCH = 4
DEC_SEQ = 2048
PAST_LEN = 512
GRID_W = 64
HEAD_DIM = 64
NA_HEADS = 8
NA_KR = 8
NA_KC = 16
SSD_HEADS = 8
SSD_HEAD_DIM = 64
SSD_D_INNER = SSD_HEADS * SSD_HEAD_DIM
SSD_STATE = 128
SSD_GROUPS = 2
SSD_CHUNK = 128
SSD_CONV_W = 5
SWA_HEADS = 16
SWA_KV_HEADS = 4
SWA_WINDOW = 128
SWA_BLOCK = 128
ROPE_BASE = 10000.0
N_EXPERTS = 16
EXPERT_FF = 2048
EC_CAPACITY = 2
NORM_EPS = 1e-6

N_CTX = BATCH * SEQ
N_LAT = DEC_BATCH * DEC_SEQ
N_ALL = N_CTX + N_LAT
NA_WIDTH = NA_HEADS * HEAD_DIM
SSD_BC = SSD_GROUPS * SSD_STATE
SSD_CONV_CH = SSD_D_INNER + 2 * SSD_BC
EVEN_IN = 3 * NA_WIDTH + SSD_D_INNER + SSD_CONV_CH + 2 * SSD_HEADS
EVEN_IN_PAD = 3200
SWA_Q = SWA_HEADS * HEAD_DIM
SWA_KV = SWA_KV_HEADS * HEAD_DIM
ODD_IN = SWA_Q + 2 * SWA_KV
CAP_CTX = EC_CAPACITY * N_CTX // N_EXPERTS
CAP_LAT = EC_CAPACITY * N_LAT // N_EXPERTS
CAP_ALL = CAP_CTX + CAP_LAT
LANES = 128
NEG_BIG = -1e30
ATT_SCALE = HEAD_DIM ** -0.5
VMEM_LIMIT = 56 * 1024 * 1024


def _params(sem):
    return pltpu.CompilerParams(dimension_semantics=sem, vmem_limit_bytes=VMEM_LIMIT)


def _group_of_block(m, tm):
    ctx_blocks = N_CTX // tm
    return lax.select(m < ctx_blocks, jnp.int32(0), 1 + (m - ctx_blocks) // (DEC_SEQ // tm))


def _norm_mod(x, g, shift, scale):
    ms = jnp.mean(x * x, axis=-1, keepdims=True)
    y = x * lax.rsqrt(ms + NORM_EPS) * g
    return y * (1.0 + scale) + shift


def _silu(x):
    return x * jax.nn.sigmoid(x)


def _softplus(x):
    return jnp.maximum(x, 0.0) + jnp.log1p(jnp.exp(-jnp.abs(x)))


def _dot(a, b, **kw):
    return jnp.dot(a, b, preferred_element_type=F32, **kw)


def _dot_nt(a, b, **kw):
    return lax.dot_general(a, b, (((1,), (1,)), ((), ())), preferred_element_type=F32, **kw)


def _value_with_ones(v_tile, upper_half):
    lane = lax.broadcasted_iota(I32, v_tile.shape, 1)
    v = pltpu.roll(v_tile, HEAD_DIM, 1) if upper_half else v_tile
    return jnp.where(lane < HEAD_DIM, v, 1.0).astype(BF16)


def _softmax_pv(q, keys, vals, biases, extra=None):
    logits = []
    for k, b in zip(keys, biases):
        s = _dot_nt(q, k)
        logits.append(s if b is None else s + b)
    mx = extra
    for s in logits:
        for c in range(s.shape[1] // LANES):
            t = s[:, c * LANES:(c + 1) * LANES]
            mx = t if mx is None else jnp.maximum(mx, t)
    m = jnp.max(mx, axis=-1, keepdims=True)
    acc = None
    for s, v in zip(logits, vals):
        pv = _dot(jnp.exp(s - m).astype(BF16), v)
        acc = pv if acc is None else acc + pv
    den = pltpu.roll(acc, HEAD_DIM, 1)
    if extra is not None:
        den = den + jnp.exp(extra - m)
    return (acc / den)[:, 0:HEAD_DIM]


def _adaln_kernel(c_ref, w_ref, b_ref, o_ref):
    o_ref[0] = _dot(_silu(c_ref[...]), w_ref[0], precision=HIGHEST) + b_ref[0]


def _adaln(cond8, w_mod, b_mod):
    d = D_MODEL
    return pl.pallas_call(
        _adaln_kernel,
        grid=(DEPTH, 6),
        in_specs=[pl.BlockSpec((8, d), lambda l, n: (0, 0)),
                  pl.BlockSpec((1, d, d), lambda l, n: (l, 0, n)),
                  pl.BlockSpec((1, 1, d), lambda l, n: (l, 0, n))],
        out_specs=pl.BlockSpec((1, 8, d), lambda l, n: (l, 0, n)),
        out_shape=jax.ShapeDtypeStruct((DEPTH, 8, 6 * d), F32),
        compiler_params=_params(("arbitrary", "arbitrary")),
        name="adaln",
    )(cond8, w_mod, b_mod.reshape(DEPTH, 1, 6 * d))


def _proj_in_kernel(x_ref, g_ref, mod_ref, w_ref, o_ref, *, tm):
    grp = _group_of_block(pl.program_id(0), tm)
    shift = mod_ref[pl.ds(grp, 1), 0:D_MODEL]
    scale = mod_ref[pl.ds(grp, 1), D_MODEL:2 * D_MODEL]
    h = _norm_mod(x_ref[...], g_ref[...], shift, scale).astype(BF16)
    o_ref[...] = _dot(h, w_ref[...])


def _proj_in(x, g, mod_l, w_bf16, tm=512):
    n = w_bf16.shape[1]
    return pl.pallas_call(
        functools.partial(_proj_in_kernel, tm=tm),
        grid=(N_ALL // tm,),
        in_specs=[pl.BlockSpec((tm, D_MODEL), lambda m: (m, 0)),
                  pl.BlockSpec((1, D_MODEL), lambda m: (0, 0)),
                  pl.BlockSpec((8, 6 * D_MODEL), lambda m: (0, 0)),
                  pl.BlockSpec((D_MODEL, n), lambda m: (0, 0))],
        out_specs=pl.BlockSpec((tm, n), lambda m: (m, 0)),
        out_shape=jax.ShapeDtypeStruct((N_ALL, n), F32),
        compiler_params=_params(("arbitrary",)),
        name="proj_in",
    )(x, g, mod_l, w_bf16)


def _out_proj_kernel(ac_ref, al_ref, bc_ref, bl_ref, w_ref, x_ref, mod_ref, o_ref, *, tm):
    m = pl.program_id(0)
    grp = _group_of_block(m, tm)
    gate = mod_ref[pl.ds(grp, 1), 2 * D_MODEL:3 * D_MODEL]
    half = D_MODEL // 2
    is_ctx = m < N_CTX // tm
    a = jnp.where(is_ctx, ac_ref[...], al_ref[...]).astype(BF16)
    b = jnp.where(is_ctx, bc_ref[...], bl_ref[...]).astype(BF16)
    acc = _dot(a, w_ref[0:half, :]) + _dot(b, w_ref[half:, :])
    o_ref[...] = x_ref[...] + gate * acc


def _out_proj(a_ctx, a_lat, acol, b_ctx, b_lat, bcol, w_bf16, x, mod_l, tm=512):
    half = D_MODEL // 2
    nctx = N_CTX // tm

    def ctx_map(col):
        return lambda m: (jnp.minimum(m, nctx - 1), col)

    def lat_map(col):
        return lambda m: (jnp.maximum(m - nctx, 0), col)

    return pl.pallas_call(
        functools.partial(_out_proj_kernel, tm=tm),
        grid=(N_ALL // tm,),
        in_specs=[pl.BlockSpec((tm, half), ctx_map(acol)),
                  pl.BlockSpec((tm, half), lat_map(acol)),
                  pl.BlockSpec((tm, half), ctx_map(bcol)),
                  pl.BlockSpec((tm, half), lat_map(bcol)),
                  pl.BlockSpec((D_MODEL, D_MODEL), lambda m: (0, 0)),
                  pl.BlockSpec((tm, D_MODEL), lambda m: (m, 0)),
                  pl.BlockSpec((8, 6 * D_MODEL), lambda m: (0, 0))],
        out_specs=pl.BlockSpec((tm, D_MODEL), lambda m: (m, 0)),
        out_shape=jax.ShapeDtypeStruct((N_ALL, D_MODEL), F32),
        compiler_params=_params(("arbitrary",)),
        name="out_proj",
    )(a_ctx, a_lat, b_ctx, b_lat, w_bf16, x, mod_l)


def _ctx_na_kernel(q_ref, k_ref, v_ref, o_ref):
    for h in range(NA_HEADS):
        sl = slice(h * HEAD_DIM, (h + 1) * HEAD_DIM)
        q = q_ref[:, sl].astype(BF16)
        k = k_ref[:, sl].astype(BF16)
        v = v_ref[:, sl].astype(BF16)
        s = _dot_nt(q, k) * ATT_SCALE
        p = jnp.exp(s - jnp.max(s, axis=-1, keepdims=True))
        p = p / jnp.sum(p, axis=-1, keepdims=True)
        o_ref[:, sl] = _dot(p.astype(BF16), v)


def _ctx_na(proj):
    w = NA_WIDTH
    return pl.pallas_call(
        _ctx_na_kernel,
        grid=(BATCH,),
        in_specs=[pl.BlockSpec((SEQ, w), lambda b: (b, 0)),
                  pl.BlockSpec((SEQ, w), lambda b: (b, 1)),
                  pl.BlockSpec((SEQ, w), lambda b: (b, 2))],
        out_specs=pl.BlockSpec((SEQ, w), lambda b: (b, 0)),
        out_shape=jax.ShapeDtypeStruct((N_CTX, w), F32),
        compiler_params=_params(("arbitrary",)),
        name="ctx_na",
    )(proj, proj, proj)


NA_QROWS = 4
NA_KROWS = 12
NA_ROWS = DEC_SEQ // GRID_W
NA_GROUPS = NA_ROWS // NA_QROWS


def _na_key_start(g):
    return jnp.clip(NA_QROWS * g - NA_KR // 2, 0, NA_ROWS - NA_KROWS)


def _na_row_geometry(g):
    start = int(np.clip(NA_QROWS * g - NA_KR // 2, 0, NA_ROWS - NA_KROWS))
    rows = [NA_QROWS * g + qr for qr in range(NA_QROWS)]
    return start, [(r, int(np.clip(r - NA_KR // 2, 0, NA_ROWS - NA_KR))) for r in rows]


def _na_fill_bias(ct_ref, bias_scr, g):
    start, rows = _na_row_geometry(g)
    masked = jnp.full((GRID_W, GRID_W), NEG_BIG, F32)
    for i in range(NA_HEADS):
        for qr, (r, rs) in enumerate(rows):
            for kr in range(NA_KROWS):
                keyrow = start + kr
                inside = rs <= keyrow < rs + NA_KR
                tile = ct_ref[i, keyrow - r + NA_KR - 1] if inside else masked
                bias_scr[i, qr * GRID_W:(qr + 1) * GRID_W, kr * GRID_W:(kr + 1) * GRID_W] = tile


def _lat_na_kernel(q_ref, k_ref, v_ref, ck_ref, cv_ref, ct_ref, o_ref, bias_scr, k_scr, v_scr, ck_scr, cv_scr):
    g = pl.program_id(1)
    for g_build in (0, 1, NA_GROUPS - 1):
        @pl.when(g == g_build)
        def _(g_build=g_build):
            _na_fill_bias(ct_ref, bias_scr, g_build)

    @pl.when(g == 0)
    def _():
        for i in range(NA_HEADS):
            sl = slice(i * HEAD_DIM, (i + 1) * HEAD_DIM)
            tile = slice((i // 2) * LANES, (i // 2 + 1) * LANES)
            k_scr[i] = k_ref[:, sl].astype(BF16)
            v_scr[i] = _value_with_ones(v_ref[:, tile], i % 2 == 1)
            ck_scr[i] = ck_ref[:, sl].astype(BF16)
            cv_scr[i] = _value_with_ones(cv_ref[:, tile], i % 2 == 1)

    start = pl.multiple_of(_na_key_start(g) * GRID_W, GRID_W)
    nk = NA_KROWS * GRID_W
    for i in range(NA_HEADS):
        sl = slice(i * HEAD_DIM, (i + 1) * HEAD_DIM)
        q = (q_ref[:, sl] * ATT_SCALE).astype(BF16)
        o_ref[:, sl] = _softmax_pv(q, [k_scr[i, pl.ds(start, nk), :], ck_scr[i]],
                                   [v_scr[i, pl.ds(start, nk), :], cv_scr[i]], [bias_scr[i], None])


def _na_col_table(rpb):
    w = np.arange(GRID_W)[:, None]
    cc = np.arange(GRID_W)[None, :]
    cs = np.clip(w - NA_KC // 2, 0, GRID_W - NA_KC)
    valid = (cc >= cs) & (cc < cs + NA_KC)
    dc = cc - w + NA_KC - 1
    onehot = ((dc[..., None] == np.arange(2 * NA_KC - 1)) & valid[..., None]).astype(np.float32)
    ct = jnp.einsum('hrd,wcd->hrwc', rpb, jnp.asarray(onehot), precision=HIGHEST)
    return jnp.where(jnp.asarray(valid)[None, None], ct, NEG_BIG)


def _lat_na(proj, ck, cv, col_tab):
    nq = NA_QROWS * GRID_W
    nk = NA_KROWS * GRID_W
    lat_q0 = N_CTX // nq
    lat_b0 = N_CTX // DEC_SEQ
    w = NA_WIDTH
    return pl.pallas_call(
        _lat_na_kernel,
        grid=(DEC_BATCH, NA_GROUPS),
        in_specs=[pl.BlockSpec((nq, w), lambda b, g: (lat_q0 + b * NA_GROUPS + g, 0)),
                  pl.BlockSpec((DEC_SEQ, w), lambda b, g: (lat_b0 + b, 1)),
                  pl.BlockSpec((DEC_SEQ, w), lambda b, g: (lat_b0 + b, 2)),
                  pl.BlockSpec((None, PAST_LEN, w), lambda b, g: (b, 0, 0)),
                  pl.BlockSpec((None, PAST_LEN, w), lambda b, g: (b, 0, 0)),
                  pl.BlockSpec((NA_HEADS, 2 * NA_KR - 1, GRID_W, GRID_W), lambda b, g: (0, 0, 0, 0))],
        out_specs=pl.BlockSpec((nq, w), lambda b, g: (b * NA_GROUPS + g, 0)),
        out_shape=jax.ShapeDtypeStruct((N_LAT, w), F32),
        scratch_shapes=[pltpu.VMEM((NA_HEADS, nq, nk), F32),
                        pltpu.VMEM((NA_HEADS, DEC_SEQ, HEAD_DIM), BF16),
                        pltpu.VMEM((NA_HEADS, DEC_SEQ, LANES), BF16),
                        pltpu.VMEM((NA_HEADS, PAST_LEN, HEAD_DIM), BF16),
                        pltpu.VMEM((NA_HEADS, PAST_LEN, LANES), BF16)],
        compiler_params=_params(("arbitrary", "arbitrary")),
        name="lat_na",
    )(proj, proj, proj, ck, cv, col_tab)


def _ssd_kernel(*refs, seq, has_h0, want_state):
    it = iter(refs)
    z_ref, xbc_ref, dt_ref = next(it), next(it), next(it)
    h0_ref = next(it) if has_h0 else None
    cw_ref, cb_ref, a_ref, dtb_ref, dsk_ref, ng_ref = (next(it) for _ in range(6))
    y_ref = next(it)
    st_ref = next(it) if want_state else None
    xc_scr, y_scr, ht_scr = next(it), next(it), next(it)

    nc = seq // SSD_CHUNK
    ch = SSD_CHUNK
    row = lax.broadcasted_iota(I32, (ch, ch), 0)
    col = lax.broadcasted_iota(I32, (ch, ch), 1)
    erow = lax.broadcasted_iota(I32, (LANES, SSD_D_INNER), 0)
    ecol = lax.broadcasted_iota(I32, (LANES, SSD_D_INNER), 1) // SSD_HEAD_DIM
    cbias = cb_ref[...]

    def conv_chunk(c, carry):
        base = pl.multiple_of(c * ch, ch)
        cur = xbc_ref[pl.ds(base, ch), :]
        pbase = pl.multiple_of(jnp.maximum(base - 8, 0), 8)
        nbase = pl.multiple_of(jnp.minimum(base + ch, seq - 8), 8)
        prev = jnp.where(c > 0, xbc_ref[pl.ds(pbase, 8), :], 0.0)
        nxt = jnp.where(c < nc - 1, xbc_ref[pl.ds(nbase, 8), :], 0.0)
        win = jnp.concatenate([prev, cur, nxt], axis=0)
        acc = jnp.broadcast_to(cbias, (ch, SSD_CONV_CH))
        pad = SSD_CONV_W // 2
        for k in range(SSD_CONV_W):
            off = 8 - pad + k
            acc = acc + win[off:off + ch, :] * cw_ref[k:k + 1, :]
        xc_scr[pl.ds(base, ch), :] = _silu(acc)
        return carry

    lax.fori_loop(0, nc, conv_chunk, 0)

    def run_direction(d):
        lane0 = d * SSD_HEADS
        expand = jnp.where(erow == ecol + lane0, 1.0, 0.0).astype(BF16)
        lmask = (col <= row) if d == 0 else (col >= row)
        tri = jnp.where(lmask, 1.0, 0.0).astype(BF16)

        def dot_split(lhs01, x, lhs_first):
            hi = x.astype(BF16)
            lo = (x - hi.astype(F32)).astype(BF16)
            if lhs_first:
                return _dot(lhs01, hi) + _dot(lhs01, lo)
            return _dot(hi, lhs01) + _dot(lo, lhs01)
        if has_h0:
            ht_scr[...] = h0_ref[d]
        else:
            ht_scr[...] = jnp.zeros((SSD_STATE, SSD_D_INNER), F32)

        def chunk(step, carry):
            c = step if d == 0 else nc - 1 - step
            base = pl.multiple_of(c * ch, ch)
            xs = xc_scr[pl.ds(base, ch), 0:SSD_D_INNER]
            dt = _softplus(dt_ref[pl.ds(base, ch), :] + dtb_ref[...])
            a = dt * a_ref[...]
            cum = dot_split(tri, a, True)
            tot = cum[ch - 1:ch, :] if d == 0 else cum[0:1, :]
            cum_t = cum.T
            dt_x = dot_split(expand, dt, False)
            cum_x = dot_split(expand, cum, False)
            tot_x = dot_split(expand, jnp.broadcast_to(tot, (8, LANES)), False)[0:1, :]
            xt = xs * dt_x
            xd = (xt * jnp.exp(tot_x - cum_x)).astype(BF16)
            xt_b = xt.astype(BF16)
            y_parts = []
            for grp in range(SSD_GROUPS):
                bsl = slice(SSD_D_INNER + grp * SSD_STATE, SSD_D_INNER + (grp + 1) * SSD_STATE)
                csl = slice(SSD_D_INNER + SSD_BC + grp * SSD_STATE, SSD_D_INNER + SSD_BC + (grp + 1) * SSD_STATE)
                b_f = xc_scr[pl.ds(base, ch), bsl]
                b_g = b_f.astype(BF16)
                c_g = xc_scr[pl.ds(base, ch), csl].astype(BF16)
                cb = _dot_nt(c_g, b_g)
                hsl = slice(grp * 4 * SSD_HEAD_DIM, (grp + 1) * 4 * SSD_HEAD_DIM)
                ht_g = ht_scr[:, hsl]
                y_off = _dot(c_g, ht_g.astype(BF16))
                for hh in range(4):
                    head = grp * 4 + hh
                    lane = lane0 + head
                    cum_col = cum[:, lane:lane + 1]
                    cum_row = cum_t[lane:lane + 1, :]
                    ldec = jnp.exp(jnp.where(lmask, cum_col - cum_row, NEG_BIG))
                    psl = slice(head * SSD_HEAD_DIM, (head + 1) * SSD_HEAD_DIM)
                    y_d = _dot((cb * ldec).astype(BF16), xt_b[:, psl])
                    y_parts.append(y_d + y_off[:, hh * SSD_HEAD_DIM:(hh + 1) * SSD_HEAD_DIM] * jnp.exp(cum_col))
                ht_scr[:, hsl] = ht_g * jnp.exp(tot_x[:, hsl]) + _dot(b_f.T.astype(BF16), xd[:, hsl])
            y = jnp.concatenate(y_parts, axis=1)
            if d == 0:
                y_scr[pl.ds(base, ch), :] = y
            else:
                y = y + y_scr[pl.ds(base, ch), :] + dsk_ref[...] * xs
                u = y * _silu(z_ref[pl.ds(base, ch), :])
                ms = jnp.mean(u * u, axis=-1, keepdims=True)
                y_ref[pl.ds(base, ch), :] = u * lax.rsqrt(ms + NORM_EPS) * ng_ref[...]
            return carry

        lax.fori_loop(0, nc, chunk, 0)
        if want_state:
            st_ref[d] = ht_scr[...]

    run_direction(0)
    run_direction(1)


def _ssd(proj, h0t, consts, *, seq, nbatch, row_blk0, want_state):
    has_h0 = h0t is not None
    in_specs = [pl.BlockSpec((seq, SSD_D_INNER), lambda b: (row_blk0 + b, 3)),
                pl.BlockSpec((seq, SSD_CONV_CH), lambda b: (row_blk0 + b, 2)),
                pl.BlockSpec((seq, LANES), lambda b: (row_blk0 + b, 24))]
    args = [proj, proj, proj]
    if has_h0:
        in_specs.append(pl.BlockSpec((None, 2, SSD_STATE, SSD_D_INNER), lambda b: (b, 0, 0, 0)))
        args.append(h0t)
    for cst in consts:
        in_specs.append(pl.BlockSpec(cst.shape, lambda b: (0, 0)))
        args.append(cst)
    out_specs = [pl.BlockSpec((seq, SSD_D_INNER), lambda b: (b, 0))]
    out_shape = [jax.ShapeDtypeStruct((nbatch * seq, SSD_D_INNER), F32)]
    if want_state:
        out_specs.append(pl.BlockSpec((None, 2, SSD_STATE, SSD_D_INNER), lambda b: (b, 0, 0, 0)))
        out_shape.append(jax.ShapeDtypeStruct((nbatch, 2, SSD_STATE, SSD_D_INNER), F32))
    res = pl.pallas_call(
        functools.partial(_ssd_kernel, seq=seq, has_h0=has_h0, want_state=want_state),
        grid=(nbatch,),
        in_specs=in_specs,
        out_specs=out_specs,
        out_shape=out_shape,
        scratch_shapes=[pltpu.VMEM((seq, SSD_CONV_CH), F32),
                        pltpu.VMEM((seq, SSD_D_INNER), F32),
                        pltpu.VMEM((SSD_STATE, SSD_D_INNER), F32)],
        compiler_params=_params(("arbitrary",)),
        name="ssd_seq%d" % seq,
    )(*args)
    return res if want_state else (res[0], None)


def _ctx_swa_kernel(q_ref, k_ref, v_ref, sink_ref, o_ref):
    grp = SWA_HEADS // SWA_KV_HEADS
    for kv in range(SWA_KV_HEADS):
        ksl = slice(kv * HEAD_DIM, (kv + 1) * HEAD_DIM)
        tile = slice((kv // 2) * LANES, (kv // 2 + 1) * LANES)
        k = k_ref[:, ksl].astype(BF16)
        v = _value_with_ones(v_ref[:, tile], kv % 2 == 1)
        q4 = jnp.concatenate(
            [q_ref[:, (kv * grp + j) * HEAD_DIM:(kv * grp + j + 1) * HEAD_DIM] for j in range(grp)],
            axis=0)
        o = _softmax_pv((q4 * ATT_SCALE).astype(BF16), [k], [v], [None], extra=sink_ref[kv])
        for j in range(grp):
            o_ref[:, (kv * grp + j) * HEAD_DIM:(kv * grp + j + 1) * HEAD_DIM] = o[j * SEQ:(j + 1) * SEQ]


def _ctx_swa(proj, sink_col):
    return pl.pallas_call(
        _ctx_swa_kernel,
        grid=(BATCH,),
        in_specs=[pl.BlockSpec((SEQ, SWA_Q), lambda b: (b, 0)),
                  pl.BlockSpec((SEQ, SWA_KV), lambda b: (b, 4)),
                  pl.BlockSpec((SEQ, SWA_KV), lambda b: (b, 5)),
                  pl.BlockSpec(sink_col.shape, lambda b: (0, 0, 0))],
        out_specs=pl.BlockSpec((SEQ, SWA_Q), lambda b: (b, 0)),
        out_shape=jax.ShapeDtypeStruct((N_CTX, SWA_Q), F32),
        compiler_params=_params(("arbitrary",)),
        name="ctx_swa",
    )(proj, proj, proj, sink_col)


SWA_NLOC = 3 * SWA_BLOCK


def _rope(x, cos, sin_signed, first):
    n = x.shape[-1]
    partner = jnp.where(first, pltpu.roll(x, n - 16, 1), pltpu.roll(x, 16, 1))
    return x * cos + partner * sin_signed


def _lat_swa_kernel(q_ref, k_ref, v_ref, ck_ref, cv_ref, cosk_ref, sink_k_ref, cosq_ref, sinq_ref, sink_ref,
                    o_ref, kr_scr, v_scr, ck_scr, cv_scr):
    i = pl.program_id(1)
    grp = SWA_HEADS // SWA_KV_HEADS
    lane = lax.broadcasted_iota(I32, (1, SWA_KV), 1)
    first = (lane % 32) < 16

    @pl.when(i == 0)
    def _():
        kr = _rope(k_ref[...], cosk_ref[...], sink_k_ref[...], first)
        for kv in range(SWA_KV_HEADS):
            ksl = slice(kv * HEAD_DIM, (kv + 1) * HEAD_DIM)
            tile = slice((kv // 2) * LANES, (kv // 2 + 1) * LANES)
            kr_scr[kv] = kr[:, ksl].astype(BF16)
            v_scr[kv] = _value_with_ones(v_ref[:, tile], kv % 2 == 1)
            ck_scr[kv] = ck_ref[:, ksl].astype(BF16)
            cv_scr[kv] = _value_with_ones(cv_ref[:, tile], kv % 2 == 1)

    kstart = pl.multiple_of(jnp.clip((i - 1) * SWA_BLOCK, 0, DEC_SEQ - SWA_NLOC), SWA_BLOCK)
    qpos = i * SWA_BLOCK + lax.broadcasted_iota(I32, (SWA_BLOCK, SWA_NLOC), 0)
    kpos = kstart + lax.broadcasted_iota(I32, (SWA_BLOCK, SWA_NLOC), 1)
    mask1 = jnp.where(jnp.abs(qpos - kpos) <= SWA_WINDOW, 0.0, NEG_BIG)
    mask = jnp.concatenate([mask1] * grp, axis=0)
    cosq = cosq_ref[...]
    sinq = sinq_ref[...]
    for kv in range(SWA_KV_HEADS):
        qr = _rope(q_ref[:, kv * SWA_KV:(kv + 1) * SWA_KV], cosq, sinq, first) * ATT_SCALE
        q4 = jnp.concatenate([qr[:, j * HEAD_DIM:(j + 1) * HEAD_DIM] for j in range(grp)], axis=0).astype(BF16)
        o = _softmax_pv(q4, [kr_scr[kv, pl.ds(kstart, SWA_NLOC), :], ck_scr[kv]],
                        [v_scr[kv, pl.ds(kstart, SWA_NLOC), :], cv_scr[kv]], [mask, None], extra=sink_ref[kv])
        for j in range(grp):
            o_ref[:, (kv * grp + j) * HEAD_DIM:(kv * grp + j + 1) * HEAD_DIM] = o[j * SWA_BLOCK:(j + 1) * SWA_BLOCK]


def _rope_tables():
    half = HEAD_DIM // 2
    quarter = half // 2
    pos = jnp.arange(DEC_SEQ)
    inv_freq = 1.0 / (ROPE_BASE ** (jnp.arange(quarter, dtype=F32) * 2.0 / half))
    d = np.arange(HEAD_DIM)
    use_col = jnp.asarray(d >= half)
    p = jnp.where(use_col[None, :], (pos % GRID_W)[:, None], (pos // GRID_W)[:, None]).astype(F32)
    ang = p * inv_freq[d % quarter][None, :]
    sign = jnp.asarray(np.where((d % half) < quarter, -1.0, 1.0), F32)
    cos = jnp.cos(ang)
    sin_signed = jnp.sin(ang) * sign[None, :]
    reps = SWA_KV // HEAD_DIM
    return jnp.tile(cos, (1, reps)), jnp.tile(sin_signed, (1, reps))


def _lat_swa(proj, ck, cv, cos_t, sin_t, sink_col):
    nb = DEC_SEQ // SWA_BLOCK
    q0 = N_CTX // SWA_BLOCK
    b0 = N_CTX // DEC_SEQ
    return pl.pallas_call(
        _lat_swa_kernel,
        grid=(DEC_BATCH, nb),
        in_specs=[pl.BlockSpec((SWA_BLOCK, SWA_Q), lambda b, i: (q0 + b * nb + i, 0)),
                  pl.BlockSpec((DEC_SEQ, SWA_KV), lambda b, i: (b0 + b, 4)),
                  pl.BlockSpec((DEC_SEQ, SWA_KV), lambda b, i: (b0 + b, 5)),
                  pl.BlockSpec((None, PAST_LEN, SWA_KV), lambda b, i: (b, 0, 0)),
                  pl.BlockSpec((None, PAST_LEN, SWA_KV), lambda b, i: (b, 0, 0)),
                  pl.BlockSpec((DEC_SEQ, SWA_KV), lambda b, i: (0, 0)),
                  pl.BlockSpec((DEC_SEQ, SWA_KV), lambda b, i: (0, 0)),
                  pl.BlockSpec((SWA_BLOCK, SWA_KV), lambda b, i: (i, 0)),
                  pl.BlockSpec((SWA_BLOCK, SWA_KV), lambda b, i: (i, 0)),
                  pl.BlockSpec(sink_col.shape, lambda b, i: (0, 0, 0))],
        out_specs=pl.BlockSpec((SWA_BLOCK, SWA_Q), lambda b, i: (b * nb + i, 0)),
        out_shape=jax.ShapeDtypeStruct((N_LAT, SWA_Q), F32),
        scratch_shapes=[pltpu.VMEM((SWA_KV_HEADS, DEC_SEQ, HEAD_DIM), BF16),
                        pltpu.VMEM((SWA_KV_HEADS, DEC_SEQ, LANES), BF16),
                        pltpu.VMEM((SWA_KV_HEADS, PAST_LEN, HEAD_DIM), BF16),
                        pltpu.VMEM((SWA_KV_HEADS, PAST_LEN, LANES), BF16)],
        compiler_params=_params(("arbitrary", "arbitrary")),
        name="lat_swa",
    )(proj, proj, proj, ck, cv, cos_t, sin_t, cos_t, sin_t, sink_col)


def _ffn_prep_kernel(x_ref, g_ref, mod_ref, wr_ref, h_ref, aff_ref, *, tm):
    grp = _group_of_block(pl.program_id(0), tm)
    shift = mod_ref[pl.ds(grp, 1), 3 * D_MODEL:4 * D_MODEL]
    scale = mod_ref[pl.ds(grp, 1), 4 * D_MODEL:5 * D_MODEL]
    h = _norm_mod(x_ref[...], g_ref[...], shift, scale)
    h_ref[...] = h
    logits = _dot_nt(wr_ref[...], h, precision=HIGHEST)
    e = jnp.exp(logits - jnp.max(logits, axis=0, keepdims=True))
    aff_ref[...] = e / jnp.sum(e, axis=0, keepdims=True)


def _ffn_prep(x, g, mod_l, w_router_t, tm=512):
    return pl.pallas_call(
        functools.partial(_ffn_prep_kernel, tm=tm),
        grid=(N_ALL // tm,),
        in_specs=[pl.BlockSpec((tm, D_MODEL), lambda m: (m, 0)),
                  pl.BlockSpec((1, D_MODEL), lambda m: (0, 0)),
                  pl.BlockSpec((8, 6 * D_MODEL), lambda m: (0, 0)),
                  pl.BlockSpec((N_EXPERTS, D_MODEL), lambda m: (0, 0))],
        out_specs=[pl.BlockSpec((tm, D_MODEL), lambda m: (m, 0)),
                   pl.BlockSpec((N_EXPERTS, tm), lambda m: (0, m))],
        out_shape=[jax.ShapeDtypeStruct((N_ALL, D_MODEL), F32),
                   jax.ShapeDtypeStruct((N_EXPERTS, N_ALL), F32)],
        compiler_params=_params(("arbitrary",)),
        name="ffn_prep",
    )(x, g, mod_l, w_router_t)


def _route_kernel(aff_ref, idx_ref, gate_ref, cend_ref, pos_ref, t_scr, *, nb, cap):
    ne = N_EXPERTS
    capf = float(cap)

    def count_ge(e, cand):
        hit = jnp.where(aff_ref[e] >= cand, 1.0, 0.0)
        return jnp.sum(jnp.sum(hit, axis=1, keepdims=True), axis=0, keepdims=True)

    def bit_step(i, ts):
        bit = jnp.left_shift(jnp.int32(1), 30 - i)
        out = []
        for e in range(ne):
            cand = ts[e] | bit
            keep = count_ge(e, lax.bitcast_convert_type(cand, F32)) >= capf
            out.append(jnp.where(keep, cand, ts[e]))
        return tuple(out)

    ts = lax.fori_loop(0, 31, bit_step, tuple(jnp.zeros((1, 1), I32) for _ in range(ne)))
    for e in range(ne):
        t_scr[e] = jnp.broadcast_to(lax.bitcast_convert_type(ts[e], F32), (8, LANES))

    r = lax.broadcasted_iota(I32, (LANES, LANES), 0)
    c = lax.broadcasted_iota(I32, (LANES, LANES), 1)
    upper = jnp.where(r <= c, 1.0, 0.0).astype(BF16)
    lower_incl = jnp.where(c <= r, 1.0, 0.0).astype(BF16)
    lower_strict = jnp.where(c < r, 1.0, 0.0).astype(BF16)
    npad = LANES - nb
    slot = lax.broadcasted_iota(I32, (LANES, cap), 1).astype(F32)
    sub = lax.broadcasted_iota(I32, (LANES, cap), 0).astype(F32)

    def cumsum_tokens(x):
        rowc = _dot(x.astype(BF16), upper)
        tot = jnp.broadcast_to(rowc[:, LANES - 1:LANES], (LANES, LANES))
        return rowc + _dot(lower_strict, tot.astype(BF16)), tot

    def per_expert(e, carry):
        a = aff_ref[e]
        if npad:
            a = jnp.concatenate([a, jnp.full((npad, LANES), -1.0, F32)], axis=0)
        thr = t_scr[e][0:1, 0:1]
        gt = a > thr
        eq = a == thr
        gtf = jnp.where(gt, 1.0, 0.0)
        eqf = jnp.where(eq, 1.0, 0.0)
        need = capf - jnp.sum(jnp.sum(gtf, axis=1, keepdims=True), axis=0, keepdims=True)
        eq_incl, _ = cumsum_tokens(eqf)
        sel = gt | (eq & (eq_incl - eqf < need))
        self = jnp.where(sel, 1.0, 0.0)
        incl, tot = cumsum_tokens(self)
        cend = _dot(lower_incl, tot.astype(BF16))
        blk = jnp.sum(jnp.where(cend[:, 0:1] <= slot, 1.0, 0.0), axis=0, keepdims=True)
        onehot = jnp.where(sub == blk, 1.0, 0.0)
        hi = jnp.floor(incl * (1.0 / LANES))
        lo = incl - hi * LANES
        inc_s = (_dot(hi.T.astype(BF16), onehot.astype(BF16)) * LANES
                 + _dot(lo.T.astype(BF16), onehot.astype(BF16)))
        within = jnp.sum(jnp.where(inc_s <= slot, 1.0, 0.0), axis=0, keepdims=True)
        aff_s = _dot(jnp.maximum(a, 0.0).T, onehot, precision=HIGHEST)
        gate = jnp.sum(jnp.where(sub == within, aff_s, 0.0), axis=0, keepdims=True)
        idx_ref[e] = (blk * LANES + within).astype(I32)
        gate_ref[e] = gate
        cend_ref[e] = cend.T[0:8, :].astype(I32)
        pos_ref[e] = jnp.where(sel, incl - 1.0, -1.0)[0:nb, :]
        return carry

    lax.fori_loop(0, ne, per_expert, 0)


def _route(aff3, cap):
    ne, nb, _ = aff3.shape
    return pl.pallas_call(
        functools.partial(_route_kernel, nb=nb, cap=cap),
        grid=(1,),
        in_specs=[pl.BlockSpec((ne, nb, LANES), lambda i: (0, 0, 0))],
        out_specs=[pl.BlockSpec((ne, 1, cap), lambda i: (0, 0, 0)),
                   pl.BlockSpec((ne, 1, cap), lambda i: (0, 0, 0)),
                   pl.BlockSpec((ne, 8, LANES), lambda i: (0, 0, 0)),
                   pl.BlockSpec((ne, nb, LANES), lambda i: (0, 0, 0))],
        out_shape=[jax.ShapeDtypeStruct((ne, 1, cap), I32),
                   jax.ShapeDtypeStruct((ne, 1, cap), F32),
                   jax.ShapeDtypeStruct((ne, 8, LANES), I32),
                   jax.ShapeDtypeStruct((ne, nb, LANES), F32)],
        scratch_shapes=[pltpu.VMEM((ne, 8, LANES), F32)],
        compiler_params=_params(("arbitrary",)),
        name="route_cap%d" % cap,
    )(aff3)


FF_TILE = 512
COMB_PIECE = 32
CAP_PAD = CAP_ALL + COMB_PIECE


def _moe_ffn_kernel(idx_ref, h_hbm, gate_ref, wg_ref, wu_ref, wd_ref, ohi_ref, olo_ref, land, xe, acc, sem):
    e = pl.program_id(0)
    f = pl.program_id(1)
    nf = EXPERT_FF // FF_TILE
    rps = CAP_ALL // nf

    def row_copy(expert, part, s):
        tok = idx_ref[expert * CAP_ALL + part * rps + s]
        return pltpu.make_async_copy(h_hbm.at[pl.ds(tok, 1), :], land.at[part, pl.ds(s, 1), :], sem)

    def wait_rows():
        for part in range(nf):
            pltpu.make_async_copy(h_hbm.at[pl.ds(0, rps), :], land.at[part], sem).wait()

    @pl.when(f == 0)
    def _():
        @pl.when(e == 0)
        def _():
            for part in range(nf):
                def issue(s, carry, part=part):
                    row_copy(0, part, s).start()
                    return carry

                lax.fori_loop(0, rps, issue, 0)

        wait_rows()
        for part in range(nf):
            xe[part * rps:(part + 1) * rps, :] = land[part].astype(BF16)
        acc[...] = jnp.zeros_like(acc)

    nxt = (e + 1) % N_EXPERTS
    for s in range(rps):
        row_copy(nxt, f, s).start()

    x = xe[...]
    g = _dot(x, wg_ref[...].astype(BF16))
    u = _dot(x, wu_ref[...].astype(BF16))
    hid = (_silu(g) * u).astype(BF16)
    acc[...] += _dot(hid, wd_ref[...].astype(BF16))

    @pl.when(f == nf - 1)
    def _():
        y = acc[...] * gate_ref[...]
        hi = y.astype(BF16)
        ohi_ref[0:CAP_ALL, :] = hi
        olo_ref[0:CAP_ALL, :] = (y - hi.astype(F32)).astype(BF16)
        tail = jnp.zeros((CAP_PAD - CAP_ALL, D_MODEL), BF16)
        ohi_ref[CAP_ALL:CAP_PAD, :] = tail
        olo_ref[CAP_ALL:CAP_PAD, :] = tail

        @pl.when(e == N_EXPERTS - 1)
        def _():
            wait_rows()


def _moe_ffn(layer, h, idx_flat, gate_col, w_gate, w_up, w_down):
    nf = EXPERT_FF // FF_TILE
    out_spec = pl.BlockSpec((None, CAP_PAD, D_MODEL), lambda e, f, idx: (e, 0, 0))
    grid_spec = pltpu.PrefetchScalarGridSpec(
        num_scalar_prefetch=1,
        grid=(N_EXPERTS, nf),
        in_specs=[pl.BlockSpec(memory_space=pl.ANY),
                  pl.BlockSpec((None, CAP_ALL, 1), lambda e, f, idx: (e, 0, 0)),
                  pl.BlockSpec((None, None, D_MODEL, FF_TILE), lambda e, f, idx: (layer, e, 0, f)),
                  pl.BlockSpec((None, None, D_MODEL, FF_TILE), lambda e, f, idx: (layer, e, 0, f)),
                  pl.BlockSpec((None, None, FF_TILE, D_MODEL), lambda e, f, idx: (layer, e, f, 0))],
        out_specs=[out_spec, out_spec],
        scratch_shapes=[pltpu.VMEM((nf, CAP_ALL // nf, D_MODEL), F32),
                        pltpu.VMEM((CAP_ALL, D_MODEL), BF16),
                        pltpu.VMEM((CAP_ALL, D_MODEL), F32),
                        pltpu.SemaphoreType.DMA(())],
    )
    out = jax.ShapeDtypeStruct((N_EXPERTS, CAP_PAD, D_MODEL), BF16)
    return pl.pallas_call(
        _moe_ffn_kernel,
        grid_spec=grid_spec,
        out_shape=[out, out],
        compiler_params=_params(("arbitrary", "arbitrary")),
        name="moe_ffn",
    )(idx_flat, h, gate_col, w_gate, w_up, w_down)


COMB_TM = 256
COMB_NB = N_ALL // COMB_TM
COMB_HALF = N_EXPERTS // 2
COMB_CHUNK = 256
COMB_ALIGN = 16
COMB_STAGE = -(-(COMB_HALF * (COMB_TM + COMB_PIECE + COMB_ALIGN)) // COMB_CHUNK) * COMB_CHUNK
ROUTE_NB = N_ALL // LANES


def _combine_kernel(st_ref, yhi_hbm, ylo_hbm, pos_ref, x_ref, mod_ref, o_ref, stage_hi, stage_lo, acc, sem):
    b = pl.program_id(0)
    nst = ROUTE_NB + 1
    per = COMB_TM // LANES

    def layout(blk, half):
        out = []
        off = jnp.int32(0)
        for j in range(COMB_HALF):
            e = half * COMB_HALF + j
            s0 = st_ref[e * nst + per * blk]
            s1 = st_ref[e * nst + per * blk + per]
            a0 = (s0 // COMB_ALIGN) * COMB_ALIGN
            npc = jnp.where(s1 > s0, (s1 - a0 + COMB_PIECE - 1) // COMB_PIECE, 0)
            out.append((e, a0, off, npc))
            off = off + npc * COMB_PIECE
        return out, off // COMB_PIECE

    def piece_copies(e, src_row, buf, dst_row):
        return [pltpu.make_async_copy(src.at[e, pl.ds(src_row, COMB_PIECE), :],
                                      dst.at[buf, pl.ds(dst_row, COMB_PIECE), :], sem.at[buf])
                for src, dst in ((yhi_hbm, stage_hi), (ylo_hbm, stage_lo))]

    def issue_unit(blk, half):
        lay, _ = layout(blk, half)
        for e, a0, off, npc in lay:
            def issue(k, carry, e=e, a0=a0, off=off):
                for cp in piece_copies(e, pl.multiple_of(a0 + k * COMB_PIECE, COMB_ALIGN), half,
                                       pl.multiple_of(off + k * COMB_PIECE, COMB_ALIGN)):
                    cp.start()
                return carry

            lax.fori_loop(0, npc, issue, 0)

    @pl.when(b == 0)
    def _():
        stage_hi[...] = jnp.zeros_like(stage_hi)
        stage_lo[...] = jnp.zeros_like(stage_lo)
        issue_unit(0, 0)

    acc[...] = jnp.zeros_like(acc)
    pos = pos_ref[...]
    row_t = lax.broadcasted_iota(I32, (COMB_CHUNK, COMB_TM), 0).astype(F32)
    row_l = lax.broadcasted_iota(I32, (COMB_CHUNK, LANES), 0).astype(F32)
    lane = lax.broadcasted_iota(I32, (1, LANES), 1)
    for half in range(2):
        lay, npieces = layout(b, half)

        def wait(k, carry, half=half):
            for cp in piece_copies(0, 0, half, 0):
                cp.wait()
            return carry

        lax.fori_loop(0, npieces, wait, 0)
        if half == 0:
            issue_unit(b, 1)
        else:
            @pl.when(b + 1 < COMB_NB)
            def _():
                issue_unit(b + 1, 0)

        srows = []
        first = jnp.full((1, LANES), 1e9, F32)
        last = jnp.zeros((1, LANES), F32)
        for j, (e, a0, off, npc) in enumerate(lay):
            p = pos[e:e + 1, :]
            srows.append(jnp.where(p >= 0.0, p + (off - a0).astype(F32), -1.0))
            first = jnp.where(lane == j, off.astype(F32), first)
            last = jnp.where(lane == j, (off + npc * COMB_PIECE).astype(F32), last)
        srow = jnp.concatenate(srows + [jnp.zeros((LANES - COMB_HALF, COMB_TM), F32)], axis=0)
        srow_hi = jnp.floor(srow * (1.0 / 64.0))
        srow_lo = (srow - 64.0 * srow_hi).astype(BF16)
        srow_hi = srow_hi.astype(BF16)

        def select_rows(ci, srow_hi=srow_hi, srow_lo=srow_lo, first=first, last=last):
            r0f = (ci * COMB_CHUNK).astype(F32)
            rid = row_l + r0f
            owner = jnp.where((rid >= first) & (rid < last), 1.0, 0.0).astype(BF16)
            want = 64.0 * _dot(owner, srow_hi) + _dot(owner, srow_lo)
            return jnp.where(want == row_t + r0f, 1.0, 0.0).T.astype(BF16)

        def chunk(ci, sel_t, half=half, select_rows=select_rows):
            sel_next = select_rows(ci + 1)
            r0 = pl.multiple_of(ci * COMB_CHUNK, COMB_CHUNK)
            acc[...] += (_dot(sel_t, stage_hi[half, pl.ds(r0, COMB_CHUNK), :])
                         + _dot(sel_t, stage_lo[half, pl.ds(r0, COMB_CHUNK), :]))
            return sel_next

        nchunks = (npieces * COMB_PIECE + COMB_CHUNK - 1) // COMB_CHUNK
        lax.fori_loop(0, nchunks, chunk, select_rows(jnp.int32(0)))

    grp = _group_of_block(b, COMB_TM)
    gate = mod_ref[pl.ds(grp, 1), 5 * D_MODEL:6 * D_MODEL]
    o_ref[...] = x_ref[...] + gate * acc[...]


def _combine(ye_hi, ye_lo, starts_flat, pos, x, mod_l):
    grid_spec = pltpu.PrefetchScalarGridSpec(
        num_scalar_prefetch=1,
        grid=(COMB_NB,),
        in_specs=[pl.BlockSpec(memory_space=pl.ANY),
                  pl.BlockSpec(memory_space=pl.ANY),
                  pl.BlockSpec((N_EXPERTS, COMB_TM), lambda b, st: (0, b)),
                  pl.BlockSpec((COMB_TM, D_MODEL), lambda b, st: (b, 0)),
                  pl.BlockSpec((8, 6 * D_MODEL), lambda b, st: (0, 0))],
        out_specs=pl.BlockSpec((COMB_TM, D_MODEL), lambda b, st: (b, 0)),
        scratch_shapes=[pltpu.VMEM((2, COMB_STAGE, D_MODEL), BF16),
                        pltpu.VMEM((2, COMB_STAGE, D_MODEL), BF16),
                        pltpu.VMEM((COMB_TM, D_MODEL), F32),
                        pltpu.SemaphoreType.DMA((2,))],
    )
    return pl.pallas_call(
        _combine_kernel,
        grid_spec=grid_spec,
        out_shape=jax.ShapeDtypeStruct((N_ALL, D_MODEL), F32),
        compiler_params=_params(("arbitrary",)),
        name="moe_combine",
    )(starts_flat, ye_hi, ye_lo, pos, x, mod_l)


def _moe_layer(layer, x, g, mod_l, w_router, w_gate, w_up, w_down):
    h, aff = _ffn_prep(x, g, mod_l, w_router.T)
    nb_ctx = N_CTX // LANES
    nb_lat = N_LAT // LANES
    idx_c, gate_c, cend_c, pos_c = _route(aff[:, :N_CTX].reshape(N_EXPERTS, nb_ctx, LANES), CAP_CTX)
    idx_l, gate_l, cend_l, pos_l = _route(aff[:, N_CTX:].reshape(N_EXPERTS, nb_lat, LANES), CAP_LAT)
    idx = jnp.concatenate([idx_c[:, 0, :], idx_l[:, 0, :] + N_CTX], axis=1)
    gate = jnp.concatenate([gate_c[:, 0, :], gate_l[:, 0, :]], axis=1)
    starts = jnp.concatenate([jnp.zeros((N_EXPERTS, 1), I32), cend_c[:, 0, :nb_ctx],
                              cend_l[:, 0, :nb_lat] + CAP_CTX], axis=1)
    pos_l = pos_l.reshape(N_EXPERTS, N_LAT)
    pos = jnp.concatenate([pos_c.reshape(N_EXPERTS, N_CTX),
                           jnp.where(pos_l >= 0.0, pos_l + CAP_CTX, -1.0)], axis=1)
    ye_hi, ye_lo = _moe_ffn(layer, h, idx.reshape(-1), gate.reshape(N_EXPERTS, CAP_ALL, 1), w_gate, w_up, w_down)
    return _combine(ye_hi, ye_lo, starts.reshape(-1), pos, x, mod_l)


def _final_norm_kernel(x_ref, g_ref, oc_ref, ol_ref, *, tm):
    x = x_ref[...]
    ms = jnp.mean(x * x, axis=-1, keepdims=True)
    y = x * lax.rsqrt(ms + NORM_EPS) * g_ref[...]
    is_ctx = pl.program_id(0) < N_CTX // tm

    @pl.when(is_ctx)
    def _():
        oc_ref[...] = y

    @pl.when(jnp.logical_not(is_ctx))
    def _():
        ol_ref[...] = y


def _final_norm(x, g, tm=1024):
    nctx = N_CTX // tm
    return pl.pallas_call(
        functools.partial(_final_norm_kernel, tm=tm),
        grid=(N_ALL // tm,),
        in_specs=[pl.BlockSpec((tm, D_MODEL), lambda m: (m, 0)),
                  pl.BlockSpec((1, D_MODEL), lambda m: (0, 0))],
        out_specs=[pl.BlockSpec((tm, D_MODEL), lambda m: (jnp.minimum(m, nctx - 1), 0)),
                   pl.BlockSpec((tm, D_MODEL), lambda m: (jnp.maximum(m - nctx, 0), 0))],
        out_shape=[jax.ShapeDtypeStruct((N_CTX, D_MODEL), F32),
                   jax.ShapeDtypeStruct((N_LAT, D_MODEL), F32)],
        compiler_params=_params(("arbitrary",)),
        name="final_norm",
    )(x, g)


def _sink_rows(sink, rows):
    grp = SWA_HEADS // SWA_KV_HEADS
    col = jnp.repeat(sink.reshape(SWA_KV_HEADS, grp), rows, axis=1).reshape(SWA_KV_HEADS, grp * rows, 1)
    return jnp.broadcast_to(col, (SWA_KV_HEADS, grp * rows, LANES))


def _lane_row(v, width=LANES):
    return jnp.zeros((1, width), F32).at[0, :v.shape[0]].set(v)


def kernel(x_prompt, x_sample, cache_na_k, cache_na_v, state_ssd, cache_swa_k, cache_swa_v, c, c_ctx, norm_mix, norm_ffn, w_mod, b_mod, w_in_even, na_rpb, ssd_conv_w, ssd_conv_b, ssd_a_log, ssd_dt_bias, ssd_d, ssd_norm, w_out_even, w_in_odd, swa_sink, w_out_odd, w_router, w_gate, w_up, w_down, final_norm):
    d = D_MODEL
    x = jnp.concatenate([x_prompt.reshape(N_CTX, d), x_sample.reshape(N_LAT, d)], axis=0)
    cond8 = jnp.zeros((8, d), F32).at[0].set(c_ctx).at[1:1 + DEC_BATCH].set(c)
    mod = _adaln(cond8, w_mod, b_mod)
    cos_t, sin_t = _rope_tables()
    new_na_k, new_na_v, new_ssd, new_swa_k, new_swa_v = [], [], [], [], []
    for l in range(DEPTH):
        j = l // 2
        mod_l = mod[l]
        g_mix = norm_mix[l].reshape(1, d)
        if l % 2 == 0:
            w_in = jnp.pad(w_in_even[j], ((0, 0), (0, EVEN_IN_PAD - EVEN_IN))).astype(BF16)
            proj = _proj_in(x, g_mix, mod_l, w_in)
            o_ctx = _ctx_na(proj)
            ck = cache_na_k[:, j].reshape(DEC_BATCH, PAST_LEN, NA_WIDTH)
            cv = cache_na_v[:, j].reshape(DEC_BATCH, PAST_LEN, NA_WIDTH)
            o_lat = _lat_na(proj, ck, cv, _na_col_table(na_rpb[j]))
            consts = [jnp.pad(ssd_conv_w[j], ((0, 8 - SSD_CONV_W), (0, 0))),
                      ssd_conv_b[j].reshape(1, SSD_CONV_CH),
                      _lane_row(-jnp.exp(ssd_a_log[j].reshape(-1))),
                      _lane_row(ssd_dt_bias[j].reshape(-1)),
                      jnp.repeat(ssd_d[j], SSD_HEAD_DIM).reshape(1, SSD_D_INNER),
                      ssd_norm[j].reshape(1, SSD_D_INNER)]
            y_ctx, st = _ssd(proj, None, consts, seq=SEQ, nbatch=BATCH, row_blk0=0, want_state=True)
            h0t = state_ssd[:, j].transpose(0, 1, 4, 2, 3).reshape(DEC_BATCH, 2, SSD_STATE, SSD_D_INNER)
            y_lat, _ = _ssd(proj, h0t, consts, seq=DEC_SEQ, nbatch=DEC_BATCH, row_blk0=N_CTX // DEC_SEQ,
                            want_state=False)
            x = _out_proj(o_ctx, o_lat, 0, y_ctx, y_lat, 0, w_out_even[j].astype(BF16), x, mod_l)
            new_na_k.append(proj[:N_CTX, NA_WIDTH:2 * NA_WIDTH].reshape(BATCH, SEQ, NA_HEADS, HEAD_DIM))
            new_na_v.append(proj[:N_CTX, 2 * NA_WIDTH:3 * NA_WIDTH].reshape(BATCH, SEQ, NA_HEADS, HEAD_DIM))
            new_ssd.append(st.reshape(BATCH, 2, SSD_STATE, SSD_HEADS, SSD_HEAD_DIM).transpose(0, 1, 3, 4, 2))
        else:
            proj = _proj_in(x, g_mix, mod_l, w_in_odd[j].astype(BF16))
            o_ctx = _ctx_swa(proj, _sink_rows(swa_sink[j], SEQ))
            ck = cache_swa_k[:, j].reshape(DEC_BATCH, PAST_LEN, SWA_KV)
            cv = cache_swa_v[:, j].reshape(DEC_BATCH, PAST_LEN, SWA_KV)
            o_lat = _lat_swa(proj, ck, cv, cos_t, sin_t, _sink_rows(swa_sink[j], SWA_BLOCK))
            x = _out_proj(o_ctx, o_lat, 0, o_ctx, o_lat, 1, w_out_odd[j].astype(BF16), x, mod_l)
            new_swa_k.append(proj[:N_CTX, SWA_Q:SWA_Q + SWA_KV].reshape(BATCH, SEQ, SWA_KV_HEADS, HEAD_DIM))
            new_swa_v.append(proj[:N_CTX, SWA_Q + SWA_KV:].reshape(BATCH, SEQ, SWA_KV_HEADS, HEAD_DIM))
        x = _moe_layer(l, x, norm_ffn[l].reshape(1, d), mod_l, w_router[l], w_gate, w_up, w_down)
    y_ctx, y_lat = _final_norm(x, final_norm.reshape(1, d))
    return (y_ctx.reshape(BATCH, SEQ, d), y_lat.reshape(DEC_BATCH, DEC_SEQ, d),
            jnp.stack(new_na_k, axis=1), jnp.stack(new_na_v, axis=1), jnp.stack(new_ssd, axis=1),
            jnp.stack(new_swa_k, axis=1), jnp.stack(new_swa_v, axis=1))
```

```python
import functools
import math

import jax
import jax.numpy as jnp
import numpy as np
from jax import lax
from jax.experimental import pallas as pl
from jax.experimental.pallas import tpu as pltpu

F32 = jnp.float32
BF16 = jnp.bfloat16
I32 = jnp.int32
HIGHEST = lax.Precision.HIGHEST

D_MODEL = 1024
BATCH = 16
SEQ = 256
DEPTH = 4
DEC_BATCH = 4
DEC_SEQ = 2048
PAST_LEN = 512
GRID_W = 64
HEAD_DIM = 64
NA_HEADS = 8
NA_KR = 8
NA_KC = 16
SSD_HEADS = 8
SSD_HEAD_DIM = 64
SSD_D_INNER = SSD_HEADS * SSD_HEAD_DIM
SSD_STATE = 128
SSD_GROUPS = 2
SSD_CHUNK = 128
SSD_CONV_W = 5
SWA_HEADS = 16
SWA_KV_HEADS = 4
SWA_WINDOW = 128
SWA_BLOCK = 128
ROPE_BASE = 10000.0
N_EXPERTS = 16
EXPERT_FF = 2048
EC_CAPACITY = 2
NORM_EPS = 1e-6

N_CTX = BATCH * SEQ
N_LAT = DEC_BATCH * DEC_SEQ
N_ALL = N_CTX + N_LAT
NA_WIDTH = NA_HEADS * HEAD_DIM
SSD_BC = SSD_GROUPS * SSD_STATE
SSD_CONV_CH = SSD_D_INNER + 2 * SSD_BC
EVEN_IN = 3 * NA_WIDTH + SSD_D_INNER + SSD_CONV_CH + 2 * SSD_HEADS
EVEN_IN_PAD = 3200
SWA_Q = SWA_HEADS * HEAD_DIM
SWA_KV = SWA_KV_HEADS * HEAD_DIM
ODD_IN = SWA_Q + 2 * SWA_KV
CAP_CTX = EC_CAPACITY * N_CTX // N_EXPERTS
CAP_LAT = EC_CAPACITY * N_LAT // N_EXPERTS
CAP_ALL = CAP_CTX + CAP_LAT
LANES = 128
NEG_BIG = -1e30
ATT_SCALE = HEAD_DIM ** -0.5
VMEM_LIMIT = 56 * 1024 * 1024


def _params(sem):
    return pltpu.CompilerParams(dimension_semantics=sem, vmem_limit_bytes=VMEM_LIMIT)


def _group_of_block(m, tm):
    ctx_blocks = N_CTX // tm
    return lax.select(m < ctx_blocks, jnp.int32(0), 1 + (m - ctx_blocks) // (DEC_SEQ // tm))


def _norm_mod(x, g, shift, scale):
    ms = jnp.mean(x * x, axis=-1, keepdims=True)
    y = x * lax.rsqrt(ms + NORM_EPS) * g
    return y * (1.0 + scale) + shift


def _silu(x):
    return x * jax.nn.sigmoid(x)


def _softplus(x):
    return jnp.maximum(x, 0.0) + jnp.log1p(jnp.exp(-jnp.abs(x)))


def _dot(a, b, **kw):
    return jnp.dot(a, b, preferred_element_type=F32, **kw)


def _dot_nt(a, b, **kw):
    return lax.dot_general(a, b, (((1,), (1,)), ((), ())), preferred_element_type=F32, **kw)


def _split_bf16(x):
    hi = x.astype(BF16)
    return hi, (x - hi.astype(F32)).astype(BF16)


def _value_with_ones(v_tile, upper_half):
    lane = lax.broadcasted_iota(I32, v_tile.shape, 1)
    v = pltpu.roll(v_tile, HEAD_DIM, 1) if upper_half else v_tile
    return jnp.where(lane < HEAD_DIM, v, 1.0).astype(BF16)


def _softmax_pv(q, keys, vals, biases, extra=None):
    logits = []
    for k, b in zip(keys, biases):
        s = _dot_nt(q, k)
        logits.append(s if b is None else s + b)
    mx = extra
    for s in logits:
        for c in range(s.shape[1] // LANES):
            t = s[:, c * LANES:(c + 1) * LANES]
            mx = t if mx is None else jnp.maximum(mx, t)
    m = jnp.max(mx, axis=-1, keepdims=True)
    acc = None
    for s, v in zip(logits, vals):
        pv = _dot(jnp.exp(s - m).astype(BF16), v)
        acc = pv if acc is None else acc + pv
    den = pltpu.roll(acc, HEAD_DIM, 1)
    if extra is not None:
        den = den + jnp.exp(extra - m)
    return (acc / den)[:, 0:HEAD_DIM]


def _adaln_kernel(c_ref, w_ref, b_ref, o_ref):
    s_hi, s_lo = _split_bf16(_silu(c_ref[...]))
    w_hi, w_lo = _split_bf16(w_ref[0])
    o_ref[0] = _dot(s_hi, w_hi) + (_dot(s_lo, w_hi) + _dot(s_hi, w_lo)) + b_ref[0]


def _adaln(cond8, w_mod, b_mod):
    d = D_MODEL
    return pl.pallas_call(
        _adaln_kernel,
        grid=(DEPTH, 6),
        in_specs=[pl.BlockSpec((8, d), lambda l, n: (0, 0)),
                  pl.BlockSpec((1, d, d), lambda l, n: (l, 0, n)),
                  pl.BlockSpec((1, 1, d), lambda l, n: (l, 0, n))],
        out_specs=pl.BlockSpec((1, 8, d), lambda l, n: (l, 0, n)),
        out_shape=jax.ShapeDtypeStruct((DEPTH, 8, 6 * d), F32),
        compiler_params=_params(("arbitrary", "arbitrary")),
        name="adaln",
    )(cond8, w_mod, b_mod.reshape(DEPTH, 1, 6 * d))


def _proj_in_kernel(x_ref, g_ref, mod_ref, w_ref, o_ref, *, tm):
    grp = _group_of_block(pl.program_id(0), tm)
    shift = mod_ref[pl.ds(grp, 1), 0:D_MODEL]
    scale = mod_ref[pl.ds(grp, 1), D_MODEL:2 * D_MODEL]
    h = _norm_mod(x_ref[...], g_ref[...], shift, scale).astype(BF16)
    o_ref[...] = _dot(h, w_ref[...])


def _proj_in(x, g, mod_l, w_bf16, tm=512):
    n = w_bf16.shape[1]
    return pl.pallas_call(
        functools.partial(_proj_in_kernel, tm=tm),
        grid=(N_ALL // tm,),
        in_specs=[pl.BlockSpec((tm, D_MODEL), lambda m: (m, 0)),
                  pl.BlockSpec((1, D_MODEL), lambda m: (0, 0)),
                  pl.BlockSpec((8, 6 * D_MODEL), lambda m: (0, 0)),
                  pl.BlockSpec((D_MODEL, n), lambda m: (0, 0))],
        out_specs=pl.BlockSpec((tm, n), lambda m: (m, 0)),
        out_shape=jax.ShapeDtypeStruct((N_ALL, n), F32),
        compiler_params=_params(("arbitrary",)),
        name="proj_in",
    )(x, g, mod_l, w_bf16)


def _out_proj_kernel(ac_ref, al_ref, bc_ref, bl_ref, w_ref, x_ref, mod_ref, o_ref, *, tm):
    m = pl.program_id(0)
    grp = _group_of_block(m, tm)
    gate = mod_ref[pl.ds(grp, 1), 2 * D_MODEL:3 * D_MODEL]
    half = D_MODEL // 2
    is_ctx = m < N_CTX // tm
    a = jnp.where(is_ctx, ac_ref[...], al_ref[...]).astype(BF16)
    b = jnp.where(is_ctx, bc_ref[...], bl_ref[...]).astype(BF16)
    acc = _dot(a, w_ref[0:half, :]) + _dot(b, w_ref[half:, :])
    o_ref[...] = x_ref[...] + gate * acc


def _out_proj(a_ctx, a_lat, acol, b_ctx, b_lat, bcol, w_bf16, x, mod_l, tm=512):
    half = D_MODEL // 2
    nctx = N_CTX // tm

    def ctx_map(col):
        return lambda m: (jnp.minimum(m, nctx - 1), col)

    def lat_map(col):
        return lambda m: (jnp.maximum(m - nctx, 0), col)

    return pl.pallas_call(
        functools.partial(_out_proj_kernel, tm=tm),
        grid=(N_ALL // tm,),
        in_specs=[pl.BlockSpec((tm, half), ctx_map(acol)),
                  pl.BlockSpec((tm, half), lat_map(acol)),
                  pl.BlockSpec((tm, half), ctx_map(bcol)),
                  pl.BlockSpec((tm, half), lat_map(bcol)),
                  pl.BlockSpec((D_MODEL, D_MODEL), lambda m: (0, 0)),
                  pl.BlockSpec((tm, D_MODEL), lambda m: (m, 0)),
                  pl.BlockSpec((8, 6 * D_MODEL), lambda m: (0, 0))],
        out_specs=pl.BlockSpec((tm, D_MODEL), lambda m: (m, 0)),
        out_shape=jax.ShapeDtypeStruct((N_ALL, D_MODEL), F32),
        compiler_params=_params(("arbitrary",)),
        name="out_proj",
    )(a_ctx, a_lat, b_ctx, b_lat, w_bf16, x, mod_l)


def _ctx_na_kernel(q_ref, k_ref, v_ref, o_ref):
    for h in range(NA_HEADS):
        sl = slice(h * HEAD_DIM, (h + 1) * HEAD_DIM)
        tile = slice((h // 2) * LANES, (h // 2 + 1) * LANES)
        q = (q_ref[:, sl] * ATT_SCALE).astype(BF16)
        v = _value_with_ones(v_ref[:, tile], h % 2 == 1)
        o_ref[:, sl] = _softmax_pv(q, [k_ref[:, sl].astype(BF16)], [v], [None])


def _ctx_na(proj):
    w = NA_WIDTH
    return pl.pallas_call(
        _ctx_na_kernel,
        grid=(BATCH,),
        in_specs=[pl.BlockSpec((SEQ, w), lambda b: (b, 0)),
                  pl.BlockSpec((SEQ, w), lambda b: (b, 1)),
                  pl.BlockSpec((SEQ, w), lambda b: (b, 2))],
        out_specs=pl.BlockSpec((SEQ, w), lambda b: (b, 0)),
        out_shape=jax.ShapeDtypeStruct((N_CTX, w), F32),
        compiler_params=_params(("arbitrary",)),
        name="ctx_na",
    )(proj, proj, proj)


NA_QROWS = 4
NA_KROWS = 12
NA_ROWS = DEC_SEQ // GRID_W
NA_GROUPS = NA_ROWS // NA_QROWS


def _na_key_start(g):
    return jnp.clip(NA_QROWS * g - NA_KR // 2, 0, NA_ROWS - NA_KROWS)


def _na_row_geometry(g):
    start = int(np.clip(NA_QROWS * g - NA_KR // 2, 0, NA_ROWS - NA_KROWS))
    rows = [NA_QROWS * g + qr for qr in range(NA_QROWS)]
    return start, [(r, int(np.clip(r - NA_KR // 2, 0, NA_ROWS - NA_KR))) for r in rows]


def _na_fill_bias(ct_ref, bias_scr, g):
    start, rows = _na_row_geometry(g)
    masked = jnp.full((GRID_W, GRID_W), NEG_BIG, F32)
    for i in range(NA_HEADS):
        for qr, (r, rs) in enumerate(rows):
            for kr in range(NA_KROWS):
                keyrow = start + kr
                inside = rs <= keyrow < rs + NA_KR
                tile = ct_ref[i, keyrow - r + NA_KR - 1] if inside else masked
                bias_scr[i, qr * GRID_W:(qr + 1) * GRID_W, kr * GRID_W:(kr + 1) * GRID_W] = tile


def _lat_na_kernel(q_ref, k_ref, v_ref, ck_ref, cv_ref, ct_ref, o_ref, bias_scr, k_scr, v_scr, ck_scr, cv_scr):
    g = pl.program_id(1)
    for g_build in (0, 1, NA_GROUPS - 1):
        @pl.when(g == g_build)
        def _(g_build=g_build):
            _na_fill_bias(ct_ref, bias_scr, g_build)

    @pl.when(g == 0)
    def _():
        for i in range(NA_HEADS):
            sl = slice(i * HEAD_DIM, (i + 1) * HEAD_DIM)
            tile = slice((i // 2) * LANES, (i // 2 + 1) * LANES)
            k_scr[i] = k_ref[:, sl].astype(BF16)
            v_scr[i] = _value_with_ones(v_ref[:, tile], i % 2 == 1)
            ck_scr[i] = ck_ref[:, sl].astype(BF16)
            cv_scr[i] = _value_with_ones(cv_ref[:, tile], i % 2 == 1)

    start = pl.multiple_of(_na_key_start(g) * GRID_W, GRID_W)
    nk = NA_KROWS * GRID_W
    for i in range(NA_HEADS):
        sl = slice(i * HEAD_DIM, (i + 1) * HEAD_DIM)
        q = (q_ref[:, sl] * ATT_SCALE).astype(BF16)
        o_ref[:, sl] = _softmax_pv(q, [k_scr[i, pl.ds(start, nk), :], ck_scr[i]],
                                   [v_scr[i, pl.ds(start, nk), :], cv_scr[i]], [bias_scr[i], None])


def _na_col_table(rpb):
    w = np.arange(GRID_W)[:, None]
    cc = np.arange(GRID_W)[None, :]
    cs = np.clip(w - NA_KC // 2, 0, GRID_W - NA_KC)
    valid = (cc >= cs) & (cc < cs + NA_KC)
    dc = cc - w + NA_KC - 1
    onehot = ((dc[..., None] == np.arange(2 * NA_KC - 1)) & valid[..., None]).astype(np.float32)
    ct = jnp.einsum('hrd,wcd->hrwc', rpb, jnp.asarray(onehot), precision=HIGHEST)
    return jnp.where(jnp.asarray(valid)[None, None], ct, NEG_BIG)


def _lat_na(proj, ck, cv, col_tab):
    nq = NA_QROWS * GRID_W
    nk = NA_KROWS * GRID_W
    lat_q0 = N_CTX // nq
    lat_b0 = N_CTX // DEC_SEQ
    w = NA_WIDTH
    return pl.pallas_call(
        _lat_na_kernel,
        grid=(DEC_BATCH, NA_GROUPS),
        in_specs=[pl.BlockSpec((nq, w), lambda b, g: (lat_q0 + b * NA_GROUPS + g, 0)),
                  pl.BlockSpec((DEC_SEQ, w), lambda b, g: (lat_b0 + b, 1)),
                  pl.BlockSpec((DEC_SEQ, w), lambda b, g: (lat_b0 + b, 2)),
                  pl.BlockSpec((None, PAST_LEN, w), lambda b, g: (b, 0, 0)),
                  pl.BlockSpec((None, PAST_LEN, w), lambda b, g: (b, 0, 0)),
                  pl.BlockSpec((NA_HEADS, 2 * NA_KR - 1, GRID_W, GRID_W), lambda b, g: (0, 0, 0, 0))],
        out_specs=pl.BlockSpec((nq, w), lambda b, g: (b * NA_GROUPS + g, 0)),
        out_shape=jax.ShapeDtypeStruct((N_LAT, w), F32),
        scratch_shapes=[pltpu.VMEM((NA_HEADS, nq, nk), F32),
                        pltpu.VMEM((NA_HEADS, DEC_SEQ, HEAD_DIM), BF16),
                        pltpu.VMEM((NA_HEADS, DEC_SEQ, LANES), BF16),
                        pltpu.VMEM((NA_HEADS, PAST_LEN, HEAD_DIM), BF16),
                        pltpu.VMEM((NA_HEADS, PAST_LEN, LANES), BF16)],
        compiler_params=_params(("arbitrary", "arbitrary")),
        name="lat_na",
    )(proj, proj, proj, ck, cv, col_tab)


def _ssd_kernel(*refs, seq, has_h0, want_state):
    it = iter(refs)
    z_ref, xbc_ref, dt_ref = next(it), next(it), next(it)
    h0_ref = next(it) if has_h0 else None
    cw_ref, cb_ref, a_ref, dtb_ref, dsk_ref, ng_ref = (next(it) for _ in range(6))
    y_ref = next(it)
    st_ref = next(it) if want_state else None
    xc_scr, y_scr, ht_scr = next(it), next(it), next(it)

    nc = seq // SSD_CHUNK
    ch = SSD_CHUNK
    row = lax.broadcasted_iota(I32, (ch, ch), 0)
    col = lax.broadcasted_iota(I32, (ch, ch), 1)
    erow = lax.broadcasted_iota(I32, (LANES, SSD_D_INNER), 0)
    ecol = lax.broadcasted_iota(I32, (LANES, SSD_D_INNER), 1) // SSD_HEAD_DIM
    cbias = cb_ref[...]

    def conv_chunk(c, carry):
        base = pl.multiple_of(c * ch, ch)
        cur = xbc_ref[pl.ds(base, ch), :]
        pbase = pl.multiple_of(jnp.maximum(base - 8, 0), 8)
        nbase = pl.multiple_of(jnp.minimum(base + ch, seq - 8), 8)
        prev = jnp.where(c > 0, xbc_ref[pl.ds(pbase, 8), :], 0.0)
        nxt = jnp.where(c < nc - 1, xbc_ref[pl.ds(nbase, 8), :], 0.0)
        win = jnp.concatenate([prev, cur, nxt], axis=0)
        acc = jnp.broadcast_to(cbias, (ch, SSD_CONV_CH))
        pad = SSD_CONV_W // 2
        for k in range(SSD_CONV_W):
            off = 8 - pad + k
            acc = acc + win[off:off + ch, :] * cw_ref[k:k + 1, :]
        xc_scr[pl.ds(base, ch), :] = _silu(acc)
        return carry

    lax.fori_loop(0, nc, conv_chunk, 0)

    def run_direction(d):
        lane0 = d * SSD_HEADS
        expand = jnp.where(erow == ecol + lane0, 1.0, 0.0).astype(BF16)
        lmask = (col <= row) if d == 0 else (col >= row)
        tri = jnp.where(lmask, 1.0, 0.0).astype(BF16)

        def dot_split(lhs01, x, lhs_first):
            hi = x.astype(BF16)
            lo = (x - hi.astype(F32)).astype(BF16)
            if lhs_first:
                return _dot(lhs01, hi) + _dot(lhs01, lo)
            return _dot(hi, lhs01) + _dot(lo, lhs01)
        if has_h0:
            ht_scr[...] = h0_ref[d]
        else:
            ht_scr[...] = jnp.zeros((SSD_STATE, SSD_D_INNER), F32)

        def chunk(step, carry):
            c = step if d == 0 else nc - 1 - step
            base = pl.multiple_of(c * ch, ch)
            xs = xc_scr[pl.ds(base, ch), 0:SSD_D_INNER]
            dt = _softplus(dt_ref[pl.ds(base, ch), :] + dtb_ref[...])
            a = dt * a_ref[...]
            cum = dot_split(tri, a, True)
            tot = cum[ch - 1:ch, :] if d == 0 else cum[0:1, :]
            cum_t = cum.T
            dt_x = dot_split(expand, dt, False)
            cum_x = dot_split(expand, cum, False)
            tot_x = dot_split(expand, jnp.broadcast_to(tot, (8, LANES)), False)[0:1, :]
            xt = xs * dt_x
            xd = (xt * jnp.exp(tot_x - cum_x)).astype(BF16)
            xt_b = xt.astype(BF16)
            y_parts = []
            for grp in range(SSD_GROUPS):
                bsl = slice(SSD_D_INNER + grp * SSD_STATE, SSD_D_INNER + (grp + 1) * SSD_STATE)
                csl = slice(SSD_D_INNER + SSD_BC + grp * SSD_STATE, SSD_D_INNER + SSD_BC + (grp + 1) * SSD_STATE)
                b_f = xc_scr[pl.ds(base, ch), bsl]
                b_g = b_f.astype(BF16)
                c_g = xc_scr[pl.ds(base, ch), csl].astype(BF16)
                cb = _dot_nt(c_g, b_g)
                hsl = slice(grp * 4 * SSD_HEAD_DIM, (grp + 1) * 4 * SSD_HEAD_DIM)
                ht_g = ht_scr[:, hsl]
                y_off = _dot(c_g, ht_g.astype(BF16))
                for hh in range(4):
                    head = grp * 4 + hh
                    lane = lane0 + head
                    cum_col = cum[:, lane:lane + 1]
                    cum_row = cum_t[lane:lane + 1, :]
                    ldec = jnp.exp(jnp.where(lmask, cum_col - cum_row, NEG_BIG))
                    psl = slice(head * SSD_HEAD_DIM, (head + 1) * SSD_HEAD_DIM)
                    y_d = _dot((cb * ldec).astype(BF16), xt_b[:, psl])
                    y_parts.append(y_d + y_off[:, hh * SSD_HEAD_DIM:(hh + 1) * SSD_HEAD_DIM] * jnp.exp(cum_col))
                ht_scr[:, hsl] = ht_g * jnp.exp(tot_x[:, hsl]) + _dot(b_f.T.astype(BF16), xd[:, hsl])
            y = jnp.concatenate(y_parts, axis=1)
            if d == 0:
                y_scr[pl.ds(base, ch), :] = y
            else:
                y = y + y_scr[pl.ds(base, ch), :] + dsk_ref[...] * xs
                u = y * _silu(z_ref[pl.ds(base, ch), :])
                ms = jnp.mean(u * u, axis=-1, keepdims=True)
                y_ref[pl.ds(base, ch), :] = u * lax.rsqrt(ms + NORM_EPS) * ng_ref[...]
            return carry

        lax.fori_loop(0, nc, chunk, 0, unroll=2)
        if want_state:
            st_ref[d] = ht_scr[...]

    run_direction(0)
    run_direction(1)


def _ssd(proj, h0t, consts, *, seq, nbatch, row_blk0, want_state):
    has_h0 = h0t is not None
    in_specs = [pl.BlockSpec((seq, SSD_D_INNER), lambda b: (row_blk0 + b, 3)),
                pl.BlockSpec((seq, SSD_CONV_CH), lambda b: (row_blk0 + b, 2)),
                pl.BlockSpec((seq, LANES), lambda b: (row_blk0 + b, 24))]
    args = [proj, proj, proj]
    if has_h0:
        in_specs.append(pl.BlockSpec((None, 2, SSD_STATE, SSD_D_INNER), lambda b: (b, 0, 0, 0)))
        args.append(h0t)
    for cst in consts:
        in_specs.append(pl.BlockSpec(cst.shape, lambda b: (0, 0)))
        args.append(cst)
    out_specs = [pl.BlockSpec((seq, SSD_D_INNER), lambda b: (b, 0))]
    out_shape = [jax.ShapeDtypeStruct((nbatch * seq, SSD_D_INNER), F32)]
    if want_state:
        out_specs.append(pl.BlockSpec((None, 2, SSD_STATE, SSD_D_INNER), lambda b: (b, 0, 0, 0)))
        out_shape.append(jax.ShapeDtypeStruct((nbatch, 2, SSD_STATE, SSD_D_INNER), F32))
    res = pl.pallas_call(
        functools.partial(_ssd_kernel, seq=seq, has_h0=has_h0, want_state=want_state),
        grid=(nbatch,),
        in_specs=in_specs,
        out_specs=out_specs,
        out_shape=out_shape,
        scratch_shapes=[pltpu.VMEM((seq, SSD_CONV_CH), F32),
                        pltpu.VMEM((seq, SSD_D_INNER), F32),
                        pltpu.VMEM((SSD_STATE, SSD_D_INNER), F32)],
        compiler_params=_params(("arbitrary",)),
        name="ssd_seq%d" % seq,
    )(*args)
    return res if want_state else (res[0], None)


def _ctx_swa_kernel(q_ref, k_ref, v_ref, sink_ref, o_ref):
    grp = SWA_HEADS // SWA_KV_HEADS
    for kv in range(SWA_KV_HEADS):
        ksl = slice(kv * HEAD_DIM, (kv + 1) * HEAD_DIM)
        tile = slice((kv // 2) * LANES, (kv // 2 + 1) * LANES)
        k = k_ref[:, ksl].astype(BF16)
        v = _value_with_ones(v_ref[:, tile], kv % 2 == 1)
        q4 = jnp.concatenate(
            [q_ref[:, (kv * grp + j) * HEAD_DIM:(kv * grp + j + 1) * HEAD_DIM] for j in range(grp)],
            axis=0)
        o = _softmax_pv((q4 * ATT_SCALE).astype(BF16), [k], [v], [None], extra=sink_ref[kv])
        for j in range(grp):
            o_ref[:, (kv * grp + j) * HEAD_DIM:(kv * grp + j + 1) * HEAD_DIM] = o[j * SEQ:(j + 1) * SEQ]


def _ctx_swa(proj, sink_col):
    return pl.pallas_call(
        _ctx_swa_kernel,
        grid=(BATCH,),
        in_specs=[pl.BlockSpec((SEQ, SWA_Q), lambda b: (b, 0)),
                  pl.BlockSpec((SEQ, SWA_KV), lambda b: (b, 4)),
                  pl.BlockSpec((SEQ, SWA_KV), lambda b: (b, 5)),
                  pl.BlockSpec(sink_col.shape, lambda b: (0, 0, 0))],
        out_specs=pl.BlockSpec((SEQ, SWA_Q), lambda b: (b, 0)),
        out_shape=jax.ShapeDtypeStruct((N_CTX, SWA_Q), F32),
        compiler_params=_params(("arbitrary",)),
        name="ctx_swa",
    )(proj, proj, proj, sink_col)


SWA_NLOC = 3 * SWA_BLOCK


def _rope(x, cos, sin_signed, first):
    n = x.shape[-1]
    partner = jnp.where(first, pltpu.roll(x, n - 16, 1), pltpu.roll(x, 16, 1))
    return x * cos + partner * sin_signed


def _lat_swa_kernel(q_ref, k_ref, v_ref, ck_ref, cv_ref, cosk_ref, sink_k_ref, cosq_ref, sinq_ref, sink_ref,
                    o_ref, kr_scr, v_scr, ck_scr, cv_scr):
    i = pl.program_id(1)
    grp = SWA_HEADS // SWA_KV_HEADS
    lane = lax.broadcasted_iota(I32, (1, SWA_KV), 1)
    first = (lane % 32) < 16

    @pl.when(i == 0)
    def _():
        kr = _rope(k_ref[...], cosk_ref[...], sink_k_ref[...], first)
        for kv in range(SWA_KV_HEADS):
            ksl = slice(kv * HEAD_DIM, (kv + 1) * HEAD_DIM)
            tile = slice((kv // 2) * LANES, (kv // 2 + 1) * LANES)
            kr_scr[kv] = kr[:, ksl].astype(BF16)
            v_scr[kv] = _value_with_ones(v_ref[:, tile], kv % 2 == 1)
            ck_scr[kv] = ck_ref[:, ksl].astype(BF16)
            cv_scr[kv] = _value_with_ones(cv_ref[:, tile], kv % 2 == 1)

    kstart = pl.multiple_of(jnp.clip((i - 1) * SWA_BLOCK, 0, DEC_SEQ - SWA_NLOC), SWA_BLOCK)
    qpos = i * SWA_BLOCK + lax.broadcasted_iota(I32, (SWA_BLOCK, SWA_NLOC), 0)
    kpos = kstart + lax.broadcasted_iota(I32, (SWA_BLOCK, SWA_NLOC), 1)
    mask1 = jnp.where(jnp.abs(qpos - kpos) <= SWA_WINDOW, 0.0, NEG_BIG)
    mask = jnp.concatenate([mask1] * grp, axis=0)
    cosq = cosq_ref[...]
    sinq = sinq_ref[...]
    for kv in range(SWA_KV_HEADS):
        qr = _rope(q_ref[:, kv * SWA_KV:(kv + 1) * SWA_KV], cosq, sinq, first) * ATT_SCALE
        q4 = jnp.concatenate([qr[:, j * HEAD_DIM:(j + 1) * HEAD_DIM] for j in range(grp)], axis=0).astype(BF16)
        o = _softmax_pv(q4, [kr_scr[kv, pl.ds(kstart, SWA_NLOC), :], ck_scr[kv]],
                        [v_scr[kv, pl.ds(kstart, SWA_NLOC), :], cv_scr[kv]], [mask, None], extra=sink_ref[kv])
        for j in range(grp):
            o_ref[:, (kv * grp + j) * HEAD_DIM:(kv * grp + j + 1) * HEAD_DIM] = o[j * SWA_BLOCK:(j + 1) * SWA_BLOCK]


def _rope_tables():
    half = HEAD_DIM // 2
    quarter = half // 2
    pos = jnp.arange(DEC_SEQ)
    inv_freq = 1.0 / (ROPE_BASE ** (jnp.arange(quarter, dtype=F32) * 2.0 / half))
    d = np.arange(HEAD_DIM)
    use_col = jnp.asarray(d >= half)
    p = jnp.where(use_col[None, :], (pos % GRID_W)[:, None], (pos // GRID_W)[:, None]).astype(F32)
    ang = p * inv_freq[d % quarter][None, :]
    sign = jnp.asarray(np.where((d % half) < quarter, -1.0, 1.0), F32)
    cos = jnp.cos(ang)
    sin_signed = jnp.sin(ang) * sign[None, :]
    reps = SWA_KV // HEAD_DIM
    return jnp.tile(cos, (1, reps)), jnp.tile(sin_signed, (1, reps))


def _lat_swa(proj, ck, cv, cos_t, sin_t, sink_col):
    nb = DEC_SEQ // SWA_BLOCK
    q0 = N_CTX // SWA_BLOCK
    b0 = N_CTX // DEC_SEQ
    return pl.pallas_call(
        _lat_swa_kernel,
        grid=(DEC_BATCH, nb),
        in_specs=[pl.BlockSpec((SWA_BLOCK, SWA_Q), lambda b, i: (q0 + b * nb + i, 0)),
                  pl.BlockSpec((DEC_SEQ, SWA_KV), lambda b, i: (b0 + b, 4)),
                  pl.BlockSpec((DEC_SEQ, SWA_KV), lambda b, i: (b0 + b, 5)),
                  pl.BlockSpec((None, PAST_LEN, SWA_KV), lambda b, i: (b, 0, 0)),
                  pl.BlockSpec((None, PAST_LEN, SWA_KV), lambda b, i: (b, 0, 0)),
                  pl.BlockSpec((DEC_SEQ, SWA_KV), lambda b, i: (0, 0)),
                  pl.BlockSpec((DEC_SEQ, SWA_KV), lambda b, i: (0, 0)),
                  pl.BlockSpec((SWA_BLOCK, SWA_KV), lambda b, i: (i, 0)),
                  pl.BlockSpec((SWA_BLOCK, SWA_KV), lambda b, i: (i, 0)),
                  pl.BlockSpec(sink_col.shape, lambda b, i: (0, 0, 0))],
        out_specs=pl.BlockSpec((SWA_BLOCK, SWA_Q), lambda b, i: (b * nb + i, 0)),
        out_shape=jax.ShapeDtypeStruct((N_LAT, SWA_Q), F32),
        scratch_shapes=[pltpu.VMEM((SWA_KV_HEADS, DEC_SEQ, HEAD_DIM), BF16),
                        pltpu.VMEM((SWA_KV_HEADS, DEC_SEQ, LANES), BF16),
                        pltpu.VMEM((SWA_KV_HEADS, PAST_LEN, HEAD_DIM), BF16),
                        pltpu.VMEM((SWA_KV_HEADS, PAST_LEN, LANES), BF16)],
        compiler_params=_params(("arbitrary", "arbitrary")),
        name="lat_swa",
    )(proj, proj, proj, ck, cv, cos_t, sin_t, cos_t, sin_t, sink_col)


def _ffn_prep_kernel(x_ref, g_ref, mod_ref, wr_ref, h_ref, aff_ref, *, tm):
    grp = _group_of_block(pl.program_id(0), tm)
    shift = mod_ref[pl.ds(grp, 1), 3 * D_MODEL:4 * D_MODEL]
    scale = mod_ref[pl.ds(grp, 1), 4 * D_MODEL:5 * D_MODEL]
    h = _norm_mod(x_ref[...], g_ref[...], shift, scale)
    h_ref[...] = h
    h_hi, h_lo = _split_bf16(h)
    w_hi, w_lo = _split_bf16(wr_ref[...])
    logits = _dot_nt(w_hi, h_hi) + (_dot_nt(w_hi, h_lo) + _dot_nt(w_lo, h_hi))
    e = jnp.exp(logits - jnp.max(logits, axis=0, keepdims=True))
    aff_ref[...] = e / jnp.sum(e, axis=0, keepdims=True)


def _ffn_prep(x, g, mod_l, w_router_t, tm=512):
    return pl.pallas_call(
        functools.partial(_ffn_prep_kernel, tm=tm),
        grid=(N_ALL // tm,),
        in_specs=[pl.BlockSpec((tm, D_MODEL), lambda m: (m, 0)),
                  pl.BlockSpec((1, D_MODEL), lambda m: (0, 0)),
                  pl.BlockSpec((8, 6 * D_MODEL), lambda m: (0, 0)),
                  pl.BlockSpec((N_EXPERTS, D_MODEL), lambda m: (0, 0))],
        out_specs=[pl.BlockSpec((tm, D_MODEL), lambda m: (m, 0)),
                   pl.BlockSpec((N_EXPERTS, tm), lambda m: (0, m))],
        out_shape=[jax.ShapeDtypeStruct((N_ALL, D_MODEL), F32),
                   jax.ShapeDtypeStruct((N_EXPERTS, N_ALL), F32)],
        compiler_params=_params(("arbitrary",)),
        name="ffn_prep",
    )(x, g, mod_l, w_router_t)


def _route_kernel(aff_ref, idx_ref, gate_ref, cend_ref, pos_ref, t_scr, *, nb, cap):
    ne = N_EXPERTS
    capf = float(cap)

    def count_ge(e, cand):
        hit = jnp.where(aff_ref[e] >= cand, 1.0, 0.0)
        return jnp.sum(jnp.sum(hit, axis=1, keepdims=True), axis=0, keepdims=True)

    def bit_step(i, ts):
        bit = jnp.left_shift(jnp.int32(1), 30 - i)
        out = []
        for e in range(ne):
            cand = ts[e] | bit
            keep = count_ge(e, lax.bitcast_convert_type(cand, F32)) >= capf
            out.append(jnp.where(keep, cand, ts[e]))
        return tuple(out)

    ts = lax.fori_loop(0, 31, bit_step, tuple(jnp.zeros((1, 1), I32) for _ in range(ne)))
    for e in range(ne):
        t_scr[e] = jnp.broadcast_to(lax.bitcast_convert_type(ts[e], F32), (8, LANES))

    r = lax.broadcasted_iota(I32, (LANES, LANES), 0)
    c = lax.broadcasted_iota(I32, (LANES, LANES), 1)
    upper = jnp.where(r <= c, 1.0, 0.0).astype(BF16)
    lower_incl = jnp.where(c <= r, 1.0, 0.0).astype(BF16)
    lower_strict = jnp.where(c < r, 1.0, 0.0).astype(BF16)
    npad = LANES - nb
    slot = lax.broadcasted_iota(I32, (LANES, cap), 1).astype(F32)
    sub = lax.broadcasted_iota(I32, (LANES, cap), 0).astype(F32)

    def cumsum_tokens(x):
        rowc = _dot(x.astype(BF16), upper)
        tot = jnp.broadcast_to(rowc[:, LANES - 1:LANES], (LANES, LANES))
        return rowc + _dot(lower_strict, tot.astype(BF16)), tot

    def per_expert(e, carry):
        a = aff_ref[e]
        if npad:
            a = jnp.concatenate([a, jnp.full((npad, LANES), -1.0, F32)], axis=0)
        thr = t_scr[e][0:1, 0:1]
        gt = a > thr
        eq = a == thr
        gtf = jnp.where(gt, 1.0, 0.0)
        eqf = jnp.where(eq, 1.0, 0.0)
        need = capf - jnp.sum(jnp.sum(gtf, axis=1, keepdims=True), axis=0, keepdims=True)
        eq_incl, _ = cumsum_tokens(eqf)
        sel = gt | (eq & (eq_incl - eqf < need))
        self = jnp.where(sel, 1.0, 0.0)
        incl, tot = cumsum_tokens(self)
        cend = _dot(lower_incl, tot.astype(BF16))
        blk = jnp.sum(jnp.where(cend[:, 0:1] <= slot, 1.0, 0.0), axis=0, keepdims=True)
        onehot = jnp.where(sub == blk, 1.0, 0.0)
        hi = jnp.floor(incl * (1.0 / LANES))
        lo = incl - hi * LANES
        a_pos = jnp.maximum(a, 0.0)
        a1 = a_pos.astype(BF16)
        r1 = a_pos - a1.astype(F32)
        a2 = r1.astype(BF16)
        a3 = (r1 - a2.astype(F32)).astype(BF16)
        stacked = jnp.concatenate([hi.T.astype(BF16), lo.T.astype(BF16), a1.T, a2.T, a3.T], axis=0)
        picked = _dot(stacked, onehot.astype(BF16))
        inc_s = picked[0:LANES] * LANES + picked[LANES:2 * LANES]
        within = jnp.sum(jnp.where(inc_s <= slot, 1.0, 0.0), axis=0, keepdims=True)
        aff_s = picked[2 * LANES:3 * LANES] + picked[3 * LANES:4 * LANES] + picked[4 * LANES:5 * LANES]
        gate = jnp.sum(jnp.where(sub == within, aff_s, 0.0), axis=0, keepdims=True)
        idx_ref[e] = (blk * LANES + within).astype(I32)
        gate_ref[e] = gate
        cend_ref[e] = cend.T[0:8, :].astype(I32)
        pos_ref[e] = jnp.where(sel, incl - 1.0, -1.0)[0:nb, :]
        return carry

    lax.fori_loop(0, ne, per_expert, 0)


def _route(aff3, cap):
    ne, nb, _ = aff3.shape
    return pl.pallas_call(
        functools.partial(_route_kernel, nb=nb, cap=cap),
        grid=(1,),
        in_specs=[pl.BlockSpec((ne, nb, LANES), lambda i: (0, 0, 0))],
        out_specs=[pl.BlockSpec((ne, 1, cap), lambda i: (0, 0, 0)),
                   pl.BlockSpec((ne, 1, cap), lambda i: (0, 0, 0)),
                   pl.BlockSpec((ne, 8, LANES), lambda i: (0, 0, 0)),
                   pl.BlockSpec((ne, nb, LANES), lambda i: (0, 0, 0))],
        out_shape=[jax.ShapeDtypeStruct((ne, 1, cap), I32),
                   jax.ShapeDtypeStruct((ne, 1, cap), F32),
                   jax.ShapeDtypeStruct((ne, 8, LANES), I32),
                   jax.ShapeDtypeStruct((ne, nb, LANES), F32)],
        scratch_shapes=[pltpu.VMEM((ne, 8, LANES), F32)],
        compiler_params=_params(("arbitrary",)),
        name="route_cap%d" % cap,
    )(aff3)


FF_TILE = 512
COMB_PIECE = 32
CAP_PAD = CAP_ALL + COMB_PIECE


def _moe_ffn_kernel(idx_ref, h_hbm, gate_ref, wg_ref, wu_ref, wd_ref, o_ref, land, xe, acc, sem):
    e = pl.program_id(0)
    f = pl.program_id(1)
    nf = EXPERT_FF // FF_TILE
    rps = CAP_ALL // nf

    def row_copy(expert, part, s):
        tok = idx_ref[expert * CAP_ALL + part * rps + s]
        return pltpu.make_async_copy(h_hbm.at[pl.ds(tok, 1), :], land.at[part, pl.ds(s, 1), :], sem)

    def wait_rows():
        for part in range(nf):
            pltpu.make_async_copy(h_hbm.at[pl.ds(0, rps), :], land.at[part], sem).wait()

    @pl.when(f == 0)
    def _():
        @pl.when(e == 0)
        def _():
            for part in range(nf):
                def issue(s, carry, part=part):
                    row_copy(0, part, s).start()
                    return carry

                lax.fori_loop(0, rps, issue, 0)

        wait_rows()
        for part in range(nf):
            xe[part * rps:(part + 1) * rps, :] = land[part].astype(BF16)
        acc[...] = jnp.zeros_like(acc)

    nxt = (e + 1) % N_EXPERTS
    for s in range(rps):
        row_copy(nxt, f, s).start()

    x = xe[...]
    g = _dot(x, wg_ref[...].astype(BF16))
    u = _dot(x, wu_ref[...].astype(BF16))
    hid = (_silu(g) * u).astype(BF16)
    acc[...] += _dot(hid, wd_ref[...].astype(BF16))

    @pl.when(f == nf - 1)
    def _():
        y = acc[...] * gate_ref[...]
        hi = y.astype(BF16)
        o_ref[0:CAP_ALL, 0:D_MODEL] = hi
        o_ref[0:CAP_ALL, D_MODEL:2 * D_MODEL] = (y - hi.astype(F32)).astype(BF16)
        o_ref[CAP_ALL:CAP_PAD, :] = jnp.zeros((CAP_PAD - CAP_ALL, 2 * D_MODEL), BF16)

        @pl.when(e == N_EXPERTS - 1)
        def _():
            wait_rows()


def _moe_ffn(layer, h, idx_flat, gate_col, w_gate, w_up, w_down):
    nf = EXPERT_FF // FF_TILE
    grid_spec = pltpu.PrefetchScalarGridSpec(
        num_scalar_prefetch=1,
        grid=(N_EXPERTS, nf),
        in_specs=[pl.BlockSpec(memory_space=pl.ANY),
                  pl.BlockSpec((None, CAP_ALL, 1), lambda e, f, idx: (e, 0, 0)),
                  pl.BlockSpec((None, None, D_MODEL, FF_TILE), lambda e, f, idx: (layer, e, 0, f)),
                  pl.BlockSpec((None, None, D_MODEL, FF_TILE), lambda e, f, idx: (layer, e, 0, f)),
                  pl.BlockSpec((None, None, FF_TILE, D_MODEL), lambda e, f, idx: (layer, e, f, 0))],
        out_specs=pl.BlockSpec((None, CAP_PAD, 2 * D_MODEL), lambda e, f, idx: (e, 0, 0)),
        scratch_shapes=[pltpu.VMEM((nf, CAP_ALL // nf, D_MODEL), F32),
                        pltpu.VMEM((CAP_ALL, D_MODEL), BF16),
                        pltpu.VMEM((CAP_ALL, D_MODEL), F32),
                        pltpu.SemaphoreType.DMA(())],
    )
    return pl.pallas_call(
        _moe_ffn_kernel,
        grid_spec=grid_spec,
        out_shape=jax.ShapeDtypeStruct((N_EXPERTS, CAP_PAD, 2 * D_MODEL), BF16),
        compiler_params=_params(("arbitrary", "arbitrary")),
        name="moe_ffn",
    )(idx_flat, h, gate_col, w_gate, w_up, w_down)


COMB_TM = 256
COMB_NB = N_ALL // COMB_TM
COMB_HALF = N_EXPERTS // 2
COMB_CHUNK = 256
COMB_ALIGN = 16
COMB_WAIT_GROUP = 32
COMB_STAGE = -(-(COMB_HALF * (COMB_TM + COMB_PIECE + COMB_ALIGN)) // COMB_CHUNK) * COMB_CHUNK
ROUTE_NB = N_ALL // LANES


def _combine_kernel(st_ref, ye_hbm, pos_ref, x_ref, mod_ref, o_ref, stage, acc, sem):
    b = pl.program_id(0)
    nst = ROUTE_NB + 1
    per = COMB_TM // LANES

    def layout(blk, half):
        out = []
        off = jnp.int32(0)
        for j in range(COMB_HALF):
            e = half * COMB_HALF + j
            s0 = st_ref[e * nst + per * blk]
            s1 = st_ref[e * nst + per * blk + per]
            a0 = (s0 // COMB_ALIGN) * COMB_ALIGN
            npc = jnp.where(s1 > s0, (s1 - a0 + COMB_PIECE - 1) // COMB_PIECE, 0)
            out.append((e, a0, off, npc))
            off = off + npc * COMB_PIECE
        return out, off // COMB_PIECE

    def piece_copy(e, src_row, buf, dst_row, pieces=1):
        rows = pieces * COMB_PIECE
        return pltpu.make_async_copy(ye_hbm.at[e, pl.ds(src_row, rows), :],
                                     stage.at[buf, pl.ds(dst_row, rows), :], sem.at[buf])

    def issue_unit(blk, half):
        lay, _ = layout(blk, half)
        for e, a0, off, npc in lay:
            def issue(k, carry, e=e, a0=a0, off=off):
                piece_copy(e, pl.multiple_of(a0 + k * COMB_PIECE, COMB_ALIGN), half,
                           pl.multiple_of(off + k * COMB_PIECE, COMB_ALIGN)).start()
                return carry

            lax.fori_loop(0, npc, issue, 0)

    @pl.when(b == 0)
    def _():
        stage[...] = jnp.zeros_like(stage)
        issue_unit(0, 0)

    acc[...] = jnp.zeros_like(acc)
    pos = pos_ref[...]
    row_t = lax.broadcasted_iota(I32, (COMB_CHUNK, COMB_TM), 0).astype(F32)
    row_l = lax.broadcasted_iota(I32, (COMB_CHUNK, LANES), 0).astype(F32)
    lane = lax.broadcasted_iota(I32, (1, LANES), 1)
    for half in range(2):
        lay, npieces = layout(b, half)

        def wait_big(k, carry, half=half):
            piece_copy(0, 0, half, 0, COMB_WAIT_GROUP).wait()
            return carry

        lax.fori_loop(0, npieces // COMB_WAIT_GROUP, wait_big, 0)
        small = COMB_WAIT_GROUP // 2
        while small >= 1:
            @pl.when((npieces & small) != 0)
            def _(small=small, half=half):
                piece_copy(0, 0, half, 0, small).wait()

            small //= 2
        if half == 0:
            issue_unit(b, 1)
        else:
            @pl.when(b + 1 < COMB_NB)
            def _():
                issue_unit(b + 1, 0)

        srows = []
        first = jnp.full((1, LANES), 1e9, F32)
        last = jnp.zeros((1, LANES), F32)
        for j, (e, a0, off, npc) in enumerate(lay):
            p = pos[e:e + 1, :]
            srows.append(jnp.where(p >= 0.0, p + (off - a0).astype(F32), -1.0))
            first = jnp.where(lane == j, off.astype(F32), first)
            last = jnp.where(lane == j, (off + npc * COMB_PIECE).astype(F32), last)
        srow = jnp.concatenate(srows + [jnp.zeros((LANES - COMB_HALF, COMB_TM), F32)], axis=0)
        srow_hi = jnp.floor(srow * (1.0 / 64.0))
        srow_lo = (srow - 64.0 * srow_hi).astype(BF16)
        srow_hi = srow_hi.astype(BF16)

        def select_rows(ci, srow_hi=srow_hi, srow_lo=srow_lo, first=first, last=last):
            r0f = (ci * COMB_CHUNK).astype(F32)
            rid = row_l + r0f
            owner = jnp.where((rid >= first) & (rid < last), 1.0, 0.0).astype(BF16)
            want = 64.0 * _dot(owner, srow_hi) + _dot(owner, srow_lo)
            return jnp.where(want == row_t + r0f, 1.0, 0.0).T.astype(BF16)

        def chunk(ci, sel_t, half=half, select_rows=select_rows):
            sel_next = select_rows(ci + 1)
            r0 = pl.multiple_of(ci * COMB_CHUNK, COMB_CHUNK)
            acc[...] += (_dot(sel_t, stage[half, pl.ds(r0, COMB_CHUNK), 0:D_MODEL])
                         + _dot(sel_t, stage[half, pl.ds(r0, COMB_CHUNK), D_MODEL:2 * D_MODEL]))
            return sel_next

        nchunks = (npieces * COMB_PIECE + COMB_CHUNK - 1) // COMB_CHUNK
        lax.fori_loop(0, nchunks, chunk, select_rows(jnp.int32(0)))

    grp = _group_of_block(b, COMB_TM)
    gate = mod_ref[pl.ds(grp, 1), 5 * D_MODEL:6 * D_MODEL]
    o_ref[...] = x_ref[...] + gate * acc[...]


def _combine(ye, starts_flat, pos, x, mod_l):
    grid_spec = pltpu.PrefetchScalarGridSpec(
        num_scalar_prefetch=1,
        grid=(COMB_NB,),
        in_specs=[pl.BlockSpec(memory_space=pl.ANY),
                  pl.BlockSpec((N_EXPERTS, COMB_TM), lambda b, st: (0, b)),
                  pl.BlockSpec((COMB_TM, D_MODEL), lambda b, st: (b, 0)),
                  pl.BlockSpec((8, 6 * D_MODEL), lambda b, st: (0, 0))],
        out_specs=pl.BlockSpec((COMB_TM, D_MODEL), lambda b, st: (b, 0)),
        scratch_shapes=[pltpu.VMEM((2, COMB_STAGE, 2 * D_MODEL), BF16),
                        pltpu.VMEM((COMB_TM, D_MODEL), F32),
                        pltpu.SemaphoreType.DMA((2,))],
    )
    return pl.pallas_call(
        _combine_kernel,
        grid_spec=grid_spec,
        out_shape=jax.ShapeDtypeStruct((N_ALL, D_MODEL), F32),
        compiler_params=_params(("arbitrary",)),
        name="moe_combine",
    )(starts_flat, ye, pos, x, mod_l)


def _moe_layer(layer, x, g, mod_l, w_router, w_gate, w_up, w_down):
    h, aff = _ffn_prep(x, g, mod_l, w_router.T)
    nb_ctx = N_CTX // LANES
    nb_lat = N_LAT // LANES
    idx_c, gate_c, cend_c, pos_c = _route(aff[:, :N_CTX].reshape(N_EXPERTS, nb_ctx, LANES), CAP_CTX)
    idx_l, gate_l, cend_l, pos_l = _route(aff[:, N_CTX:].reshape(N_EXPERTS, nb_lat, LANES), CAP_LAT)
    idx = jnp.concatenate([idx_c[:, 0, :], idx_l[:, 0, :] + N_CTX], axis=1)
    gate = jnp.concatenate([gate_c[:, 0, :], gate_l[:, 0, :]], axis=1)
    starts = jnp.concatenate([jnp.zeros((N_EXPERTS, 1), I32), cend_c[:, 0, :nb_ctx],
                              cend_l[:, 0, :nb_lat] + CAP_CTX], axis=1)
    pos_l = pos_l.reshape(N_EXPERTS, N_LAT)
    pos = jnp.concatenate([pos_c.reshape(N_EXPERTS, N_CTX),
                           jnp.where(pos_l >= 0.0, pos_l + CAP_CTX, -1.0)], axis=1)
    ye = _moe_ffn(layer, h, idx.reshape(-1), gate.reshape(N_EXPERTS, CAP_ALL, 1), w_gate, w_up, w_down)
    return _combine(ye, starts.reshape(-1), pos, x, mod_l)


def _final_norm_kernel(x_ref, g_ref, oc_ref, ol_ref, *, tm):
    x = x_ref[...]
    ms = jnp.mean(x * x, axis=-1, keepdims=True)
    y = x * lax.rsqrt(ms + NORM_EPS) * g_ref[...]
    is_ctx = pl.program_id(0) < N_CTX // tm

    @pl.when(is_ctx)
    def _():
        oc_ref[...] = y

    @pl.when(jnp.logical_not(is_ctx))
    def _():
        ol_ref[...] = y


def _final_norm(x, g, tm=1024):
    nctx = N_CTX // tm
    return pl.pallas_call(
        functools.partial(_final_norm_kernel, tm=tm),
        grid=(N_ALL // tm,),
        in_specs=[pl.BlockSpec((tm, D_MODEL), lambda m: (m, 0)),
                  pl.BlockSpec((1, D_MODEL), lambda m: (0, 0))],
        out_specs=[pl.BlockSpec((tm, D_MODEL), lambda m: (jnp.minimum(m, nctx - 1), 0)),
                   pl.BlockSpec((tm, D_MODEL), lambda m: (jnp.maximum(m - nctx, 0), 0))],
        out_shape=[jax.ShapeDtypeStruct((N_CTX, D_MODEL), F32),
                   jax.ShapeDtypeStruct((N_LAT, D_MODEL), F32)],
        compiler_params=_params(("arbitrary",)),
        name="final_norm",
    )(x, g)


def _sink_rows(sink, rows):
    grp = SWA_HEADS // SWA_KV_HEADS
    col = jnp.repeat(sink.reshape(SWA_KV_HEADS, grp), rows, axis=1).reshape(SWA_KV_HEADS, grp * rows, 1)
    return jnp.broadcast_to(col, (SWA_KV_HEADS, grp * rows, LANES))


def _lane_row(v, width=LANES):
    return jnp.zeros((1, width), F32).at[0, :v.shape[0]].set(v)


def kernel(x_prompt, x_sample, cache_na_k, cache_na_v, state_ssd, cache_swa_k, cache_swa_v, c, c_ctx, norm_mix, norm_ffn, w_mod, b_mod, w_in_even, na_rpb, ssd_conv_w, ssd_conv_b, ssd_a_log, ssd_dt_bias, ssd_d, ssd_norm, w_out_even, w_in_odd, swa_sink, w_out_odd, w_router, w_gate, w_up, w_down, final_norm):
    d = D_MODEL
    x = jnp.concatenate([x_prompt.reshape(N_CTX, d), x_sample.reshape(N_LAT, d)], axis=0)
    cond8 = jnp.zeros((8, d), F32).at[0].set(c_ctx).at[1:1 + DEC_BATCH].set(c)
    mod = _adaln(cond8, w_mod, b_mod)
    cos_t, sin_t = _rope_tables()
    new_na_k, new_na_v, new_ssd, new_swa_k, new_swa_v = [], [], [], [], []
    for l in range(DEPTH):
        j = l // 2
        mod_l = mod[l]
        g_mix = norm_mix[l].reshape(1, d)
        if l % 2 == 0:
            w_in = jnp.pad(w_in_even[j], ((0, 0), (0, EVEN_IN_PAD - EVEN_IN))).astype(BF16)
            proj = _proj_in(x, g_mix, mod_l, w_in)
            o_ctx = _ctx_na(proj)
            ck = cache_na_k[:, j].reshape(DEC_BATCH, PAST_LEN, NA_WIDTH)
            cv = cache_na_v[:, j].reshape(DEC_BATCH, PAST_LEN, NA_WIDTH)
            o_lat = _lat_na(proj, ck, cv, _na_col_table(na_rpb[j]))
            consts = [jnp.pad(ssd_conv_w[j], ((0, 8 - SSD_CONV_W), (0, 0))),
                      ssd_conv_b[j].reshape(1, SSD_CONV_CH),
                      _lane_row(-jnp.exp(ssd_a_log[j].reshape(-1))),
                      _lane_row(ssd_dt_bias[j].reshape(-1)),
                      jnp.repeat(ssd_d[j], SSD_HEAD_DIM).reshape(1, SSD_D_INNER),
                      ssd_norm[j].reshape(1, SSD_D_INNER)]
            y_ctx, st = _ssd(proj, None, consts, seq=SEQ, nbatch=BATCH, row_blk0=0, want_state=True)
            h0t = state_ssd[:, j].transpose(0, 1, 4, 2, 3).reshape(DEC_BATCH, 2, SSD_STATE, SSD_D_INNER)
            y_lat, _ = _ssd(proj, h0t, consts, seq=DEC_SEQ, nbatch=DEC_BATCH, row_blk0=N_CTX // DEC_SEQ,
                            want_state=False)
            x = _out_proj(o_ctx, o_lat, 0, y_ctx, y_lat, 0, w_out_even[j].astype(BF16), x, mod_l)
            new_na_k.append(proj[:N_CTX, NA_WIDTH:2 * NA_WIDTH].reshape(BATCH, SEQ, NA_WIDTH))
            new_na_v.append(proj[:N_CTX, 2 * NA_WIDTH:3 * NA_WIDTH].reshape(BATCH, SEQ, NA_WIDTH))
            new_ssd.append(st)
        else:
            proj = _proj_in(x, g_mix, mod_l, w_in_odd[j].astype(BF16))
            o_ctx = _ctx_swa(proj, _sink_rows(swa_sink[j], SEQ))
            ck = cache_swa_k[:, j].reshape(DEC_BATCH, PAST_LEN, SWA_KV)
            cv = cache_swa_v[:, j].reshape(DEC_BATCH, PAST_LEN, SWA_KV)
            o_lat = _lat_swa(proj, ck, cv, cos_t, sin_t, _sink_rows(swa_sink[j], SWA_BLOCK))
            x = _out_proj(o_ctx, o_lat, 0, o_ctx, o_lat, 1, w_out_odd[j].astype(BF16), x, mod_l)
            new_swa_k.append(proj[:N_CTX, SWA_Q:SWA_Q + SWA_KV].reshape(BATCH, SEQ, SWA_KV))
            new_swa_v.append(proj[:N_CTX, SWA_Q + SWA_KV:].reshape(BATCH, SEQ, SWA_KV))
        x = _moe_layer(l, x, norm_ffn[l].reshape(1, d), mod_l, w_router[l], w_gate, w_up, w_down)
    y_ctx, y_lat = _final_norm(x, final_norm.reshape(1, d))
    n_even, n_odd = len(new_na_k), len(new_swa_k)
    na_shape = (BATCH, n_even, SEQ, NA_HEADS, HEAD_DIM)
    swa_shape = (BATCH, n_odd, SEQ, SWA_KV_HEADS, HEAD_DIM)
    ssd_t = jnp.stack(new_ssd, axis=1).reshape(BATCH, n_even, 2, SSD_STATE, SSD_HEADS, SSD_HEAD_DIM)
    return (y_ctx.reshape(BATCH, SEQ, d), y_lat.reshape(DEC_BATCH, DEC_SEQ, d),
            jnp.stack(new_na_k, axis=1).reshape(na_shape), jnp.stack(new_na_v, axis=1).reshape(na_shape),
            ssd_t.transpose(0, 1, 2, 4, 5, 3),
            jnp.stack(new_swa_k, axis=1).reshape(swa_shape), jnp.stack(new_swa_v, axis=1).reshape(swa_shape))
```

```python
import functools
import math

import jax
import jax.numpy as jnp
import numpy as np
from jax import lax
from jax.experimental import pallas as pl
from jax.experimental.pallas import tpu as pltpu

F32 = jnp.float32
BF16 = jnp.bfloat16
I32 = jnp.int32
HIGHEST = lax.Precision.HIGHEST

D_MODEL = 1024
BATCH = 16
SEQ = 256
DEPTH = 4
DEC_BATCH = 4
DEC_SEQ = 2048
PAST_LEN = 512
GRID_W = 64
HEAD_DIM = 64
NA_HEADS = 8
NA_KR = 8
NA_KC = 16
SSD_HEADS = 8
SSD_HEAD_DIM = 64
SSD_D_INNER = SSD_HEADS * SSD_HEAD_DIM
SSD_STATE = 128
SSD_GROUPS = 2
SSD_CHUNK = 128
SSD_CONV_W = 5
SWA_HEADS = 16
SWA_KV_HEADS = 4
SWA_WINDOW = 128
SWA_BLOCK = 128
ROPE_BASE = 10000.0
N_EXPERTS = 16
EXPERT_FF = 2048
EC_CAPACITY = 2
NORM_EPS = 1e-6

N_CTX = BATCH * SEQ
N_LAT = DEC_BATCH * DEC_SEQ
N_ALL = N_CTX + N_LAT
NA_WIDTH = NA_HEADS * HEAD_DIM
SSD_BC = SSD_GROUPS * SSD_STATE
SSD_CONV_CH = SSD_D_INNER + 2 * SSD_BC
EVEN_IN = 3 * NA_WIDTH + SSD_D_INNER + SSD_CONV_CH + 2 * SSD_HEADS
EVEN_IN_PAD = 3200
SWA_Q = SWA_HEADS * HEAD_DIM
SWA_KV = SWA_KV_HEADS * HEAD_DIM
ODD_IN = SWA_Q + 2 * SWA_KV
CAP_CTX = EC_CAPACITY * N_CTX // N_EXPERTS
CAP_LAT = EC_CAPACITY * N_LAT // N_EXPERTS
CAP_ALL = CAP_CTX + CAP_LAT
LANES = 128
NEG_BIG = -1e30
ATT_SCALE = HEAD_DIM ** -0.5
VMEM_LIMIT = 56 * 1024 * 1024


def _params(sem):
    return pltpu.CompilerParams(dimension_semantics=sem, vmem_limit_bytes=VMEM_LIMIT)


def _group_of_block(m, tm):
    ctx_blocks = N_CTX // tm
    return lax.select(m < ctx_blocks, jnp.int32(0), 1 + (m - ctx_blocks) // (DEC_SEQ // tm))


def _norm_mod(x, g, shift, scale):
    ms = jnp.mean(x * x, axis=-1, keepdims=True)
    y = x * lax.rsqrt(ms + NORM_EPS) * g
    return y * (1.0 + scale) + shift


def _silu(x):
    return x * jax.nn.sigmoid(x)


def _softplus(x):
    return jnp.maximum(x, 0.0) + jnp.log1p(jnp.exp(-jnp.abs(x)))


def _dot(a, b, **kw):
    return jnp.dot(a, b, preferred_element_type=F32, **kw)


def _dot_nt(a, b, **kw):
    return lax.dot_general(a, b, (((1,), (1,)), ((), ())), preferred_element_type=F32, **kw)


def _split_bf16(x):
    hi = x.astype(BF16)
    return hi, (x - hi.astype(F32)).astype(BF16)


def _value_with_ones(v_tile, upper_half):
    lane = lax.broadcasted_iota(I32, v_tile.shape, 1)
    v = pltpu.roll(v_tile, HEAD_DIM, 1) if upper_half else v_tile
    return jnp.where(lane < HEAD_DIM, v, 1.0).astype(BF16)


def _softmax_pv(q, keys, vals, biases, extra=None):
    logits = []
    for k, b in zip(keys, biases):
        s = _dot_nt(q, k)
        logits.append(s if b is None else s + b)
    mx = extra
    for s in logits:
        for c in range(s.shape[1] // LANES):
            t = s[:, c * LANES:(c + 1) * LANES]
            mx = t if mx is None else jnp.maximum(mx, t)
    m = jnp.max(mx, axis=-1, keepdims=True)
    acc = None
    for s, v in zip(logits, vals):
        pv = _dot(jnp.exp(s - m).astype(BF16), v)
        acc = pv if acc is None else acc + pv
    den = pltpu.roll(acc, HEAD_DIM, 1)
    if extra is not None:
        den = den + jnp.exp(extra - m)
    return (acc / den)[:, 0:HEAD_DIM]


def _adaln_kernel(c_ref, w_ref, b_ref, o_ref):
    s_hi, s_lo = _split_bf16(_silu(c_ref[...]))
    w_hi, w_lo = _split_bf16(w_ref[0])
    o_ref[0] = _dot(s_hi, w_hi) + (_dot(s_lo, w_hi) + _dot(s_hi, w_lo)) + b_ref[0]


def _adaln(cond8, w_mod, b_mod):
    d = D_MODEL
    return pl.pallas_call(
        _adaln_kernel,
        grid=(DEPTH, 6),
        in_specs=[pl.BlockSpec((8, d), lambda l, n: (0, 0)),
                  pl.BlockSpec((1, d, d), lambda l, n: (l, 0, n)),
                  pl.BlockSpec((1, 1, d), lambda l, n: (l, 0, n))],
        out_specs=pl.BlockSpec((1, 8, d), lambda l, n: (l, 0, n)),
        out_shape=jax.ShapeDtypeStruct((DEPTH, 8, 6 * d), F32),
        compiler_params=_params(("arbitrary", "arbitrary")),
        name="adaln",
    )(cond8, w_mod, b_mod.reshape(DEPTH, 1, 6 * d))


def _proj_in_kernel(x_ref, g_ref, mod_ref, w_ref, o_ref, *, tm):
    grp = _group_of_block(pl.program_id(0), tm)
    shift = mod_ref[pl.ds(grp, 1), 0:D_MODEL]
    scale = mod_ref[pl.ds(grp, 1), D_MODEL:2 * D_MODEL]
    h = _norm_mod(x_ref[...], g_ref[...], shift, scale).astype(BF16)
    o_ref[...] = _dot(h, w_ref[...])


def _proj_in(x, g, mod_l, w_bf16, tm=512):
    n = w_bf16.shape[1]
    return pl.pallas_call(
        functools.partial(_proj_in_kernel, tm=tm),
        grid=(N_ALL // tm,),
        in_specs=[pl.BlockSpec((tm, D_MODEL), lambda m: (m, 0)),
                  pl.BlockSpec((1, D_MODEL), lambda m: (0, 0)),
                  pl.BlockSpec((8, 6 * D_MODEL), lambda m: (0, 0)),
                  pl.BlockSpec((D_MODEL, n), lambda m: (0, 0))],
        out_specs=pl.BlockSpec((tm, n), lambda m: (m, 0)),
        out_shape=jax.ShapeDtypeStruct((N_ALL, n), F32),
        compiler_params=_params(("arbitrary",)),
        name="proj_in",
    )(x, g, mod_l, w_bf16)


def _router_affinities(h, wr):
    h_hi, h_lo = _split_bf16(h)
    w_hi, w_lo = _split_bf16(wr)
    logits = _dot_nt(w_hi, h_hi) + (_dot_nt(w_hi, h_lo) + _dot_nt(w_lo, h_hi))
    e = jnp.exp(logits - jnp.max(logits, axis=0, keepdims=True))
    return e / jnp.sum(e, axis=0, keepdims=True)


def _out_proj_kernel(ac_ref, al_ref, bc_ref, bl_ref, w_ref, x_ref, mod_ref, g_ref, wr_ref, o_ref, h_ref, aff_ref,
                     *, tm):
    m = pl.program_id(0)
    grp = _group_of_block(m, tm)

    def mod_row(k):
        return mod_ref[pl.ds(grp, 1), k * D_MODEL:(k + 1) * D_MODEL]

    half = D_MODEL // 2
    is_ctx = m < N_CTX // tm
    a = jnp.where(is_ctx, ac_ref[...], al_ref[...]).astype(BF16)
    b = jnp.where(is_ctx, bc_ref[...], bl_ref[...]).astype(BF16)
    acc = _dot(a, w_ref[0:half, :]) + _dot(b, w_ref[half:, :])
    x_new = x_ref[...] + mod_row(2) * acc
    o_ref[...] = x_new
    h = _norm_mod(x_new, g_ref[...], mod_row(3), mod_row(4))
    h_ref[...] = h
    aff_ref[...] = _router_affinities(h, wr_ref[...])


def _out_proj(a_ctx, a_lat, acol, b_ctx, b_lat, bcol, w_bf16, x, mod_l, g_ffn, w_router_t, tm=512):
    half = D_MODEL // 2
    nctx = N_CTX // tm

    def ctx_map(col):
        return lambda m: (jnp.minimum(m, nctx - 1), col)

    def lat_map(col):
        return lambda m: (jnp.maximum(m - nctx, 0), col)

    return pl.pallas_call(
        functools.partial(_out_proj_kernel, tm=tm),
        grid=(N_ALL // tm,),
        in_specs=[pl.BlockSpec((tm, half), ctx_map(acol)),
                  pl.BlockSpec((tm, half), lat_map(acol)),
                  pl.BlockSpec((tm, half), ctx_map(bcol)),
                  pl.BlockSpec((tm, half), lat_map(bcol)),
                  pl.BlockSpec((D_MODEL, D_MODEL), lambda m: (0, 0)),
                  pl.BlockSpec((tm, D_MODEL), lambda m: (m, 0)),
                  pl.BlockSpec((8, 6 * D_MODEL), lambda m: (0, 0)),
                  pl.BlockSpec((1, D_MODEL), lambda m: (0, 0)),
                  pl.BlockSpec((N_EXPERTS, D_MODEL), lambda m: (0, 0))],
        out_specs=[pl.BlockSpec((tm, D_MODEL), lambda m: (m, 0)),
                   pl.BlockSpec((tm, D_MODEL), lambda m: (m, 0)),
                   pl.BlockSpec((N_EXPERTS, tm), lambda m: (0, m))],
        out_shape=[jax.ShapeDtypeStruct((N_ALL, D_MODEL), F32),
                   jax.ShapeDtypeStruct((N_ALL, D_MODEL), F32),
                   jax.ShapeDtypeStruct((N_EXPERTS, N_ALL), F32)],
        compiler_params=_params(("arbitrary",)),
        name="out_proj",
    )(a_ctx, a_lat, b_ctx, b_lat, w_bf16, x, mod_l, g_ffn, w_router_t)


def _ctx_na_kernel(q_ref, k_ref, v_ref, o_ref):
    for h in range(NA_HEADS):
        sl = slice(h * HEAD_DIM, (h + 1) * HEAD_DIM)
        tile = slice((h // 2) * LANES, (h // 2 + 1) * LANES)
        q = (q_ref[:, sl] * ATT_SCALE).astype(BF16)
        v = _value_with_ones(v_ref[:, tile], h % 2 == 1)
        o_ref[:, sl] = _softmax_pv(q, [k_ref[:, sl].astype(BF16)], [v], [None])


def _ctx_na(proj):
    w = NA_WIDTH
    return pl.pallas_call(
        _ctx_na_kernel,
        grid=(BATCH,),
        in_specs=[pl.BlockSpec((SEQ, w), lambda b: (b, 0)),
                  pl.BlockSpec((SEQ, w), lambda b: (b, 1)),
                  pl.BlockSpec((SEQ, w), lambda b: (b, 2))],
        out_specs=pl.BlockSpec((SEQ, w), lambda b: (b, 0)),
        out_shape=jax.ShapeDtypeStruct((N_CTX, w), F32),
        compiler_params=_params(("arbitrary",)),
        name="ctx_na",
    )(proj, proj, proj)


NA_QROWS = 4
NA_KROWS = 12
NA_ROWS = DEC_SEQ // GRID_W
NA_GROUPS = NA_ROWS // NA_QROWS


def _na_key_start(g):
    return jnp.clip(NA_QROWS * g - NA_KR // 2, 0, NA_ROWS - NA_KROWS)


def _na_row_geometry(g):
    start = int(np.clip(NA_QROWS * g - NA_KR // 2, 0, NA_ROWS - NA_KROWS))
    rows = [NA_QROWS * g + qr for qr in range(NA_QROWS)]
    return start, [(r, int(np.clip(r - NA_KR // 2, 0, NA_ROWS - NA_KR))) for r in rows]


def _na_fill_bias(ct_ref, bias_scr, g):
    start, rows = _na_row_geometry(g)
    masked = jnp.full((GRID_W, GRID_W), NEG_BIG, F32)
    for i in range(NA_HEADS):
        for qr, (r, rs) in enumerate(rows):
            for kr in range(NA_KROWS):
                keyrow = start + kr
                inside = rs <= keyrow < rs + NA_KR
                tile = ct_ref[i, keyrow - r + NA_KR - 1] if inside else masked
                bias_scr[i, qr * GRID_W:(qr + 1) * GRID_W, kr * GRID_W:(kr + 1) * GRID_W] = tile


def _lat_na_kernel(q_ref, k_ref, v_ref, ck_ref, cv_ref, ct_ref, o_ref, bias_scr, k_scr, v_scr, ck_scr, cv_scr):
    g = pl.program_id(1)
    for g_build in (0, 1, NA_GROUPS - 1):
        @pl.when(g == g_build)
        def _(g_build=g_build):
            _na_fill_bias(ct_ref, bias_scr, g_build)

    @pl.when(g == 0)
    def _():
        for i in range(NA_HEADS):
            sl = slice(i * HEAD_DIM, (i + 1) * HEAD_DIM)
            tile = slice((i // 2) * LANES, (i // 2 + 1) * LANES)
            k_scr[i] = k_ref[:, sl].astype(BF16)
            v_scr[i] = _value_with_ones(v_ref[:, tile], i % 2 == 1)
            ck_scr[i] = ck_ref[:, sl].astype(BF16)
            cv_scr[i] = _value_with_ones(cv_ref[:, tile], i % 2 == 1)

    start = pl.multiple_of(_na_key_start(g) * GRID_W, GRID_W)
    nk = NA_KROWS * GRID_W
    for i in range(NA_HEADS):
        sl = slice(i * HEAD_DIM, (i + 1) * HEAD_DIM)
        q = (q_ref[:, sl] * ATT_SCALE).astype(BF16)
        o_ref[:, sl] = _softmax_pv(q, [k_scr[i, pl.ds(start, nk), :], ck_scr[i]],
                                   [v_scr[i, pl.ds(start, nk), :], cv_scr[i]], [bias_scr[i], None])


def _na_col_table(rpb):
    w = np.arange(GRID_W)[:, None]
    cc = np.arange(GRID_W)[None, :]
    cs = np.clip(w - NA_KC // 2, 0, GRID_W - NA_KC)
    valid = (cc >= cs) & (cc < cs + NA_KC)
    dc = cc - w + NA_KC - 1
    onehot = ((dc[..., None] == np.arange(2 * NA_KC - 1)) & valid[..., None]).astype(np.float32)
    ct = jnp.einsum('hrd,wcd->hrwc', rpb, jnp.asarray(onehot), precision=HIGHEST)
    return jnp.where(jnp.asarray(valid)[None, None], ct, NEG_BIG)


def _lat_na(proj, ck, cv, col_tab):
    nq = NA_QROWS * GRID_W
    nk = NA_KROWS * GRID_W
    lat_q0 = N_CTX // nq
    lat_b0 = N_CTX // DEC_SEQ
    w = NA_WIDTH
    return pl.pallas_call(
        _lat_na_kernel,
        grid=(DEC_BATCH, NA_GROUPS),
        in_specs=[pl.BlockSpec((nq, w), lambda b, g: (lat_q0 + b * NA_GROUPS + g, 0)),
                  pl.BlockSpec((DEC_SEQ, w), lambda b, g: (lat_b0 + b, 1)),
                  pl.BlockSpec((DEC_SEQ, w), lambda b, g: (lat_b0 + b, 2)),
                  pl.BlockSpec((None, PAST_LEN, w), lambda b, g: (b, 0, 0)),
                  pl.BlockSpec((None, PAST_LEN, w), lambda b, g: (b, 0, 0)),
                  pl.BlockSpec((NA_HEADS, 2 * NA_KR - 1, GRID_W, GRID_W), lambda b, g: (0, 0, 0, 0))],
        out_specs=pl.BlockSpec((nq, w), lambda b, g: (b * NA_GROUPS + g, 0)),
        out_shape=jax.ShapeDtypeStruct((N_LAT, w), F32),
        scratch_shapes=[pltpu.VMEM((NA_HEADS, nq, nk), F32),
                        pltpu.VMEM((NA_HEADS, DEC_SEQ, HEAD_DIM), BF16),
                        pltpu.VMEM((NA_HEADS, DEC_SEQ, LANES), BF16),
                        pltpu.VMEM((NA_HEADS, PAST_LEN, HEAD_DIM), BF16),
                        pltpu.VMEM((NA_HEADS, PAST_LEN, LANES), BF16)],
        compiler_params=_params(("arbitrary", "arbitrary")),
        name="lat_na",
    )(proj, proj, proj, ck, cv, col_tab)


def _ssd_kernel(*refs, seq, has_h0, want_state):
    it = iter(refs)
    z_ref, xbc_ref, dt_ref = next(it), next(it), next(it)
    h0_ref = next(it) if has_h0 else None
    cw_ref, cb_ref, a_ref, dtb_ref, dsk_ref, ng_ref = (next(it) for _ in range(6))
    y_ref = next(it)
    st_ref = next(it) if want_state else None
    xc_scr, y_scr, ht_scr = next(it), next(it), next(it)

    nc = seq // SSD_CHUNK
    ch = SSD_CHUNK
    row = lax.broadcasted_iota(I32, (ch, ch), 0)
    col = lax.broadcasted_iota(I32, (ch, ch), 1)
    erow = lax.broadcasted_iota(I32, (LANES, SSD_D_INNER), 0)
    ecol = lax.broadcasted_iota(I32, (LANES, SSD_D_INNER), 1) // SSD_HEAD_DIM
    cbias = cb_ref[...]

    def conv_chunk(c, carry):
        base = pl.multiple_of(c * ch, ch)
        cur = xbc_ref[pl.ds(base, ch), :]
        pbase = pl.multiple_of(jnp.maximum(base - 8, 0), 8)
        nbase = pl.multiple_of(jnp.minimum(base + ch, seq - 8), 8)
        prev = jnp.where(c > 0, xbc_ref[pl.ds(pbase, 8), :], 0.0)
        nxt = jnp.where(c < nc - 1, xbc_ref[pl.ds(nbase, 8), :], 0.0)
        win = jnp.concatenate([prev, cur, nxt], axis=0)
        acc = jnp.broadcast_to(cbias, (ch, SSD_CONV_CH))
        pad = SSD_CONV_W // 2
        for k in range(SSD_CONV_W):
            off = 8 - pad + k
            acc = acc + win[off:off + ch, :] * cw_ref[k:k + 1, :]
        xc_scr[pl.ds(base, ch), :] = _silu(acc)
        return carry

    lax.fori_loop(0, nc, conv_chunk, 0)

    def run_direction(d):
        lane0 = d * SSD_HEADS
        expand = jnp.where(erow == ecol + lane0, 1.0, 0.0).astype(BF16)
        lmask = (col <= row) if d == 0 else (col >= row)
        tri = jnp.where(lmask, 1.0, 0.0).astype(BF16)

        def dot_split(lhs01, x, lhs_first):
            hi = x.astype(BF16)
            lo = (x - hi.astype(F32)).astype(BF16)
            if lhs_first:
                return _dot(lhs01, hi) + _dot(lhs01, lo)
            return _dot(hi, lhs01) + _dot(lo, lhs01)
        if has_h0:
            ht_scr[...] = h0_ref[d]
        else:
            ht_scr[...] = jnp.zeros((SSD_STATE, SSD_D_INNER), F32)

        def chunk(step, carry):
            c = step if d == 0 else nc - 1 - step
            base = pl.multiple_of(c * ch, ch)
            xs = xc_scr[pl.ds(base, ch), 0:SSD_D_INNER]
            dt = _softplus(dt_ref[pl.ds(base, ch), :] + dtb_ref[...])
            a = dt * a_ref[...]
            cum = dot_split(tri, a, True)
            tot = cum[ch - 1:ch, :] if d == 0 else cum[0:1, :]
            cum_t = cum.T
            dt_x = dot_split(expand, dt, False)
            cum_x = dot_split(expand, cum, False)
            tot_x = dot_split(expand, jnp.broadcast_to(tot, (8, LANES)), False)[0:1, :]
            xt = xs * dt_x
            xd = (xt * jnp.exp(tot_x - cum_x)).astype(BF16)
            xt_b = xt.astype(BF16)
            y_parts = []
            for grp in range(SSD_GROUPS):
                bsl = slice(SSD_D_INNER + grp * SSD_STATE, SSD_D_INNER + (grp + 1) * SSD_STATE)
                csl = slice(SSD_D_INNER + SSD_BC + grp * SSD_STATE, SSD_D_INNER + SSD_BC + (grp + 1) * SSD_STATE)
                b_f = xc_scr[pl.ds(base, ch), bsl]
                b_g = b_f.astype(BF16)
                c_g = xc_scr[pl.ds(base, ch), csl].astype(BF16)
                cb = _dot_nt(c_g, b_g)
                hsl = slice(grp * 4 * SSD_HEAD_DIM, (grp + 1) * 4 * SSD_HEAD_DIM)
                ht_g = ht_scr[:, hsl]
                y_off = _dot(c_g, ht_g.astype(BF16))
                for hh in range(4):
                    head = grp * 4 + hh
                    lane = lane0 + head
                    cum_col = cum[:, lane:lane + 1]
                    cum_row = cum_t[lane:lane + 1, :]
                    ldec = jnp.exp(jnp.where(lmask, cum_col - cum_row, NEG_BIG))
                    psl = slice(head * SSD_HEAD_DIM, (head + 1) * SSD_HEAD_DIM)
                    y_d = _dot((cb * ldec).astype(BF16), xt_b[:, psl])
                    y_parts.append(y_d + y_off[:, hh * SSD_HEAD_DIM:(hh + 1) * SSD_HEAD_DIM] * jnp.exp(cum_col))
                ht_scr[:, hsl] = ht_g * jnp.exp(tot_x[:, hsl]) + _dot(b_f.T.astype(BF16), xd[:, hsl])
            y = jnp.concatenate(y_parts, axis=1)
            if d == 0:
                y_scr[pl.ds(base, ch), :] = y
            else:
                y = y + y_scr[pl.ds(base, ch), :] + dsk_ref[...] * xs
                u = y * _silu(z_ref[pl.ds(base, ch), :])
                ms = jnp.mean(u * u, axis=-1, keepdims=True)
                y_ref[pl.ds(base, ch), :] = u * lax.rsqrt(ms + NORM_EPS) * ng_ref[...]
            return carry

        lax.fori_loop(0, nc, chunk, 0, unroll=2)
        if want_state:
            st_ref[d] = ht_scr[...]

    run_direction(0)
    run_direction(1)


def _ssd(proj, h0t, consts, *, seq, nbatch, row_blk0, want_state):
    has_h0 = h0t is not None
    in_specs = [pl.BlockSpec((seq, SSD_D_INNER), lambda b: (row_blk0 + b, 3)),
                pl.BlockSpec((seq, SSD_CONV_CH), lambda b: (row_blk0 + b, 2)),
                pl.BlockSpec((seq, LANES), lambda b: (row_blk0 + b, 24))]
    args = [proj, proj, proj]
    if has_h0:
        in_specs.append(pl.BlockSpec((None, 2, SSD_STATE, SSD_D_INNER), lambda b: (b, 0, 0, 0)))
        args.append(h0t)
    for cst in consts:
        in_specs.append(pl.BlockSpec(cst.shape, lambda b: (0, 0)))
        args.append(cst)
    out_specs = [pl.BlockSpec((seq, SSD_D_INNER), lambda b: (b, 0))]
    out_shape = [jax.ShapeDtypeStruct((nbatch * seq, SSD_D_INNER), F32)]
    if want_state:
        out_specs.append(pl.BlockSpec((None, 2, SSD_STATE, SSD_D_INNER), lambda b: (b, 0, 0, 0)))
        out_shape.append(jax.ShapeDtypeStruct((nbatch, 2, SSD_STATE, SSD_D_INNER), F32))
    res = pl.pallas_call(
        functools.partial(_ssd_kernel, seq=seq, has_h0=has_h0, want_state=want_state),
        grid=(nbatch,),
        in_specs=in_specs,
        out_specs=out_specs,
        out_shape=out_shape,
        scratch_shapes=[pltpu.VMEM((seq, SSD_CONV_CH), F32),
                        pltpu.VMEM((seq, SSD_D_INNER), F32),
                        pltpu.VMEM((SSD_STATE, SSD_D_INNER), F32)],
        compiler_params=_params(("arbitrary",)),
        name="ssd_seq%d" % seq,
    )(*args)
    return res if want_state else (res[0], None)


def _ctx_swa_kernel(q_ref, k_ref, v_ref, sink_ref, o_ref):
    grp = SWA_HEADS // SWA_KV_HEADS
    for kv in range(SWA_KV_HEADS):
        ksl = slice(kv * HEAD_DIM, (kv + 1) * HEAD_DIM)
        tile = slice((kv // 2) * LANES, (kv // 2 + 1) * LANES)
        k = k_ref[:, ksl].astype(BF16)
        v = _value_with_ones(v_ref[:, tile], kv % 2 == 1)
        q4 = jnp.concatenate(
            [q_ref[:, (kv * grp + j) * HEAD_DIM:(kv * grp + j + 1) * HEAD_DIM] for j in range(grp)],
            axis=0)
        o = _softmax_pv((q4 * ATT_SCALE).astype(BF16), [k], [v], [None], extra=sink_ref[kv])
        for j in range(grp):
            o_ref[:, (kv * grp + j) * HEAD_DIM:(kv * grp + j + 1) * HEAD_DIM] = o[j * SEQ:(j + 1) * SEQ]


def _ctx_swa(proj, sink_col):
    return pl.pallas_call(
        _ctx_swa_kernel,
        grid=(BATCH,),
        in_specs=[pl.BlockSpec((SEQ, SWA_Q), lambda b: (b, 0)),
                  pl.BlockSpec((SEQ, SWA_KV), lambda b: (b, 4)),
                  pl.BlockSpec((SEQ, SWA_KV), lambda b: (b, 5)),
                  pl.BlockSpec(sink_col.shape, lambda b: (0, 0, 0))],
        out_specs=pl.BlockSpec((SEQ, SWA_Q), lambda b: (b, 0)),
        out_shape=jax.ShapeDtypeStruct((N_CTX, SWA_Q), F32),
        compiler_params=_params(("arbitrary",)),
        name="ctx_swa",
    )(proj, proj, proj, sink_col)


SWA_NLOC = 3 * SWA_BLOCK


def _rope(x, cos, sin_signed, first):
    n = x.shape[-1]
    partner = jnp.where(first, pltpu.roll(x, n - 16, 1), pltpu.roll(x, 16, 1))
    return x * cos + partner * sin_signed


def _lat_swa_kernel(q_ref, k_ref, v_ref, ck_ref, cv_ref, cosk_ref, sink_k_ref, cosq_ref, sinq_ref, sink_ref,
                    o_ref, kr_scr, v_scr, ck_scr, cv_scr):
    i = pl.program_id(1)
    grp = SWA_HEADS // SWA_KV_HEADS
    lane = lax.broadcasted_iota(I32, (1, SWA_KV), 1)
    first = (lane % 32) < 16

    @pl.when(i == 0)
    def _():
        kr = _rope(k_ref[...], cosk_ref[...], sink_k_ref[...], first)
        for kv in range(SWA_KV_HEADS):
            ksl = slice(kv * HEAD_DIM, (kv + 1) * HEAD_DIM)
            tile = slice((kv // 2) * LANES, (kv // 2 + 1) * LANES)
            kr_scr[kv] = kr[:, ksl].astype(BF16)
            v_scr[kv] = _value_with_ones(v_ref[:, tile], kv % 2 == 1)
            ck_scr[kv] = ck_ref[:, ksl].astype(BF16)
            cv_scr[kv] = _value_with_ones(cv_ref[:, tile], kv % 2 == 1)

    kstart = pl.multiple_of(jnp.clip((i - 1) * SWA_BLOCK, 0, DEC_SEQ - SWA_NLOC), SWA_BLOCK)
    qpos = i * SWA_BLOCK + lax.broadcasted_iota(I32, (SWA_BLOCK, SWA_NLOC), 0)
    kpos = kstart + lax.broadcasted_iota(I32, (SWA_BLOCK, SWA_NLOC), 1)
    mask1 = jnp.where(jnp.abs(qpos - kpos) <= SWA_WINDOW, 0.0, NEG_BIG)
    mask = jnp.concatenate([mask1] * grp, axis=0)
    cosq = cosq_ref[...]
    sinq = sinq_ref[...]
    for kv in range(SWA_KV_HEADS):
        qr = _rope(q_ref[:, kv * SWA_KV:(kv + 1) * SWA_KV], cosq, sinq, first) * ATT_SCALE
        q4 = jnp.concatenate([qr[:, j * HEAD_DIM:(j + 1) * HEAD_DIM] for j in range(grp)], axis=0).astype(BF16)
        o = _softmax_pv(q4, [kr_scr[kv, pl.ds(kstart, SWA_NLOC), :], ck_scr[kv]],
                        [v_scr[kv, pl.ds(kstart, SWA_NLOC), :], cv_scr[kv]], [mask, None], extra=sink_ref[kv])
        for j in range(grp):
            o_ref[:, (kv * grp + j) * HEAD_DIM:(kv * grp + j + 1) * HEAD_DIM] = o[j * SWA_BLOCK:(j + 1) * SWA_BLOCK]


def _rope_tables():
    half = HEAD_DIM // 2
    quarter = half // 2
    pos = jnp.arange(DEC_SEQ)
    inv_freq = 1.0 / (ROPE_BASE ** (jnp.arange(quarter, dtype=F32) * 2.0 / half))
    d = np.arange(HEAD_DIM)
    use_col = jnp.asarray(d >= half)
    p = jnp.where(use_col[None, :], (pos % GRID_W)[:, None], (pos // GRID_W)[:, None]).astype(F32)
    ang = p * inv_freq[d % quarter][None, :]
    sign = jnp.asarray(np.where((d % half) < quarter, -1.0, 1.0), F32)
    cos = jnp.cos(ang)
    sin_signed = jnp.sin(ang) * sign[None, :]
    reps = SWA_KV // HEAD_DIM
    return jnp.tile(cos, (1, reps)), jnp.tile(sin_signed, (1, reps))


def _lat_swa(proj, ck, cv, cos_t, sin_t, sink_col):
    nb = DEC_SEQ // SWA_BLOCK
    q0 = N_CTX // SWA_BLOCK
    b0 = N_CTX // DEC_SEQ
    return pl.pallas_call(
        _lat_swa_kernel,
        grid=(DEC_BATCH, nb),
        in_specs=[pl.BlockSpec((SWA_BLOCK, SWA_Q), lambda b, i: (q0 + b * nb + i, 0)),
                  pl.BlockSpec((DEC_SEQ, SWA_KV), lambda b, i: (b0 + b, 4)),
                  pl.BlockSpec((DEC_SEQ, SWA_KV), lambda b, i: (b0 + b, 5)),
                  pl.BlockSpec((None, PAST_LEN, SWA_KV), lambda b, i: (b, 0, 0)),
                  pl.BlockSpec((None, PAST_LEN, SWA_KV), lambda b, i: (b, 0, 0)),
                  pl.BlockSpec((DEC_SEQ, SWA_KV), lambda b, i: (0, 0)),
                  pl.BlockSpec((DEC_SEQ, SWA_KV), lambda b, i: (0, 0)),
                  pl.BlockSpec((SWA_BLOCK, SWA_KV), lambda b, i: (i, 0)),
                  pl.BlockSpec((SWA_BLOCK, SWA_KV), lambda b, i: (i, 0)),
                  pl.BlockSpec(sink_col.shape, lambda b, i: (0, 0, 0))],
        out_specs=pl.BlockSpec((SWA_BLOCK, SWA_Q), lambda b, i: (b * nb + i, 0)),
        out_shape=jax.ShapeDtypeStruct((N_LAT, SWA_Q), F32),
        scratch_shapes=[pltpu.VMEM((SWA_KV_HEADS, DEC_SEQ, HEAD_DIM), BF16),
                        pltpu.VMEM((SWA_KV_HEADS, DEC_SEQ, LANES), BF16),
                        pltpu.VMEM((SWA_KV_HEADS, PAST_LEN, HEAD_DIM), BF16),
                        pltpu.VMEM((SWA_KV_HEADS, PAST_LEN, LANES), BF16)],
        compiler_params=_params(("arbitrary", "arbitrary")),
        name="lat_swa",
    )(proj, proj, proj, ck, cv, cos_t, sin_t, cos_t, sin_t, sink_col)


def _route_kernel(aff_ref, idx_ref, gate_ref, cend_ref, pos_ref, t_scr, *, nb, cap):
    ne = N_EXPERTS
    capf = float(cap)

    def count_ge(e, cand):
        hit = jnp.where(aff_ref[e] >= cand, 1.0, 0.0)
        return jnp.sum(jnp.sum(hit, axis=1, keepdims=True), axis=0, keepdims=True)

    def bit_step(i, ts):
        bit = jnp.left_shift(jnp.int32(1), 30 - i)
        out = []
        for e in range(ne):
            cand = ts[e] | bit
            keep = count_ge(e, lax.bitcast_convert_type(cand, F32)) >= capf
            out.append(jnp.where(keep, cand, ts[e]))
        return tuple(out)

    ts = lax.fori_loop(0, 31, bit_step, tuple(jnp.zeros((1, 1), I32) for _ in range(ne)))
    for e in range(ne):
        t_scr[e] = jnp.broadcast_to(lax.bitcast_convert_type(ts[e], F32), (8, LANES))

    r = lax.broadcasted_iota(I32, (LANES, LANES), 0)
    c = lax.broadcasted_iota(I32, (LANES, LANES), 1)
    upper = jnp.where(r <= c, 1.0, 0.0).astype(BF16)
    lower_incl = jnp.where(c <= r, 1.0, 0.0).astype(BF16)
    lower_strict = jnp.where(c < r, 1.0, 0.0).astype(BF16)
    npad = LANES - nb
    slot = lax.broadcasted_iota(I32, (LANES, cap), 1).astype(F32)
    sub = lax.broadcasted_iota(I32, (LANES, cap), 0).astype(F32)

    def cumsum_tokens(x):
        rowc = _dot(x.astype(BF16), upper)
        tot = jnp.broadcast_to(rowc[:, LANES - 1:LANES], (LANES, LANES))
        return rowc + _dot(lower_strict, tot.astype(BF16)), tot

    def per_expert(e, carry):
        a = aff_ref[e]
        if npad:
            a = jnp.concatenate([a, jnp.full((npad, LANES), -1.0, F32)], axis=0)
        thr = t_scr[e][0:1, 0:1]
        gt = a > thr
        eq = a == thr
        gtf = jnp.where(gt, 1.0, 0.0)
        eqf = jnp.where(eq, 1.0, 0.0)
        need = capf - jnp.sum(jnp.sum(gtf, axis=1, keepdims=True), axis=0, keepdims=True)
        eq_incl, _ = cumsum_tokens(eqf)
        sel = gt | (eq & (eq_incl - eqf < need))
        self = jnp.where(sel, 1.0, 0.0)
        incl, tot = cumsum_tokens(self)
        cend = _dot(lower_incl, tot.astype(BF16))
        blk = jnp.sum(jnp.where(cend[:, 0:1] <= slot, 1.0, 0.0), axis=0, keepdims=True)
        onehot = jnp.where(sub == blk, 1.0, 0.0)
        hi = jnp.floor(incl * (1.0 / LANES))
        lo = incl - hi * LANES
        a_pos = jnp.maximum(a, 0.0)
        a1 = a_pos.astype(BF16)
        r1 = a_pos - a1.astype(F32)
        a2 = r1.astype(BF16)
        a3 = (r1 - a2.astype(F32)).astype(BF16)
        stacked = jnp.concatenate([hi.T.astype(BF16), lo.T.astype(BF16), a1.T, a2.T, a3.T], axis=0)
        picked = _dot(stacked, onehot.astype(BF16))
        inc_s = picked[0:LANES] * LANES + picked[LANES:2 * LANES]
        within = jnp.sum(jnp.where(inc_s <= slot, 1.0, 0.0), axis=0, keepdims=True)
        aff_s = picked[2 * LANES:3 * LANES] + picked[3 * LANES:4 * LANES] + picked[4 * LANES:5 * LANES]
        gate = jnp.sum(jnp.where(sub == within, aff_s, 0.0), axis=0, keepdims=True)
        idx_ref[e] = (blk * LANES + within).astype(I32)
        gate_ref[e] = gate
        cend_ref[e] = cend.T[0:8, :].astype(I32)
        pos_ref[e] = jnp.where(sel, incl - 1.0, -1.0)[0:nb, :]
        return carry

    lax.fori_loop(0, ne, per_expert, 0, unroll=2)


def _route(aff3, cap):
    ne, nb, _ = aff3.shape
    return pl.pallas_call(
        functools.partial(_route_kernel, nb=nb, cap=cap),
        grid=(1,),
        in_specs=[pl.BlockSpec((ne, nb, LANES), lambda i: (0, 0, 0))],
        out_specs=[pl.BlockSpec((ne, 1, cap), lambda i: (0, 0, 0)),
                   pl.BlockSpec((ne, 1, cap), lambda i: (0, 0, 0)),
                   pl.BlockSpec((ne, 8, LANES), lambda i: (0, 0, 0)),
                   pl.BlockSpec((ne, nb, LANES), lambda i: (0, 0, 0))],
        out_shape=[jax.ShapeDtypeStruct((ne, 1, cap), I32),
                   jax.ShapeDtypeStruct((ne, 1, cap), F32),
                   jax.ShapeDtypeStruct((ne, 8, LANES), I32),
                   jax.ShapeDtypeStruct((ne, nb, LANES), F32)],
        scratch_shapes=[pltpu.VMEM((ne, 8, LANES), F32)],
        compiler_params=_params(("arbitrary",)),
        name="route_cap%d" % cap,
    )(aff3)


FF_TILE = 512
COMB_PIECE = 32
CAP_PAD = CAP_ALL + COMB_PIECE


def _moe_ffn_kernel(idx_ref, h_hbm, gate_ref, wg_ref, wu_ref, wd_ref, o_ref, land, xe, acc, g_scr, u_scr, sem):
    e = pl.program_id(0)
    f = pl.program_id(1)
    nf = EXPERT_FF // FF_TILE
    rps = CAP_ALL // nf

    def row_copy(expert, part, s):
        tok = idx_ref[expert * CAP_ALL + part * rps + s]
        return pltpu.make_async_copy(h_hbm.at[pl.ds(tok, 1), :], land.at[part, pl.ds(s, 1), :], sem)

    def wait_rows():
        for part in range(nf):
            pltpu.make_async_copy(h_hbm.at[pl.ds(0, rps), :], land.at[part], sem).wait()

    @pl.when(f == 0)
    def _():
        @pl.when(e == 0)
        def _():
            for part in range(nf):
                def issue(s, carry, part=part):
                    row_copy(0, part, s).start()
                    return carry

                lax.fori_loop(0, rps, issue, 0)

        wait_rows()
        for part in range(nf):
            xe[part * rps:(part + 1) * rps, :] = land[part].astype(BF16)
        acc[...] = jnp.zeros_like(acc)

    nxt = (e + 1) % N_EXPERTS
    third = rps // 3

    def request_rows(lo, hi):
        for s in range(lo, hi):
            row_copy(nxt, f, s).start()

    @pl.when(f >= 0)
    def _():
        request_rows(0, third)
        g_scr[...] = _dot(xe[...], wg_ref[...].astype(BF16))

    @pl.when(e >= 0)
    def _():
        request_rows(third, 2 * third)
        u_scr[...] = _dot(xe[...], wu_ref[...].astype(BF16))

    request_rows(2 * third, rps)
    hid = (_silu(g_scr[...]) * u_scr[...]).astype(BF16)
    acc[...] += _dot(hid, wd_ref[...].astype(BF16))

    @pl.when(f == nf - 1)
    def _():
        y = acc[...] * gate_ref[...]
        hi = y.astype(BF16)
        o_ref[0:CAP_ALL, 0:D_MODEL] = hi
        o_ref[0:CAP_ALL, D_MODEL:2 * D_MODEL] = (y - hi.astype(F32)).astype(BF16)
        o_ref[CAP_ALL:CAP_PAD, :] = jnp.zeros((CAP_PAD - CAP_ALL, 2 * D_MODEL), BF16)

        @pl.when(e == N_EXPERTS - 1)
        def _():
            wait_rows()


def _moe_ffn(layer, h, idx_flat, gate_col, w_gate, w_up, w_down):
    nf = EXPERT_FF // FF_TILE
    grid_spec = pltpu.PrefetchScalarGridSpec(
        num_scalar_prefetch=1,
        grid=(N_EXPERTS, nf),
        in_specs=[pl.BlockSpec(memory_space=pl.ANY),
                  pl.BlockSpec((None, CAP_ALL, 1), lambda e, f, idx: (e, 0, 0)),
                  pl.BlockSpec((None, None, D_MODEL, FF_TILE), lambda e, f, idx: (layer, e, 0, f)),
                  pl.BlockSpec((None, None, D_MODEL, FF_TILE), lambda e, f, idx: (layer, e, 0, f)),
                  pl.BlockSpec((None, None, FF_TILE, D_MODEL), lambda e, f, idx: (layer, e, f, 0))],
        out_specs=pl.BlockSpec((None, CAP_PAD, 2 * D_MODEL), lambda e, f, idx: (e, 0, 0)),
        scratch_shapes=[pltpu.VMEM((nf, CAP_ALL // nf, D_MODEL), F32),
                        pltpu.VMEM((CAP_ALL, D_MODEL), BF16),
                        pltpu.VMEM((CAP_ALL, D_MODEL), F32),
                        pltpu.VMEM((CAP_ALL, FF_TILE), F32),
                        pltpu.VMEM((CAP_ALL, FF_TILE), F32),
                        pltpu.SemaphoreType.DMA(())],
    )
    return pl.pallas_call(
        _moe_ffn_kernel,
        grid_spec=grid_spec,
        out_shape=jax.ShapeDtypeStruct((N_EXPERTS, CAP_PAD, 2 * D_MODEL), BF16),
        compiler_params=_params(("arbitrary", "arbitrary")),
        name="moe_ffn",
    )(idx_flat, h, gate_col, w_gate, w_up, w_down)


COMB_TM = 512
COMB_NB = N_ALL // COMB_TM
COMB_HALF = N_EXPERTS // 2
COMB_CHUNK = 256
COMB_ALIGN = 16
COMB_WAIT_GROUP = 32
COMB_STAGE = -(-(COMB_HALF * (COMB_TM + COMB_PIECE + COMB_ALIGN)) // COMB_CHUNK) * COMB_CHUNK
ROUTE_NB = N_ALL // LANES


def _combine_kernel(st_ref, ye_hbm, pos_ref, x_ref, mod_ref, o_ref, stage, acc, sem):
    b = pl.program_id(0)
    nst = ROUTE_NB + 1
    per = COMB_TM // LANES

    def layout(blk, half):
        out = []
        off = jnp.int32(0)
        for j in range(COMB_HALF):
            e = half * COMB_HALF + j
            s0 = st_ref[e * nst + per * blk]
            s1 = st_ref[e * nst + per * blk + per]
            a0 = (s0 // COMB_ALIGN) * COMB_ALIGN
            npc = jnp.where(s1 > s0, (s1 - a0 + COMB_PIECE - 1) // COMB_PIECE, 0)
            out.append((e, a0, off, npc))
            off = off + npc * COMB_PIECE
        return out, off // COMB_PIECE

    def piece_copy(e, src_row, buf, dst_row, pieces=1):
        rows = pieces * COMB_PIECE
        return pltpu.make_async_copy(ye_hbm.at[e, pl.ds(src_row, rows), :],
                                     stage.at[buf, pl.ds(dst_row, rows), :], sem.at[buf])

    def issue_unit(blk, half):
        lay, _ = layout(blk, half)
        for e, a0, off, npc in lay:
            def issue(k, carry, e=e, a0=a0, off=off):
                piece_copy(e, pl.multiple_of(a0 + k * COMB_PIECE, COMB_ALIGN), half,
                           pl.multiple_of(off + k * COMB_PIECE, COMB_ALIGN)).start()
                return carry

            lax.fori_loop(0, npc, issue, 0)

    @pl.when(b == 0)
    def _():
        stage[...] = jnp.zeros_like(stage)
        issue_unit(0, 0)

    acc[...] = jnp.zeros_like(acc)
    pos = pos_ref[...]
    row_t = lax.broadcasted_iota(I32, (COMB_CHUNK, COMB_TM), 0).astype(F32)
    row_l = lax.broadcasted_iota(I32, (COMB_CHUNK, LANES), 0).astype(F32)
    lane = lax.broadcasted_iota(I32, (1, LANES), 1)
    for half in range(2):
        lay, npieces = layout(b, half)

        def wait_big(k, carry, half=half):
            piece_copy(0, 0, half, 0, COMB_WAIT_GROUP).wait()
            return carry

        lax.fori_loop(0, npieces // COMB_WAIT_GROUP, wait_big, 0)
        small = COMB_WAIT_GROUP // 2
        while small >= 1:
            @pl.when((npieces & small) != 0)
            def _(small=small, half=half):
                piece_copy(0, 0, half, 0, small).wait()

            small //= 2
        if half == 0:
            issue_unit(b, 1)
        else:
            @pl.when(b + 1 < COMB_NB)
            def _():
                issue_unit(b + 1, 0)

        srows = []
        first = jnp.full((1, LANES), 1e9, F32)
        last = jnp.zeros((1, LANES), F32)
        for j, (e, a0, off, npc) in enumerate(lay):
            p = pos[e:e + 1, :]
            srows.append(jnp.where(p >= 0.0, p + (off - a0).astype(F32), -1.0))
            first = jnp.where(lane == j, off.astype(F32), first)
            last = jnp.where(lane == j, (off + npc * COMB_PIECE).astype(F32), last)
        srow = jnp.concatenate(srows + [jnp.zeros((LANES - COMB_HALF, COMB_TM), F32)], axis=0)
        srow_hi = jnp.floor(srow * (1.0 / 64.0))
        srow_lo = (srow - 64.0 * srow_hi).astype(BF16)
        srow_hi = srow_hi.astype(BF16)

        def select_rows(ci, srow_hi=srow_hi, srow_lo=srow_lo, first=first, last=last):
            r0f = (ci * COMB_CHUNK).astype(F32)
            rid = row_l + r0f
            owner = jnp.where((rid >= first) & (rid < last), 1.0, 0.0).astype(BF16)
            want = 64.0 * _dot(owner, srow_hi) + _dot(owner, srow_lo)
            return jnp.where(want == row_t + r0f, 1.0, 0.0).T.astype(BF16)

        def chunk(ci, sel_t, half=half, select_rows=select_rows):
            sel_next = select_rows(ci + 1)
            r0 = pl.multiple_of(ci * COMB_CHUNK, COMB_CHUNK)
            acc[...] += (_dot(sel_t, stage[half, pl.ds(r0, COMB_CHUNK), 0:D_MODEL])
                         + _dot(sel_t, stage[half, pl.ds(r0, COMB_CHUNK), D_MODEL:2 * D_MODEL]))
            return sel_next

        nchunks = (npieces * COMB_PIECE + COMB_CHUNK - 1) // COMB_CHUNK
        lax.fori_loop(0, nchunks, chunk, select_rows(jnp.int32(0)))

    grp = _group_of_block(b, COMB_TM)
    gate = mod_ref[pl.ds(grp, 1), 5 * D_MODEL:6 * D_MODEL]
    o_ref[...] = x_ref[...] + gate * acc[...]


def _combine(ye, starts_flat, pos, x, mod_l):
    grid_spec = pltpu.PrefetchScalarGridSpec(
        num_scalar_prefetch=1,
        grid=(COMB_NB,),
        in_specs=[pl.BlockSpec(memory_space=pl.ANY),
                  pl.BlockSpec((N_EXPERTS, COMB_TM), lambda b, st: (0, b)),
                  pl.BlockSpec((COMB_TM, D_MODEL), lambda b, st: (b, 0)),
                  pl.BlockSpec((8, 6 * D_MODEL), lambda b, st: (0, 0))],
        out_specs=pl.BlockSpec((COMB_TM, D_MODEL), lambda b, st: (b, 0)),
        scratch_shapes=[pltpu.VMEM((2, COMB_STAGE, 2 * D_MODEL), BF16),
                        pltpu.VMEM((COMB_TM, D_MODEL), F32),
                        pltpu.SemaphoreType.DMA((2,))],
    )
    return pl.pallas_call(
        _combine_kernel,
        grid_spec=grid_spec,
        out_shape=jax.ShapeDtypeStruct((N_ALL, D_MODEL), F32),
        compiler_params=_params(("arbitrary",)),
        name="moe_combine",
    )(starts_flat, ye, pos, x, mod_l)


def _moe_layer(layer, x, h, aff, mod_l, w_gate, w_up, w_down):
    nb_ctx = N_CTX // LANES
    nb_lat = N_LAT // LANES
    idx_c, gate_c, cend_c, pos_c = _route(aff[:, :N_CTX].reshape(N_EXPERTS, nb_ctx, LANES), CAP_CTX)
    idx_l, gate_l, cend_l, pos_l = _route(aff[:, N_CTX:].reshape(N_EXPERTS, nb_lat, LANES), CAP_LAT)
    idx = jnp.concatenate([idx_c[:, 0, :], idx_l[:, 0, :] + N_CTX], axis=1)
    gate = jnp.concatenate([gate_c[:, 0, :], gate_l[:, 0, :]], axis=1)
    starts = jnp.concatenate([jnp.zeros((N_EXPERTS, 1), I32), cend_c[:, 0, :nb_ctx],
                              cend_l[:, 0, :nb_lat] + CAP_CTX], axis=1)
    pos_l = pos_l.reshape(N_EXPERTS, N_LAT)
    pos = jnp.concatenate([pos_c.reshape(N_EXPERTS, N_CTX),
                           jnp.where(pos_l >= 0.0, pos_l + CAP_CTX, -1.0)], axis=1)
    ye = _moe_ffn(layer, h, idx.reshape(-1), gate.reshape(N_EXPERTS, CAP_ALL, 1), w_gate, w_up, w_down)
    return _combine(ye, starts.reshape(-1), pos, x, mod_l)


def _final_norm_kernel(x_ref, g_ref, oc_ref, ol_ref, *, tm):
    x = x_ref[...]
    ms = jnp.mean(x * x, axis=-1, keepdims=True)
    y = x * lax.rsqrt(ms + NORM_EPS) * g_ref[...]
    is_ctx = pl.program_id(0) < N_CTX // tm

    @pl.when(is_ctx)
    def _():
        oc_ref[...] = y

    @pl.when(jnp.logical_not(is_ctx))
    def _():
        ol_ref[...] = y


def _final_norm(x, g, tm=1024):
    nctx = N_CTX // tm
    return pl.pallas_call(
        functools.partial(_final_norm_kernel, tm=tm),
        grid=(N_ALL // tm,),
        in_specs=[pl.BlockSpec((tm, D_MODEL), lambda m: (m, 0)),
                  pl.BlockSpec((1, D_MODEL), lambda m: (0, 0))],
        out_specs=[pl.BlockSpec((tm, D_MODEL), lambda m: (jnp.minimum(m, nctx - 1), 0)),
                   pl.BlockSpec((tm, D_MODEL), lambda m: (jnp.maximum(m - nctx, 0), 0))],
        out_shape=[jax.ShapeDtypeStruct((N_CTX, D_MODEL), F32),
                   jax.ShapeDtypeStruct((N_LAT, D_MODEL), F32)],
        compiler_params=_params(("arbitrary",)),
        name="final_norm",
    )(x, g)


def _sink_rows(sink, rows):
    grp = SWA_HEADS // SWA_KV_HEADS
    col = jnp.repeat(sink.reshape(SWA_KV_HEADS, grp), rows, axis=1).reshape(SWA_KV_HEADS, grp * rows, 1)
    return jnp.broadcast_to(col, (SWA_KV_HEADS, grp * rows, LANES))


def _lane_row(v, width=LANES):
    return jnp.zeros((1, width), F32).at[0, :v.shape[0]].set(v)


def kernel(x_prompt, x_sample, cache_na_k, cache_na_v, state_ssd, cache_swa_k, cache_swa_v, c, c_ctx, norm_mix, norm_ffn, w_mod, b_mod, w_in_even, na_rpb, ssd_conv_w, ssd_conv_b, ssd_a_log, ssd_dt_bias, ssd_d, ssd_norm, w_out_even, w_in_odd, swa_sink, w_out_odd, w_router, w_gate, w_up, w_down, final_norm):
    d = D_MODEL
    x = jnp.concatenate([x_prompt.reshape(N_CTX, d), x_sample.reshape(N_LAT, d)], axis=0)
    cond8 = jnp.zeros((8, d), F32).at[0].set(c_ctx).at[1:1 + DEC_BATCH].set(c)
    mod = _adaln(cond8, w_mod, b_mod)
    cos_t, sin_t = _rope_tables()
    new_na_k, new_na_v, new_ssd, new_swa_k, new_swa_v = [], [], [], [], []
    for l in range(DEPTH):
        j = l // 2
        mod_l = mod[l]
        g_mix = norm_mix[l].reshape(1, d)
        g_ffn = norm_ffn[l].reshape(1, d)
        if l % 2 == 0:
            w_in = jnp.pad(w_in_even[j], ((0, 0), (0, EVEN_IN_PAD - EVEN_IN))).astype(BF16)
            proj = _proj_in(x, g_mix, mod_l, w_in)
            o_ctx = _ctx_na(proj)
            ck = cache_na_k[:, j].reshape(DEC_BATCH, PAST_LEN, NA_WIDTH)
            cv = cache_na_v[:, j].reshape(DEC_BATCH, PAST_LEN, NA_WIDTH)
            o_lat = _lat_na(proj, ck, cv, _na_col_table(na_rpb[j]))
            consts = [jnp.pad(ssd_conv_w[j], ((0, 8 - SSD_CONV_W), (0, 0))),
                      ssd_conv_b[j].reshape(1, SSD_CONV_CH),
                      _lane_row(-jnp.exp(ssd_a_log[j].reshape(-1))),
                      _lane_row(ssd_dt_bias[j].reshape(-1)),
                      jnp.repeat(ssd_d[j], SSD_HEAD_DIM).reshape(1, SSD_D_INNER),
                      ssd_norm[j].reshape(1, SSD_D_INNER)]
            y_ctx, st = _ssd(proj, None, consts, seq=SEQ, nbatch=BATCH, row_blk0=0, want_state=True)
            h0t = state_ssd[:, j].transpose(0, 1, 4, 2, 3).reshape(DEC_BATCH, 2, SSD_STATE, SSD_D_INNER)
            y_lat, _ = _ssd(proj, h0t, consts, seq=DEC_SEQ, nbatch=DEC_BATCH, row_blk0=N_CTX // DEC_SEQ,
                            want_state=False)
            x, h, aff = _out_proj(o_ctx, o_lat, 0, y_ctx, y_lat, 0, w_out_even[j].astype(BF16), x, mod_l,
                                  g_ffn, w_router[l].T)
            new_na_k.append(proj[:N_CTX, NA_WIDTH:2 * NA_WIDTH].reshape(BATCH, SEQ, NA_WIDTH))
            new_na_v.append(proj[:N_CTX, 2 * NA_WIDTH:3 * NA_WIDTH].reshape(BATCH, SEQ, NA_WIDTH))
            new_ssd.append(st)
        else:
            proj = _proj_in(x, g_mix, mod_l, w_in_odd[j].astype(BF16))
            o_ctx = _ctx_swa(proj, _sink_rows(swa_sink[j], SEQ))
            ck = cache_swa_k[:, j].reshape(DEC_BATCH, PAST_LEN, SWA_KV)
            cv = cache_swa_v[:, j].reshape(DEC_BATCH, PAST_LEN, SWA_KV)
            o_lat = _lat_swa(proj, ck, cv, cos_t, sin_t, _sink_rows(swa_sink[j], SWA_BLOCK))
            x, h, aff = _out_proj(o_ctx, o_lat, 0, o_ctx, o_lat, 1, w_out_odd[j].astype(BF16), x, mod_l,
                                  g_ffn, w_router[l].T)
            new_swa_k.append(proj[:N_CTX, SWA_Q:SWA_Q + SWA_KV].reshape(BATCH, SEQ, SWA_KV))
            new_swa_v.append(proj[:N_CTX, SWA_Q + SWA_KV:].reshape(BATCH, SEQ, SWA_KV))
        x = _moe_layer(l, x, h, aff, mod_l, w_gate, w_up, w_down)
    y_ctx, y_lat = _final_norm(x, final_norm.reshape(1, d))
    n_even, n_odd = len(new_na_k), len(new_swa_k)
    na_shape = (BATCH, n_even, SEQ, NA_HEADS, HEAD_DIM)
    swa_shape = (BATCH, n_odd, SEQ, SWA_KV_HEADS, HEAD_DIM)
    ssd_t = jnp.stack(new_ssd, axis=1).reshape(BATCH, n_even, 2, SSD_STATE, SSD_HEADS, SSD_HEAD_DIM)
    return (y_ctx.reshape(BATCH, SEQ, d), y_lat.reshape(DEC_BATCH, DEC_SEQ, d),
            jnp.stack(new_na_k, axis=1).reshape(na_shape), jnp.stack(new_na_v, axis=1).reshape(na_shape),
            ssd_t.transpose(0, 1, 2, 4, 5, 3),
            jnp.stack(new_swa_k, axis=1).reshape(swa_shape), jnp.stack(new_swa_v, axis=1).reshape(swa_shape))
```

```python
import functools
import math

import jax
import jax.numpy as jnp
import numpy as np
from jax import lax
from jax.experimental import pallas as pl
from jax.experimental.pallas import tpu as pltpu

F32 = jnp.float32
BF16 = jnp.bfloat16
I32 = jnp.int32
HIGHEST = lax.Precision.HIGHEST

D_MODEL = 1024
BATCH = 16
SEQ = 256
DEPTH = 4
DEC_BATCH = 4
DEC_SEQ = 2048
PAST_LEN = 512
GRID_W = 64
HEAD_DIM = 64
NA_HEADS = 8
NA_KR = 8
NA_KC = 16
SSD_HEADS = 8
SSD_HEAD_DIM = 64
SSD_D_INNER = SSD_HEADS * SSD_HEAD_DIM
SSD_STATE = 128
SSD_GROUPS = 2
SSD_CHUNK = 128
SSD_CONV_W = 5
SWA_HEADS = 16
SWA_KV_HEADS = 4
SWA_WINDOW = 128
SWA_BLOCK = 128
ROPE_BASE = 10000.0
N_EXPERTS = 16
EXPERT_FF = 2048
EC_CAPACITY = 2
NORM_EPS = 1e-6

N_CTX = BATCH * SEQ
N_LAT = DEC_BATCH * DEC_SEQ
N_ALL = N_CTX + N_LAT
NA_WIDTH = NA_HEADS * HEAD_DIM
SSD_BC = SSD_GROUPS * SSD_STATE
SSD_CONV_CH = SSD_D_INNER + 2 * SSD_BC
EVEN_IN = 3 * NA_WIDTH + SSD_D_INNER + SSD_CONV_CH + 2 * SSD_HEADS
EVEN_IN_PAD = 3200
SWA_Q = SWA_HEADS * HEAD_DIM
SWA_KV = SWA_KV_HEADS * HEAD_DIM
ODD_IN = SWA_Q + 2 * SWA_KV
CAP_CTX = EC_CAPACITY * N_CTX // N_EXPERTS
CAP_LAT = EC_CAPACITY * N_LAT // N_EXPERTS
CAP_ALL = CAP_CTX + CAP_LAT
LANES = 128
NEG_BIG = -1e30
ATT_SCALE = HEAD_DIM ** -0.5
VMEM_LIMIT = 56 * 1024 * 1024


def _params(sem):
    return pltpu.CompilerParams(dimension_semantics=sem, vmem_limit_bytes=VMEM_LIMIT)


def _group_of_block(m, tm):
    ctx_blocks = N_CTX // tm
    return lax.select(m < ctx_blocks, jnp.int32(0), 1 + (m - ctx_blocks) // (DEC_SEQ // tm))


def _norm_mod(x, g, shift, scale):
    ms = jnp.mean(x * x, axis=-1, keepdims=True)
    y = x * lax.rsqrt(ms + NORM_EPS) * g
    return y * (1.0 + scale) + shift


def _silu(x):
    return x * jax.nn.sigmoid(x)


def _softplus(x):
    return jnp.maximum(x, 0.0) + jnp.log1p(jnp.exp(-jnp.abs(x)))


def _dot(a, b, **kw):
    return jnp.dot(a, b, preferred_element_type=F32, **kw)


def _dot_nt(a, b, **kw):
    return lax.dot_general(a, b, (((1,), (1,)), ((), ())), preferred_element_type=F32, **kw)


def _split_bf16(x):
    hi = x.astype(BF16)
    return hi, (x - hi.astype(F32)).astype(BF16)


def _value_with_ones(v_tile, upper_half):
    lane = lax.broadcasted_iota(I32, v_tile.shape, 1)
    v = pltpu.roll(v_tile, HEAD_DIM, 1) if upper_half else v_tile
    return jnp.where(lane < HEAD_DIM, v, 1.0).astype(BF16)


def _softmax_pv(q, keys, vals, biases, extra=None):
    logits = []
    for k, b in zip(keys, biases):
        s = _dot_nt(q, k)
        logits.append(s if b is None else s + b)
    mx = extra
    for s in logits:
        for c in range(s.shape[1] // LANES):
            t = s[:, c * LANES:(c + 1) * LANES]
            mx = t if mx is None else jnp.maximum(mx, t)
    m = jnp.max(mx, axis=-1, keepdims=True)
    acc = None
    for s, v in zip(logits, vals):
        pv = _dot(jnp.exp(s - m).astype(BF16), v)
        acc = pv if acc is None else acc + pv
    den = pltpu.roll(acc, HEAD_DIM, 1)
    if extra is not None:
        den = den + jnp.exp(extra - m)
    return (acc / den)[:, 0:HEAD_DIM]


def _adaln_kernel(c_ref, w_ref, b_ref, o_ref):
    s_hi, s_lo = _split_bf16(_silu(c_ref[...]))
    w_hi, w_lo = _split_bf16(w_ref[0])
    o_ref[0] = _dot(s_hi, w_hi) + (_dot(s_lo, w_hi) + _dot(s_hi, w_lo)) + b_ref[0]


def _adaln(cond8, w_mod, b_mod):
    d = D_MODEL
    return pl.pallas_call(
        _adaln_kernel,
        grid=(DEPTH, 6),
        in_specs=[pl.BlockSpec((8, d), lambda l, n: (0, 0)),
                  pl.BlockSpec((1, d, d), lambda l, n: (l, 0, n)),
                  pl.BlockSpec((1, 1, d), lambda l, n: (l, 0, n))],
        out_specs=pl.BlockSpec((1, 8, d), lambda l, n: (l, 0, n)),
        out_shape=jax.ShapeDtypeStruct((DEPTH, 8, 6 * d), F32),
        compiler_params=_params(("arbitrary", "arbitrary")),
        name="adaln",
    )(cond8, w_mod, b_mod.reshape(DEPTH, 1, 6 * d))


def _token_row_specs(x, tm):
    if not isinstance(x, tuple):
        return [x], [pl.BlockSpec((tm, x.shape[1]), lambda m: (m, 0))]
    nctx = N_CTX // tm
    x_ctx, x_lat = x
    return ([x_ctx, x_lat],
            [pl.BlockSpec((tm, x_ctx.shape[1]), lambda m: (jnp.minimum(m, nctx - 1), 0)),
             pl.BlockSpec((tm, x_lat.shape[1]), lambda m: (jnp.maximum(m - nctx, 0), 0))])


def _token_rows(x_refs, tm):
    if len(x_refs) == 1:
        return x_refs[0][...]
    return jnp.where(pl.program_id(0) < N_CTX // tm, x_refs[0][...], x_refs[1][...])


def _proj_in_kernel(*refs, tm):
    g_ref, mod_ref, w_ref, o_ref = refs[-4:]
    grp = _group_of_block(pl.program_id(0), tm)
    shift = mod_ref[pl.ds(grp, 1), 0:D_MODEL]
    scale = mod_ref[pl.ds(grp, 1), D_MODEL:2 * D_MODEL]
    h = _norm_mod(_token_rows(refs[:-4], tm), g_ref[...], shift, scale).astype(BF16)
    o_ref[...] = _dot(h, w_ref[...])


def _proj_in(x, g, mod_l, w_bf16, tm=512):
    n = w_bf16.shape[1]
    x_arrays, x_specs = _token_row_specs(x, tm)
    return pl.pallas_call(
        functools.partial(_proj_in_kernel, tm=tm),
        grid=(N_ALL // tm,),
        in_specs=x_specs + [pl.BlockSpec((1, D_MODEL), lambda m: (0, 0)),
                            pl.BlockSpec((8, 6 * D_MODEL), lambda m: (0, 0)),
                            pl.BlockSpec((D_MODEL, n), lambda m: (0, 0))],
        out_specs=pl.BlockSpec((tm, n), lambda m: (m, 0)),
        out_shape=jax.ShapeDtypeStruct((N_ALL, n), F32),
        compiler_params=_params(("arbitrary",)),
        name="proj_in",
    )(*x_arrays, g, mod_l, w_bf16)


def _router_affinities(h, wr):
    h_hi, h_lo = _split_bf16(h)
    w_hi, w_lo = _split_bf16(wr)
    logits = _dot_nt(w_hi, h_hi) + (_dot_nt(w_hi, h_lo) + _dot_nt(w_lo, h_hi))
    e = jnp.exp(logits - jnp.max(logits, axis=0, keepdims=True))
    return e / jnp.sum(e, axis=0, keepdims=True)


def _out_proj_kernel(ac_ref, al_ref, bc_ref, bl_ref, w_ref, mod_ref, g_ref, wr_ref, *refs, tm):
    x_refs, (o_ref, h_ref, aff_ref) = refs[:-3], refs[-3:]
    m = pl.program_id(0)
    grp = _group_of_block(m, tm)

    def mod_row(k):
        return mod_ref[pl.ds(grp, 1), k * D_MODEL:(k + 1) * D_MODEL]

    half = D_MODEL // 2
    is_ctx = m < N_CTX // tm
    a = jnp.where(is_ctx, ac_ref[...], al_ref[...]).astype(BF16)
    b = jnp.where(is_ctx, bc_ref[...], bl_ref[...]).astype(BF16)
    acc = _dot(a, w_ref[0:half, :]) + _dot(b, w_ref[half:, :])
    x_new = _token_rows(x_refs, tm) + mod_row(2) * acc
    o_ref[...] = x_new
    h = _norm_mod(x_new, g_ref[...], mod_row(3), mod_row(4))
    h_ref[...] = h
    aff_ref[...] = _router_affinities(h, wr_ref[...])


def _out_proj(a_ctx, a_lat, acol, b_ctx, b_lat, bcol, w_bf16, x, mod_l, g_ffn, w_router_t, tm=512):
    half = D_MODEL // 2
    nctx = N_CTX // tm

    def ctx_map(col):
        return lambda m: (jnp.minimum(m, nctx - 1), col)

    def lat_map(col):
        return lambda m: (jnp.maximum(m - nctx, 0), col)

    x_arrays, x_specs = _token_row_specs(x, tm)
    return pl.pallas_call(
        functools.partial(_out_proj_kernel, tm=tm),
        grid=(N_ALL // tm,),
        in_specs=[pl.BlockSpec((tm, half), ctx_map(acol)),
                  pl.BlockSpec((tm, half), lat_map(acol)),
                  pl.BlockSpec((tm, half), ctx_map(bcol)),
                  pl.BlockSpec((tm, half), lat_map(bcol)),
                  pl.BlockSpec((D_MODEL, D_MODEL), lambda m: (0, 0)),
                  pl.BlockSpec((8, 6 * D_MODEL), lambda m: (0, 0)),
                  pl.BlockSpec((1, D_MODEL), lambda m: (0, 0)),
                  pl.BlockSpec((N_EXPERTS, D_MODEL), lambda m: (0, 0))] + x_specs,
        out_specs=[pl.BlockSpec((tm, D_MODEL), lambda m: (m, 0)),
                   pl.BlockSpec((tm, D_MODEL), lambda m: (m, 0)),
                   pl.BlockSpec((N_EXPERTS, tm), lambda m: (0, m))],
        out_shape=[jax.ShapeDtypeStruct((N_ALL, D_MODEL), F32),
                   jax.ShapeDtypeStruct((N_ALL, D_MODEL), F32),
                   jax.ShapeDtypeStruct((N_EXPERTS, N_ALL), F32)],
        compiler_params=_params(("arbitrary",)),
        name="out_proj",
    )(a_ctx, a_lat, b_ctx, b_lat, w_bf16, mod_l, g_ffn, w_router_t, *x_arrays)


def _ctx_na_kernel(q_ref, k_ref, v_ref, o_ref):
    for h in range(NA_HEADS):
        sl = slice(h * HEAD_DIM, (h + 1) * HEAD_DIM)
        tile = slice((h // 2) * LANES, (h // 2 + 1) * LANES)
        q = (q_ref[:, sl] * ATT_SCALE).astype(BF16)
        v = _value_with_ones(v_ref[:, tile], h % 2 == 1)
        o_ref[:, sl] = _softmax_pv(q, [k_ref[:, sl].astype(BF16)], [v], [None])


def _ctx_na(proj):
    w = NA_WIDTH
    return pl.pallas_call(
        _ctx_na_kernel,
        grid=(BATCH,),
        in_specs=[pl.BlockSpec((SEQ, w), lambda b: (b, 0)),
                  pl.BlockSpec((SEQ, w), lambda b: (b, 1)),
                  pl.BlockSpec((SEQ, w), lambda b: (b, 2))],
        out_specs=pl.BlockSpec((SEQ, w), lambda b: (b, 0)),
        out_shape=jax.ShapeDtypeStruct((N_CTX, w), F32),
        compiler_params=_params(("arbitrary",)),
        name="ctx_na",
    )(proj, proj, proj)


NA_QROWS = 4
NA_KROWS = 12
NA_ROWS = DEC_SEQ // GRID_W
NA_GROUPS = NA_ROWS // NA_QROWS


def _na_key_start(g):
    return jnp.clip(NA_QROWS * g - NA_KR // 2, 0, NA_ROWS - NA_KROWS)


def _na_row_geometry(g):
    start = int(np.clip(NA_QROWS * g - NA_KR // 2, 0, NA_ROWS - NA_KROWS))
    rows = [NA_QROWS * g + qr for qr in range(NA_QROWS)]
    return start, [(r, int(np.clip(r - NA_KR // 2, 0, NA_ROWS - NA_KR))) for r in rows]


def _na_fill_bias(ct_ref, bias_scr, g):
    start, rows = _na_row_geometry(g)
    masked = jnp.full((GRID_W, GRID_W), NEG_BIG, F32)
    for i in range(NA_HEADS):
        for qr, (r, rs) in enumerate(rows):
            for kr in range(NA_KROWS):
                keyrow = start + kr
                inside = rs <= keyrow < rs + NA_KR
                tile = ct_ref[i, keyrow - r + NA_KR - 1] if inside else masked
                bias_scr[i, qr * GRID_W:(qr + 1) * GRID_W, kr * GRID_W:(kr + 1) * GRID_W] = tile


def _lat_na_kernel(q_ref, k_ref, v_ref, ck_ref, cv_ref, ct_ref, o_ref, bias_scr, k_scr, v_scr, ck_scr, cv_scr):
    g = pl.program_id(1)
    for g_build in (0, 1, NA_GROUPS - 1):
        @pl.when(g == g_build)
        def _(g_build=g_build):
            _na_fill_bias(ct_ref, bias_scr, g_build)

    @pl.when(g == 0)
    def _():
        for i in range(NA_HEADS):
            sl = slice(i * HEAD_DIM, (i + 1) * HEAD_DIM)
            tile = slice((i // 2) * LANES, (i // 2 + 1) * LANES)
            k_scr[i] = k_ref[:, sl].astype(BF16)
            v_scr[i] = _value_with_ones(v_ref[:, tile], i % 2 == 1)
            ck_scr[i] = ck_ref[:, sl].astype(BF16)
            cv_scr[i] = _value_with_ones(cv_ref[:, tile], i % 2 == 1)

    start = pl.multiple_of(_na_key_start(g) * GRID_W, GRID_W)
    nk = NA_KROWS * GRID_W
    for i in range(NA_HEADS):
        sl = slice(i * HEAD_DIM, (i + 1) * HEAD_DIM)
        q = (q_ref[:, sl] * ATT_SCALE).astype(BF16)
        o_ref[:, sl] = _softmax_pv(q, [k_scr[i, pl.ds(start, nk), :], ck_scr[i]],
                                   [v_scr[i, pl.ds(start, nk), :], cv_scr[i]], [bias_scr[i], None])


def _na_col_table(rpb):
    w = np.arange(GRID_W)[:, None]
    cc = np.arange(GRID_W)[None, :]
    cs = np.clip(w - NA_KC // 2, 0, GRID_W - NA_KC)
    valid = (cc >= cs) & (cc < cs + NA_KC)
    dc = cc - w + NA_KC - 1
    onehot = ((dc[..., None] == np.arange(2 * NA_KC - 1)) & valid[..., None]).astype(np.float32)
    ct = jnp.einsum('hrd,wcd->hrwc', rpb, jnp.asarray(onehot), precision=HIGHEST)
    return jnp.where(jnp.asarray(valid)[None, None], ct, NEG_BIG)


def _lat_na(proj, ck, cv, col_tab):
    nq = NA_QROWS * GRID_W
    nk = NA_KROWS * GRID_W
    lat_q0 = N_CTX // nq
    lat_b0 = N_CTX // DEC_SEQ
    w = NA_WIDTH
    return pl.pallas_call(
        _lat_na_kernel,
        grid=(DEC_BATCH, NA_GROUPS),
        in_specs=[pl.BlockSpec((nq, w), lambda b, g: (lat_q0 + b * NA_GROUPS + g, 0)),
                  pl.BlockSpec((DEC_SEQ, w), lambda b, g: (lat_b0 + b, 1)),
                  pl.BlockSpec((DEC_SEQ, w), lambda b, g: (lat_b0 + b, 2)),
                  pl.BlockSpec((None, PAST_LEN, w), lambda b, g: (b, 0, 0)),
                  pl.BlockSpec((None, PAST_LEN, w), lambda b, g: (b, 0, 0)),
                  pl.BlockSpec((NA_HEADS, 2 * NA_KR - 1, GRID_W, GRID_W), lambda b, g: (0, 0, 0, 0))],
        out_specs=pl.BlockSpec((nq, w), lambda b, g: (b * NA_GROUPS + g, 0)),
        out_shape=jax.ShapeDtypeStruct((N_LAT, w), F32),
        scratch_shapes=[pltpu.VMEM((NA_HEADS, nq, nk), F32),
                        pltpu.VMEM((NA_HEADS, DEC_SEQ, HEAD_DIM), BF16),
                        pltpu.VMEM((NA_HEADS, DEC_SEQ, LANES), BF16),
                        pltpu.VMEM((NA_HEADS, PAST_LEN, HEAD_DIM), BF16),
                        pltpu.VMEM((NA_HEADS, PAST_LEN, LANES), BF16)],
        compiler_params=_params(("arbitrary", "arbitrary")),
        name="lat_na",
    )(proj, proj, proj, ck, cv, col_tab)


def _ssd_kernel(*refs, seq, has_h0, want_state):
    it = iter(refs)
    z_ref, xbc_ref, dt_ref = next(it), next(it), next(it)
    h0_ref = next(it) if has_h0 else None
    cw_ref, cb_ref, a_ref, dtb_ref, dsk_ref, ng_ref = (next(it) for _ in range(6))
    y_ref = next(it)
    st_ref = next(it) if want_state else None
    xc_scr, y_scr, ht_scr = next(it), next(it), next(it)

    nc = seq // SSD_CHUNK
    ch = SSD_CHUNK
    row = lax.broadcasted_iota(I32, (ch, ch), 0)
    col = lax.broadcasted_iota(I32, (ch, ch), 1)
    erow = lax.broadcasted_iota(I32, (LANES, SSD_D_INNER), 0)
    ecol = lax.broadcasted_iota(I32, (LANES, SSD_D_INNER), 1) // SSD_HEAD_DIM
    srow = lax.broadcasted_iota(I32, (LANES, SSD_HEADS * LANES), 0)
    scol = lax.broadcasted_iota(I32, (LANES, SSD_HEADS * LANES), 1) // LANES
    lane128 = lax.broadcasted_iota(I32, (1, LANES), 1)
    cbias = cb_ref[...]

    def conv_chunk(c, carry):
        base = pl.multiple_of(c * ch, ch)
        cur = xbc_ref[pl.ds(base, ch), :]
        pbase = pl.multiple_of(jnp.maximum(base - 8, 0), 8)
        nbase = pl.multiple_of(jnp.minimum(base + ch, seq - 8), 8)
        prev = jnp.where(c > 0, xbc_ref[pl.ds(pbase, 8), :], 0.0)
        nxt = jnp.where(c < nc - 1, xbc_ref[pl.ds(nbase, 8), :], 0.0)
        win = jnp.concatenate([prev, cur, nxt], axis=0)
        acc = jnp.broadcast_to(cbias, (ch, SSD_CONV_CH))
        pad = SSD_CONV_W // 2
        for k in range(SSD_CONV_W):
            off = 8 - pad + k
            acc = acc + win[off:off + ch, :] * cw_ref[k:k + 1, :]
        xc_scr[pl.ds(base, ch), :] = _silu(acc)
        return carry

    lax.fori_loop(0, nc, conv_chunk, 0)

    def run_direction(d):
        lane0 = d * SSD_HEADS
        expand = jnp.where(erow == ecol + lane0, 1.0, 0.0).astype(BF16)
        spread = jnp.where(srow == scol + lane0, 1.0, 0.0).astype(BF16)
        lmask = (col <= row) if d == 0 else (col >= row)
        tri = jnp.where(lmask, 1.0, 0.0).astype(BF16)

        def dot_split(lhs01, x, lhs_first):
            hi = x.astype(BF16)
            lo = (x - hi.astype(F32)).astype(BF16)
            if lhs_first:
                return _dot(lhs01, hi) + _dot(lhs01, lo)
            return _dot(hi, lhs01) + _dot(lo, lhs01)
        if has_h0:
            ht_scr[...] = h0_ref[d]
        else:
            ht_scr[...] = jnp.zeros((SSD_STATE, SSD_D_INNER), F32)

        def chunk(step, carry):
            c = step if d == 0 else nc - 1 - step
            base = pl.multiple_of(c * ch, ch)
            xs = xc_scr[pl.ds(base, ch), 0:SSD_D_INNER]
            dt = _softplus(dt_ref[pl.ds(base, ch), :] + dtb_ref[...])
            a = dt * a_ref[...]
            cum = dot_split(tri, a, True)
            tot = cum[ch - 1:ch, :] if d == 0 else cum[0:1, :]
            cum_t = cum.T
            dt_x = dot_split(expand, dt, False)
            cum_x = dot_split(expand, cum, False)
            cum_w = dot_split(spread, cum, False)
            tot_x = dot_split(expand, jnp.broadcast_to(tot, (8, LANES)), False)[0:1, :]
            xt = xs * dt_x
            xd = (xt * jnp.exp(tot_x - cum_x)).astype(BF16)
            ecum_x = jnp.exp(cum_x)
            y_parts = []
            for grp in range(SSD_GROUPS):
                bsl = slice(SSD_D_INNER + grp * SSD_STATE, SSD_D_INNER + (grp + 1) * SSD_STATE)
                csl = slice(SSD_D_INNER + SSD_BC + grp * SSD_STATE, SSD_D_INNER + SSD_BC + (grp + 1) * SSD_STATE)
                b_f = xc_scr[pl.ds(base, ch), bsl]
                b_g = b_f.astype(BF16)
                c_g = xc_scr[pl.ds(base, ch), csl].astype(BF16)
                cb = _dot_nt(c_g, b_g)
                hsl = slice(grp * 4 * SSD_HEAD_DIM, (grp + 1) * 4 * SSD_HEAD_DIM)
                ht_g = ht_scr[:, hsl]
                y_off = _dot(c_g, ht_g.astype(BF16))
                for pair in range(2):
                    h0 = grp * 4 + 2 * pair
                    decayed = []
                    for head in (h0, h0 + 1):
                        cum_col = cum_w[:, head * LANES:(head + 1) * LANES]
                        cum_row = cum_t[lane0 + head:lane0 + head + 1, :]
                        ldec = jnp.exp(jnp.where(lmask, cum_col - cum_row, NEG_BIG))
                        decayed.append((cb * ldec).astype(BF16))
                    tile = slice((h0 // 2) * LANES, (h0 // 2 + 1) * LANES)
                    x_pair = xt[:, tile]
                    x_diag = jnp.concatenate([jnp.where(lane128 < SSD_HEAD_DIM, x_pair, 0.0),
                                              jnp.where(lane128 >= SSD_HEAD_DIM, x_pair, 0.0)], axis=0)
                    y_d = _dot(jnp.concatenate(decayed, axis=1), x_diag.astype(BF16))
                    y_parts.append(y_d + y_off[:, pair * LANES:(pair + 1) * LANES] * ecum_x[:, tile])
                ht_scr[:, hsl] = ht_g * jnp.exp(tot_x[:, hsl]) + _dot(b_f.T.astype(BF16), xd[:, hsl])
            y = jnp.concatenate(y_parts, axis=1)
            if d == 0:
                y_scr[pl.ds(base, ch), :] = y
            else:
                y = y + y_scr[pl.ds(base, ch), :] + dsk_ref[...] * xs
                u = y * _silu(z_ref[pl.ds(base, ch), :])
                ms = jnp.mean(u * u, axis=-1, keepdims=True)
                y_ref[pl.ds(base, ch), :] = u * lax.rsqrt(ms + NORM_EPS) * ng_ref[...]
            return carry

        lax.fori_loop(0, nc, chunk, 0, unroll=2)
        if want_state:
            st_ref[d] = ht_scr[...]

    run_direction(0)
    run_direction(1)


def _ssd(proj, h0t, consts, *, seq, nbatch, row_blk0, want_state):
    has_h0 = h0t is not None
    in_specs = [pl.BlockSpec((seq, SSD_D_INNER), lambda b: (row_blk0 + b, 3)),
                pl.BlockSpec((seq, SSD_CONV_CH), lambda b: (row_blk0 + b, 2)),
                pl.BlockSpec((seq, LANES), lambda b: (row_blk0 + b, 24))]
    args = [proj, proj, proj]
    if has_h0:
        in_specs.append(pl.BlockSpec((None, 2, SSD_STATE, SSD_D_INNER), lambda b: (b, 0, 0, 0)))
        args.append(h0t)
    for cst in consts:
        in_specs.append(pl.BlockSpec(cst.shape, lambda b: (0, 0)))
        args.append(cst)
    out_specs = [pl.BlockSpec((seq, SSD_D_INNER), lambda b: (b, 0))]
    out_shape = [jax.ShapeDtypeStruct((nbatch * seq, SSD_D_INNER), F32)]
    if want_state:
        out_specs.append(pl.BlockSpec((None, 2, SSD_STATE, SSD_D_INNER), lambda b: (b, 0, 0, 0)))
        out_shape.append(jax.ShapeDtypeStruct((nbatch, 2, SSD_STATE, SSD_D_INNER), F32))
    res = pl.pallas_call(
        functools.partial(_ssd_kernel, seq=seq, has_h0=has_h0, want_state=want_state),
        grid=(nbatch,),
        in_specs=in_specs,
        out_specs=out_specs,
        out_shape=out_shape,
        scratch_shapes=[pltpu.VMEM((seq, SSD_CONV_CH), F32),
                        pltpu.VMEM((seq, SSD_D_INNER), F32),
                        pltpu.VMEM((SSD_STATE, SSD_D_INNER), F32)],
        compiler_params=_params(("arbitrary",)),
        name="ssd_seq%d" % seq,
    )(*args)
    return res if want_state else (res[0], None)


def _ctx_swa_kernel(q_ref, k_ref, v_ref, sink_ref, o_ref):
    grp = SWA_HEADS // SWA_KV_HEADS
    for kv in range(SWA_KV_HEADS):
        ksl = slice(kv * HEAD_DIM, (kv + 1) * HEAD_DIM)
        tile = slice((kv // 2) * LANES, (kv // 2 + 1) * LANES)
        k = k_ref[:, ksl].astype(BF16)
        v = _value_with_ones(v_ref[:, tile], kv % 2 == 1)
        q4 = jnp.concatenate(
            [q_ref[:, (kv * grp + j) * HEAD_DIM:(kv * grp + j + 1) * HEAD_DIM] for j in range(grp)],
            axis=0)
        o = _softmax_pv((q4 * ATT_SCALE).astype(BF16), [k], [v], [None], extra=sink_ref[kv])
        for j in range(grp):
            o_ref[:, (kv * grp + j) * HEAD_DIM:(kv * grp + j + 1) * HEAD_DIM] = o[j * SEQ:(j + 1) * SEQ]


def _ctx_swa(proj, sink_col):
    return pl.pallas_call(
        _ctx_swa_kernel,
        grid=(BATCH,),
        in_specs=[pl.BlockSpec((SEQ, SWA_Q), lambda b: (b, 0)),
                  pl.BlockSpec((SEQ, SWA_KV), lambda b: (b, 4)),
                  pl.BlockSpec((SEQ, SWA_KV), lambda b: (b, 5)),
                  pl.BlockSpec(sink_col.shape, lambda b: (0, 0, 0))],
        out_specs=pl.BlockSpec((SEQ, SWA_Q), lambda b: (b, 0)),
        out_shape=jax.ShapeDtypeStruct((N_CTX, SWA_Q), F32),
        compiler_params=_params(("arbitrary",)),
        name="ctx_swa",
    )(proj, proj, proj, sink_col)


SWA_NLOC = 3 * SWA_BLOCK


def _rope(x, cos, sin_signed, first):
    n = x.shape[-1]
    partner = jnp.where(first, pltpu.roll(x, n - 16, 1), pltpu.roll(x, 16, 1))
    return x * cos + partner * sin_signed


def _lat_swa_kernel(q_ref, k_ref, v_ref, ck_ref, cv_ref, cosk_ref, sink_k_ref, cosq_ref, sinq_ref, sink_ref,
                    o_ref, kr_scr, v_scr, ck_scr, cv_scr):
    i = pl.program_id(1)
    grp = SWA_HEADS // SWA_KV_HEADS
    lane = lax.broadcasted_iota(I32, (1, SWA_KV), 1)
    first = (lane % 32) < 16

    @pl.when(i == 0)
    def _():
        kr = _rope(k_ref[...], cosk_ref[...], sink_k_ref[...], first)
        for kv in range(SWA_KV_HEADS):
            ksl = slice(kv * HEAD_DIM, (kv + 1) * HEAD_DIM)
            tile = slice((kv // 2) * LANES, (kv // 2 + 1) * LANES)
            kr_scr[kv] = kr[:, ksl].astype(BF16)
            v_scr[kv] = _value_with_ones(v_ref[:, tile], kv % 2 == 1)
            ck_scr[kv] = ck_ref[:, ksl].astype(BF16)
            cv_scr[kv] = _value_with_ones(cv_ref[:, tile], kv % 2 == 1)

    kstart = pl.multiple_of(jnp.clip((i - 1) * SWA_BLOCK, 0, DEC_SEQ - SWA_NLOC), SWA_BLOCK)
    qpos = i * SWA_BLOCK + lax.broadcasted_iota(I32, (SWA_BLOCK, SWA_NLOC), 0)
    kpos = kstart + lax.broadcasted_iota(I32, (SWA_BLOCK, SWA_NLOC), 1)
    mask1 = jnp.where(jnp.abs(qpos - kpos) <= SWA_WINDOW, 0.0, NEG_BIG)
    mask = jnp.concatenate([mask1] * grp, axis=0)
    cosq = cosq_ref[...]
    sinq = sinq_ref[...]
    for kv in range(SWA_KV_HEADS):
        qr = _rope(q_ref[:, kv * SWA_KV:(kv + 1) * SWA_KV], cosq, sinq, first) * ATT_SCALE
        q4 = jnp.concatenate([qr[:, j * HEAD_DIM:(j + 1) * HEAD_DIM] for j in range(grp)], axis=0).astype(BF16)
        o = _softmax_pv(q4, [kr_scr[kv, pl.ds(kstart, SWA_NLOC), :], ck_scr[kv]],
                        [v_scr[kv, pl.ds(kstart, SWA_NLOC), :], cv_scr[kv]], [mask, None], extra=sink_ref[kv])
        for j in range(grp):
            o_ref[:, (kv * grp + j) * HEAD_DIM:(kv * grp + j + 1) * HEAD_DIM] = o[j * SWA_BLOCK:(j + 1) * SWA_BLOCK]


def _rope_tables():
    half = HEAD_DIM // 2
    quarter = half // 2
    pos = jnp.arange(DEC_SEQ)
    inv_freq = 1.0 / (ROPE_BASE ** (jnp.arange(quarter, dtype=F32) * 2.0 / half))
    d = np.arange(HEAD_DIM)
    use_col = jnp.asarray(d >= half)
    p = jnp.where(use_col[None, :], (pos % GRID_W)[:, None], (pos // GRID_W)[:, None]).astype(F32)
    ang = p * inv_freq[d % quarter][None, :]
    sign = jnp.asarray(np.where((d % half) < quarter, -1.0, 1.0), F32)
    cos = jnp.cos(ang)
    sin_signed = jnp.sin(ang) * sign[None, :]
    reps = SWA_KV // HEAD_DIM
    return jnp.tile(cos, (1, reps)), jnp.tile(sin_signed, (1, reps))


def _lat_swa(proj, ck, cv, cos_t, sin_t, sink_col):
    nb = DEC_SEQ // SWA_BLOCK
    q0 = N_CTX // SWA_BLOCK
    b0 = N_CTX // DEC_SEQ
    return pl.pallas_call(
        _lat_swa_kernel,
        grid=(DEC_BATCH, nb),
        in_specs=[pl.BlockSpec((SWA_BLOCK, SWA_Q), lambda b, i: (q0 + b * nb + i, 0)),
                  pl.BlockSpec((DEC_SEQ, SWA_KV), lambda b, i: (b0 + b, 4)),
                  pl.BlockSpec((DEC_SEQ, SWA_KV), lambda b, i: (b0 + b, 5)),
                  pl.BlockSpec((None, PAST_LEN, SWA_KV), lambda b, i: (b, 0, 0)),
                  pl.BlockSpec((None, PAST_LEN, SWA_KV), lambda b, i: (b, 0, 0)),
                  pl.BlockSpec((DEC_SEQ, SWA_KV), lambda b, i: (0, 0)),
                  pl.BlockSpec((DEC_SEQ, SWA_KV), lambda b, i: (0, 0)),
                  pl.BlockSpec((SWA_BLOCK, SWA_KV), lambda b, i: (i, 0)),
                  pl.BlockSpec((SWA_BLOCK, SWA_KV), lambda b, i: (i, 0)),
                  pl.BlockSpec(sink_col.shape, lambda b, i: (0, 0, 0))],
        out_specs=pl.BlockSpec((SWA_BLOCK, SWA_Q), lambda b, i: (b * nb + i, 0)),
        out_shape=jax.ShapeDtypeStruct((N_LAT, SWA_Q), F32),
        scratch_shapes=[pltpu.VMEM((SWA_KV_HEADS, DEC_SEQ, HEAD_DIM), BF16),
                        pltpu.VMEM((SWA_KV_HEADS, DEC_SEQ, LANES), BF16),
                        pltpu.VMEM((SWA_KV_HEADS, PAST_LEN, HEAD_DIM), BF16),
                        pltpu.VMEM((SWA_KV_HEADS, PAST_LEN, LANES), BF16)],
        compiler_params=_params(("arbitrary", "arbitrary")),
        name="lat_swa",
    )(proj, proj, proj, ck, cv, cos_t, sin_t, cos_t, sin_t, sink_col)


ROUTE_NB = N_ALL // LANES
ROUTE_SETS = ((0, N_CTX // LANES, CAP_CTX, 0), (N_CTX // LANES, N_LAT // LANES, CAP_LAT, CAP_CTX))


def _route_kernel(aff_ref, idx_ref, gate_ref, st_ref, pos_ref, t_scr, cend_scr):
    ne = N_EXPERTS
    r = lax.broadcasted_iota(I32, (LANES, LANES), 0)
    c = lax.broadcasted_iota(I32, (LANES, LANES), 1)
    upper = jnp.where(r <= c, 1.0, 0.0).astype(BF16)
    lower_incl = jnp.where(c <= r, 1.0, 0.0).astype(BF16)
    lower_strict = jnp.where(c < r, 1.0, 0.0).astype(BF16)
    lane = lax.broadcasted_iota(I32, (8, LANES), 1)

    def cumsum_tokens(x):
        rowc = _dot(x.astype(BF16), upper)
        tot = jnp.broadcast_to(rowc[:, LANES - 1:LANES], (LANES, LANES))
        return rowc + _dot(lower_strict, tot.astype(BF16)), tot

    for set_id, (b0, nb, cap, slot0) in enumerate(ROUTE_SETS):
        capf = float(cap)
        npad = LANES - nb

        def count_ge(e, cand, b0=b0, nb=nb):
            hit = jnp.where(aff_ref[e, b0:b0 + nb, :] >= cand, 1.0, 0.0)
            return jnp.sum(jnp.sum(hit, axis=1, keepdims=True), axis=0, keepdims=True)

        def bit_step(i, ts, count_ge=count_ge, capf=capf):
            bit = jnp.left_shift(jnp.int32(1), 30 - i)
            out = []
            for e in range(ne):
                cand = ts[e] | bit
                keep = count_ge(e, lax.bitcast_convert_type(cand, F32)) >= capf
                out.append(jnp.where(keep, cand, ts[e]))
            return tuple(out)

        ts = lax.fori_loop(0, 31, bit_step, tuple(jnp.zeros((1, 1), I32) for _ in range(ne)))
        for e in range(ne):
            t_scr[e] = jnp.broadcast_to(lax.bitcast_convert_type(ts[e], F32), (8, LANES))

        slot = lax.broadcasted_iota(I32, (LANES, cap), 1).astype(F32)
        sub = lax.broadcasted_iota(I32, (LANES, cap), 0).astype(F32)

        def per_expert(e, carry, set_id=set_id, b0=b0, nb=nb, cap=cap, slot0=slot0, capf=capf, npad=npad,
                       slot=slot, sub=sub):
            a = aff_ref[e, b0:b0 + nb, :]
            if npad:
                a = jnp.concatenate([a, jnp.full((npad, LANES), -1.0, F32)], axis=0)
            thr = t_scr[e][0:1, 0:1]
            gt = a > thr
            eq = a == thr
            gtf = jnp.where(gt, 1.0, 0.0)
            eqf = jnp.where(eq, 1.0, 0.0)
            need = capf - jnp.sum(jnp.sum(gtf, axis=1, keepdims=True), axis=0, keepdims=True)
            eq_incl, _ = cumsum_tokens(eqf)
            sel = gt | (eq & (eq_incl - eqf < need))
            self = jnp.where(sel, 1.0, 0.0)
            incl, tot = cumsum_tokens(self)
            cend = _dot(lower_incl, tot.astype(BF16))
            blk = jnp.sum(jnp.where(cend[:, 0:1] <= slot, 1.0, 0.0), axis=0, keepdims=True)
            onehot = jnp.where(sub == blk, 1.0, 0.0)
            hi = jnp.floor(incl * (1.0 / LANES))
            lo = incl - hi * LANES
            a_pos = jnp.maximum(a, 0.0)
            a1 = a_pos.astype(BF16)
            r1 = a_pos - a1.astype(F32)
            a2 = r1.astype(BF16)
            a3 = (r1 - a2.astype(F32)).astype(BF16)
            stacked = jnp.concatenate([hi.T.astype(BF16), lo.T.astype(BF16), a1.T, a2.T, a3.T], axis=0)
            picked = _dot(stacked, onehot.astype(BF16))
            inc_s = picked[0:LANES] * LANES + picked[LANES:2 * LANES]
            within = jnp.sum(jnp.where(inc_s <= slot, 1.0, 0.0), axis=0, keepdims=True)
            aff_s = picked[2 * LANES:3 * LANES] + picked[3 * LANES:4 * LANES] + picked[4 * LANES:5 * LANES]
            gate = jnp.sum(jnp.where(sub == within, aff_s, 0.0), axis=0, keepdims=True)
            idx_ref[e, :, slot0:slot0 + cap] = ((blk + float(b0)) * LANES + within).astype(I32)
            gate_ref[e, :, slot0:slot0 + cap] = gate
            pos_ref[e, b0:b0 + nb, :] = jnp.where(sel, incl - 1.0 + float(slot0), -1.0)[0:nb, :]
            ends = pltpu.roll(cend.T[0:8, :] + float(slot0), b0 + 1, 1)
            if set_id == 0:
                cend_scr[e] = jnp.where(lane == 0, 0.0, ends)
            else:
                st_ref[e] = jnp.where(lane <= b0, cend_scr[e], ends).astype(I32)
            return carry

        lax.fori_loop(0, ne, per_expert, 0, unroll=2)


def _route(aff3):
    ne = N_EXPERTS
    return pl.pallas_call(
        _route_kernel,
        grid=(1,),
        in_specs=[pl.BlockSpec((ne, ROUTE_NB, LANES), lambda i: (0, 0, 0))],
        out_specs=[pl.BlockSpec((ne, 1, CAP_ALL), lambda i: (0, 0, 0)),
                   pl.BlockSpec((ne, 1, CAP_ALL), lambda i: (0, 0, 0)),
                   pl.BlockSpec((ne, 8, LANES), lambda i: (0, 0, 0)),
                   pl.BlockSpec((ne, ROUTE_NB, LANES), lambda i: (0, 0, 0))],
        out_shape=[jax.ShapeDtypeStruct((ne, 1, CAP_ALL), I32),
                   jax.ShapeDtypeStruct((ne, 1, CAP_ALL), F32),
                   jax.ShapeDtypeStruct((ne, 8, LANES), I32),
                   jax.ShapeDtypeStruct((ne, ROUTE_NB, LANES), F32)],
        scratch_shapes=[pltpu.VMEM((ne, 8, LANES), F32),
                        pltpu.VMEM((ne, 8, LANES), F32)],
        compiler_params=_params(("arbitrary",)),
        name="route",
    )(aff3)


FF_TILE = 512
COMB_PIECE = 32
CAP_PAD = CAP_ALL + COMB_PIECE


def _moe_ffn_kernel(idx_ref, h_hbm, gate_ref, wg_ref, wu_ref, wd_ref, o_ref, land, xe, acc, g_scr, u_scr, sem):
    e = pl.program_id(0)
    f = pl.program_id(1)
    nf = EXPERT_FF // FF_TILE
    rps = CAP_ALL // nf

    def row_copy(expert, part, s):
        tok = idx_ref[expert * CAP_ALL + part * rps + s]
        return pltpu.make_async_copy(h_hbm.at[pl.ds(tok, 1), :], land.at[part, pl.ds(s, 1), :], sem)

    def wait_rows():
        for part in range(nf):
            pltpu.make_async_copy(h_hbm.at[pl.ds(0, rps), :], land.at[part], sem).wait()

    @pl.when(f == 0)
    def _():
        @pl.when(e == 0)
        def _():
            for part in range(nf):
                def issue(s, carry, part=part):
                    row_copy(0, part, s).start()
                    return carry

                lax.fori_loop(0, rps, issue, 0)

        wait_rows()
        for part in range(nf):
            xe[part * rps:(part + 1) * rps, :] = land[part].astype(BF16)
        acc[...] = jnp.zeros_like(acc)

    nxt = (e + 1) % N_EXPERTS
    third = rps // 3

    def request_rows(lo, hi):
        for s in range(lo, hi):
            row_copy(nxt, f, s).start()

    @pl.when(f >= 0)
    def _():
        request_rows(0, third)
        g_scr[...] = _dot(xe[...], wg_ref[...].astype(BF16))

    @pl.when(e >= 0)
    def _():
        request_rows(third, 2 * third)
        u_scr[...] = _dot(xe[...], wu_ref[...].astype(BF16))

    request_rows(2 * third, rps)
    hid = (_silu(g_scr[...]) * u_scr[...]).astype(BF16)
    acc[...] += _dot(hid, wd_ref[...].astype(BF16))

    @pl.when(f == nf - 1)
    def _():
        y = acc[...] * gate_ref[...]
        hi = y.astype(BF16)
        o_ref[0:CAP_ALL, 0:D_MODEL] = hi
        o_ref[0:CAP_ALL, D_MODEL:2 * D_MODEL] = (y - hi.astype(F32)).astype(BF16)
        o_ref[CAP_ALL:CAP_PAD, :] = jnp.zeros((CAP_PAD - CAP_ALL, 2 * D_MODEL), BF16)

        @pl.when(e == N_EXPERTS - 1)
        def _():
            wait_rows()


def _moe_ffn(layer, h, idx_flat, gate_col, w_gate, w_up, w_down):
    nf = EXPERT_FF // FF_TILE
    grid_spec = pltpu.PrefetchScalarGridSpec(
        num_scalar_prefetch=1,
        grid=(N_EXPERTS, nf),
        in_specs=[pl.BlockSpec(memory_space=pl.ANY),
                  pl.BlockSpec((None, CAP_ALL, 1), lambda e, f, idx: (e, 0, 0)),
                  pl.BlockSpec((None, None, D_MODEL, FF_TILE), lambda e, f, idx: (layer, e, 0, f)),
                  pl.BlockSpec((None, None, D_MODEL, FF_TILE), lambda e, f, idx: (layer, e, 0, f)),
                  pl.BlockSpec((None, None, FF_TILE, D_MODEL), lambda e, f, idx: (layer, e, f, 0))],
        out_specs=pl.BlockSpec((None, CAP_PAD, 2 * D_MODEL), lambda e, f, idx: (e, 0, 0)),
        scratch_shapes=[pltpu.VMEM((nf, CAP_ALL // nf, D_MODEL), F32),
                        pltpu.VMEM((CAP_ALL, D_MODEL), BF16),
                        pltpu.VMEM((CAP_ALL, D_MODEL), F32),
                        pltpu.VMEM((CAP_ALL, FF_TILE), F32),
                        pltpu.VMEM((CAP_ALL, FF_TILE), F32),
                        pltpu.SemaphoreType.DMA(())],
    )
    return pl.pallas_call(
        _moe_ffn_kernel,
        grid_spec=grid_spec,
        out_shape=jax.ShapeDtypeStruct((N_EXPERTS, CAP_PAD, 2 * D_MODEL), BF16),
        compiler_params=_params(("arbitrary", "arbitrary")),
        name="moe_ffn",
    )(idx_flat, h, gate_col, w_gate, w_up, w_down)


COMB_TM = 512
COMB_NB = N_ALL // COMB_TM
COMB_HALF = N_EXPERTS // 2
COMB_CHUNK = 256
COMB_ALIGN = 16
COMB_WAIT_GROUP = 32
COMB_STAGE = -(-(COMB_HALF * (COMB_TM + COMB_PIECE + COMB_ALIGN)) // COMB_CHUNK) * COMB_CHUNK


def _combine_kernel(st_ref, ye_hbm, pos_ref, x_ref, mod_ref, o_ref, stage, acc, sem):
    b = pl.program_id(0)
    nst = LANES
    per = COMB_TM // LANES

    def layout(blk, half):
        out = []
        off = jnp.int32(0)
        for j in range(COMB_HALF):
            e = half * COMB_HALF + j
            s0 = st_ref[e * nst + per * blk]
            s1 = st_ref[e * nst + per * blk + per]
            a0 = (s0 // COMB_ALIGN) * COMB_ALIGN
            npc = jnp.where(s1 > s0, (s1 - a0 + COMB_PIECE - 1) // COMB_PIECE, 0)
            out.append((e, a0, off, npc))
            off = off + npc * COMB_PIECE
        return out, off // COMB_PIECE

    def piece_copy(e, src_row, buf, dst_row, pieces=1):
        rows = pieces * COMB_PIECE
        return pltpu.make_async_copy(ye_hbm.at[e, pl.ds(src_row, rows), :],
                                     stage.at[buf, pl.ds(dst_row, rows), :], sem.at[buf])

    def issue_unit(blk, half):
        lay, _ = layout(blk, half)
        for e, a0, off, npc in lay:
            def issue(k, carry, e=e, a0=a0, off=off):
                piece_copy(e, pl.multiple_of(a0 + k * COMB_PIECE, COMB_ALIGN), half,
                           pl.multiple_of(off + k * COMB_PIECE, COMB_ALIGN)).start()
                return carry

            lax.fori_loop(0, npc, issue, 0)

    @pl.when(b == 0)
    def _():
        stage[...] = jnp.zeros_like(stage)
        issue_unit(0, 0)

    acc[...] = jnp.zeros_like(acc)
    pos = pos_ref[...]
    row_t = lax.broadcasted_iota(I32, (COMB_CHUNK, COMB_TM), 0).astype(F32)
    row_l = lax.broadcasted_iota(I32, (COMB_CHUNK, LANES), 0).astype(F32)
    lane = lax.broadcasted_iota(I32, (1, LANES), 1)
    for half in range(2):
        lay, npieces = layout(b, half)

        def wait_big(k, carry, half=half):
            piece_copy(0, 0, half, 0, COMB_WAIT_GROUP).wait()
            return carry

        lax.fori_loop(0, npieces // COMB_WAIT_GROUP, wait_big, 0)
        small = COMB_WAIT_GROUP // 2
        while small >= 1:
            @pl.when((npieces & small) != 0)
            def _(small=small, half=half):
                piece_copy(0, 0, half, 0, small).wait()

            small //= 2
        if half == 0:
            issue_unit(b, 1)
        else:
            @pl.when(b + 1 < COMB_NB)
            def _():
                issue_unit(b + 1, 0)

        srows = []
        first = jnp.full((1, LANES), 1e9, F32)
        last = jnp.zeros((1, LANES), F32)
        for j, (e, a0, off, npc) in enumerate(lay):
            p = pos[e:e + 1, :]
            srows.append(jnp.where(p >= 0.0, p + (off - a0).astype(F32), -1.0))
            first = jnp.where(lane == j, off.astype(F32), first)
            last = jnp.where(lane == j, (off + npc * COMB_PIECE).astype(F32), last)
        srow = jnp.concatenate(srows + [jnp.zeros((LANES - COMB_HALF, COMB_TM), F32)], axis=0)
        srow_hi = jnp.floor(srow * (1.0 / 64.0))
        srow_lo = (srow - 64.0 * srow_hi).astype(BF16)
        srow_hi = srow_hi.astype(BF16)

        def select_rows(ci, srow_hi=srow_hi, srow_lo=srow_lo, first=first, last=last):
            r0f = (ci * COMB_CHUNK).astype(F32)
            rid = row_l + r0f
            owner = jnp.where((rid >= first) & (rid < last), 1.0, 0.0).astype(BF16)
            want = 64.0 * _dot(owner, srow_hi) + _dot(owner, srow_lo)
            return jnp.where(want == row_t + r0f, 1.0, 0.0).T.astype(BF16)

        def chunk(ci, sel_t, half=half, select_rows=select_rows):
            sel_next = select_rows(ci + 1)
            r0 = pl.multiple_of(ci * COMB_CHUNK, COMB_CHUNK)
            acc[...] += (_dot(sel_t, stage[half, pl.ds(r0, COMB_CHUNK), 0:D_MODEL])
                         + _dot(sel_t, stage[half, pl.ds(r0, COMB_CHUNK), D_MODEL:2 * D_MODEL]))
            return sel_next

        nchunks = (npieces * COMB_PIECE + COMB_CHUNK - 1) // COMB_CHUNK
        lax.fori_loop(0, nchunks, chunk, select_rows(jnp.int32(0)))

    grp = _group_of_block(b, COMB_TM)
    gate = mod_ref[pl.ds(grp, 1), 5 * D_MODEL:6 * D_MODEL]
    o_ref[...] = x_ref[...] + gate * acc[...]


def _combine(ye, starts_flat, pos, x, mod_l):
    grid_spec = pltpu.PrefetchScalarGridSpec(
        num_scalar_prefetch=1,
        grid=(COMB_NB,),
        in_specs=[pl.BlockSpec(memory_space=pl.ANY),
                  pl.BlockSpec((N_EXPERTS, COMB_TM), lambda b, st: (0, b)),
                  pl.BlockSpec((COMB_TM, D_MODEL), lambda b, st: (b, 0)),
                  pl.BlockSpec((8, 6 * D_MODEL), lambda b, st: (0, 0))],
        out_specs=pl.BlockSpec((COMB_TM, D_MODEL), lambda b, st: (b, 0)),
        scratch_shapes=[pltpu.VMEM((2, COMB_STAGE, 2 * D_MODEL), BF16),
                        pltpu.VMEM((COMB_TM, D_MODEL), F32),
                        pltpu.SemaphoreType.DMA((2,))],
    )
    return pl.pallas_call(
        _combine_kernel,
        grid_spec=grid_spec,
        out_shape=jax.ShapeDtypeStruct((N_ALL, D_MODEL), F32),
        compiler_params=_params(("arbitrary",)),
        name="moe_combine",
    )(starts_flat, ye, pos, x, mod_l)


def _moe_layer(layer, x, h, aff, mod_l, w_gate, w_up, w_down):
    idx, gate, st, pos = _route(aff.reshape(N_EXPERTS, ROUTE_NB, LANES))
    ye = _moe_ffn(layer, h, idx.reshape(-1), gate.reshape(N_EXPERTS, CAP_ALL, 1), w_gate, w_up, w_down)
    return _combine(ye, st[:, 0, :].reshape(-1), pos.reshape(N_EXPERTS, N_ALL), x, mod_l)


def _final_norm_kernel(x_ref, g_ref, oc_ref, ol_ref, *, tm):
    x = x_ref[...]
    ms = jnp.mean(x * x, axis=-1, keepdims=True)
    y = x * lax.rsqrt(ms + NORM_EPS) * g_ref[...]
    is_ctx = pl.program_id(0) < N_CTX // tm

    @pl.when(is_ctx)
    def _():
        oc_ref[...] = y

    @pl.when(jnp.logical_not(is_ctx))
    def _():
        ol_ref[...] = y


def _final_norm(x, g, tm=1024):
    nctx = N_CTX // tm
    return pl.pallas_call(
        functools.partial(_final_norm_kernel, tm=tm),
        grid=(N_ALL // tm,),
        in_specs=[pl.BlockSpec((tm, D_MODEL), lambda m: (m, 0)),
                  pl.BlockSpec((1, D_MODEL), lambda m: (0, 0))],
        out_specs=[pl.BlockSpec((tm, D_MODEL), lambda m: (jnp.minimum(m, nctx - 1), 0)),
                   pl.BlockSpec((tm, D_MODEL), lambda m: (jnp.maximum(m - nctx, 0), 0))],
        out_shape=[jax.ShapeDtypeStruct((N_CTX, D_MODEL), F32),
                   jax.ShapeDtypeStruct((N_LAT, D_MODEL), F32)],
        compiler_params=_params(("arbitrary",)),
        name="final_norm",
    )(x, g)


def _sink_rows(sink, rows):
    grp = SWA_HEADS // SWA_KV_HEADS
    col = jnp.repeat(sink.reshape(SWA_KV_HEADS, grp), rows, axis=1).reshape(SWA_KV_HEADS, grp * rows, 1)
    return jnp.broadcast_to(col, (SWA_KV_HEADS, grp * rows, LANES))


def _lane_row(v, width=LANES):
    return jnp.zeros((1, width), F32).at[0, :v.shape[0]].set(v)


def kernel(x_prompt, x_sample, cache_na_k, cache_na_v, state_ssd, cache_swa_k, cache_swa_v, c, c_ctx, norm_mix, norm_ffn, w_mod, b_mod, w_in_even, na_rpb, ssd_conv_w, ssd_conv_b, ssd_a_log, ssd_dt_bias, ssd_d, ssd_norm, w_out_even, w_in_odd, swa_sink, w_out_odd, w_router, w_gate, w_up, w_down, final_norm):
    d = D_MODEL
    x = (x_prompt.reshape(N_CTX, d), x_sample.reshape(N_LAT, d))
    cond8 = jnp.zeros((8, d), F32).at[0].set(c_ctx).at[1:1 + DEC_BATCH].set(c)
    mod = _adaln(cond8, w_mod, b_mod)
    cos_t, sin_t = _rope_tables()
    new_na_k, new_na_v, new_ssd, new_swa_k, new_swa_v = [], [], [], [], []
    for l in range(DEPTH):
        j = l // 2
        mod_l = mod[l]
        g_mix = norm_mix[l].reshape(1, d)
        g_ffn = norm_ffn[l].reshape(1, d)
        if l % 2 == 0:
            w_in = jnp.pad(w_in_even[j], ((0, 0), (0, EVEN_IN_PAD - EVEN_IN))).astype(BF16)
            proj = _proj_in(x, g_mix, mod_l, w_in)
            o_ctx = _ctx_na(proj)
            ck = cache_na_k[:, j].reshape(DEC_BATCH, PAST_LEN, NA_WIDTH)
            cv = cache_na_v[:, j].reshape(DEC_BATCH, PAST_LEN, NA_WIDTH)
            o_lat = _lat_na(proj, ck, cv, _na_col_table(na_rpb[j]))
            consts = [jnp.pad(ssd_conv_w[j], ((0, 8 - SSD_CONV_W), (0, 0))),
                      ssd_conv_b[j].reshape(1, SSD_CONV_CH),
                      _lane_row(-jnp.exp(ssd_a_log[j].reshape(-1))),
                      _lane_row(ssd_dt_bias[j].reshape(-1)),
                      jnp.repeat(ssd_d[j], SSD_HEAD_DIM).reshape(1, SSD_D_INNER),
                      ssd_norm[j].reshape(1, SSD_D_INNER)]
            y_ctx, st = _ssd(proj, None, consts, seq=SEQ, nbatch=BATCH, row_blk0=0, want_state=True)
            h0t = state_ssd[:, j].transpose(0, 1, 4, 2, 3).reshape(DEC_BATCH, 2, SSD_STATE, SSD_D_INNER)
            y_lat, _ = _ssd(proj, h0t, consts, seq=DEC_SEQ, nbatch=DEC_BATCH, row_blk0=N_CTX // DEC_SEQ,
                            want_state=False)
            x, h, aff = _out_proj(o_ctx, o_lat, 0, y_ctx, y_lat, 0, w_out_even[j].astype(BF16), x, mod_l,
                                  g_ffn, w_router[l].T)
            new_na_k.append(proj[:N_CTX, NA_WIDTH:2 * NA_WIDTH].reshape(BATCH, SEQ, NA_WIDTH))
            new_na_v.append(proj[:N_CTX, 2 * NA_WIDTH:3 * NA_WIDTH].reshape(BATCH, SEQ, NA_WIDTH))
            new_ssd.append(st)
        else:
            proj = _proj_in(x, g_mix, mod_l, w_in_odd[j].astype(BF16))
            o_ctx = _ctx_swa(proj, _sink_rows(swa_sink[j], SEQ))
            ck = cache_swa_k[:, j].reshape(DEC_BATCH, PAST_LEN, SWA_KV)
            cv = cache_swa_v[:, j].reshape(DEC_BATCH, PAST_LEN, SWA_KV)
            o_lat = _lat_swa(proj, ck, cv, cos_t, sin_t, _sink_rows(swa_sink[j], SWA_BLOCK))
            x, h, aff = _out_proj(o_ctx, o_lat, 0, o_ctx, o_lat, 1, w_out_odd[j].astype(BF16), x, mod_l,
                                  g_ffn, w_router[l].T)
            new_swa_k.append(proj[:N_CTX, SWA_Q:SWA_Q + SWA_KV].reshape(BATCH, SEQ, SWA_KV))
            new_swa_v.append(proj[:N_CTX, SWA_Q + SWA_KV:].reshape(BATCH, SEQ, SWA_KV))
        x = _moe_layer(l, x, h, aff, mod_l, w_gate, w_up, w_down)
    y_ctx, y_lat = _final_norm(x, final_norm.reshape(1, d))
    n_even, n_odd = len(new_na_k), len(new_swa_k)
    na_shape = (BATCH, n_even, SEQ, NA_HEADS, HEAD_DIM)
    swa_shape = (BATCH, n_odd, SEQ, SWA_KV_HEADS, HEAD_DIM)
    ssd_t = jnp.stack(new_ssd, axis=1).reshape(BATCH, n_even, 2, SSD_STATE, SSD_HEADS, SSD_HEAD_DIM)
    return (y_ctx.reshape(BATCH, SEQ, d), y_lat.reshape(DEC_BATCH, DEC_SEQ, d),
            jnp.stack(new_na_k, axis=1).reshape(na_shape), jnp.stack(new_na_v, axis=1).reshape(na_shape),
            ssd_t.transpose(0, 1, 2, 4, 5, 3),
            jnp.stack(new_swa_k, axis=1).reshape(swa_shape), jnp.stack(new_swa_v, axis=1).reshape(swa_shape))
```

```python
import functools
import math

import jax
import jax.numpy as jnp
import numpy as np
from jax import lax
from jax.experimental import pallas as pl
from jax.experimental.pallas import tpu as pltpu

F32 = jnp.float32
BF16 = jnp.bfloat16
I32 = jnp.int32
HIGHEST = lax.Precision.HIGHEST

D_MODEL = 1024
BATCH = 16
SEQ = 256
DEPTH = 4
DEC_BATCH = 4
DEC_SEQ = 2048
PAST_LEN = 512
GRID_W = 64
HEAD_DIM = 64
NA_HEADS = 8
NA_KR = 8
NA_KC = 16
SSD_HEADS = 8
SSD_HEAD_DIM = 64
SSD_D_INNER = SSD_HEADS * SSD_HEAD_DIM
SSD_STATE = 128
SSD_GROUPS = 2
SSD_CHUNK = 128
SSD_CONV_W = 5
SWA_HEADS = 16
SWA_KV_HEADS = 4
SWA_WINDOW = 128
SWA_BLOCK = 128
ROPE_BASE = 10000.0
N_EXPERTS = 16
EXPERT_FF = 2048
EC_CAPACITY = 2
NORM_EPS = 1e-6

N_CTX = BATCH * SEQ
N_LAT = DEC_BATCH * DEC_SEQ
N_ALL = N_CTX + N_LAT
NA_WIDTH = NA_HEADS * HEAD_DIM
SSD_BC = SSD_GROUPS * SSD_STATE
SSD_CONV_CH = SSD_D_INNER + 2 * SSD_BC
EVEN_IN = 3 * NA_WIDTH + SSD_D_INNER + SSD_CONV_CH + 2 * SSD_HEADS
EVEN_IN_PAD = 3200
SWA_Q = SWA_HEADS * HEAD_DIM
SWA_KV = SWA_KV_HEADS * HEAD_DIM
ODD_IN = SWA_Q + 2 * SWA_KV
CAP_CTX = EC_CAPACITY * N_CTX // N_EXPERTS
CAP_LAT = EC_CAPACITY * N_LAT // N_EXPERTS
CAP_ALL = CAP_CTX + CAP_LAT
LANES = 128
NEG_BIG = -1e30
ATT_SCALE = HEAD_DIM ** -0.5
VMEM_LIMIT = 56 * 1024 * 1024


def _params(sem):
    return pltpu.CompilerParams(dimension_semantics=sem, vmem_limit_bytes=VMEM_LIMIT)


def _group_of_block(m, tm):
    ctx_blocks = N_CTX // tm
    return lax.select(m < ctx_blocks, jnp.int32(0), 1 + (m - ctx_blocks) // (DEC_SEQ // tm))


def _norm_mod(x, g, shift, scale):
    ms = jnp.mean(x * x, axis=-1, keepdims=True)
    y = x * lax.rsqrt(ms + NORM_EPS) * g
    return y * (1.0 + scale) + shift


def _silu(x):
    return x * jax.nn.sigmoid(x)


def _softplus(x):
    return jnp.maximum(x, 0.0) + jnp.log1p(jnp.exp(-jnp.abs(x)))


def _dot(a, b, **kw):
    return jnp.dot(a, b, preferred_element_type=F32, **kw)


def _dot_nt(a, b, **kw):
    return lax.dot_general(a, b, (((1,), (1,)), ((), ())), preferred_element_type=F32, **kw)


def _split_bf16(x):
    hi = x.astype(BF16)
    return hi, (x - hi.astype(F32)).astype(BF16)


def _value_with_ones(v_tile, upper_half):
    lane = lax.broadcasted_iota(I32, v_tile.shape, 1)
    v = pltpu.roll(v_tile, HEAD_DIM, 1) if upper_half else v_tile
    return jnp.where(lane < HEAD_DIM, v, 1.0).astype(BF16)


def _softmax_pv(q, keys, vals, biases, extra=None):
    logits = []
    for k, b in zip(keys, biases):
        s = _dot_nt(q, k)
        logits.append(s if b is None else s + b)
    mx = extra
    for s in logits:
        for c in range(s.shape[1] // LANES):
            t = s[:, c * LANES:(c + 1) * LANES]
            mx = t if mx is None else jnp.maximum(mx, t)
    m = jnp.max(mx, axis=-1, keepdims=True)
    acc = None
    for s, v in zip(logits, vals):
        pv = _dot(jnp.exp(s - m).astype(BF16), v)
        acc = pv if acc is None else acc + pv
    den = pltpu.roll(acc, HEAD_DIM, 1)
    if extra is not None:
        den = den + jnp.exp(extra - m)
    return (acc / den)[:, 0:HEAD_DIM]


def _adaln_kernel(c_ref, w_ref, b_ref, o_ref):
    s_hi, s_lo = _split_bf16(_silu(c_ref[...]))
    w_hi, w_lo = _split_bf16(w_ref[0])
    o_ref[0] = _dot(s_hi, w_hi) + (_dot(s_lo, w_hi) + _dot(s_hi, w_lo)) + b_ref[0]


def _adaln(cond8, w_mod, b_mod):
    d = D_MODEL
    return pl.pallas_call(
        _adaln_kernel,
        grid=(DEPTH, 6),
        in_specs=[pl.BlockSpec((8, d), lambda l, n: (0, 0)),
                  pl.BlockSpec((1, d, d), lambda l, n: (l, 0, n)),
                  pl.BlockSpec((1, 1, d), lambda l, n: (l, 0, n))],
        out_specs=pl.BlockSpec((1, 8, d), lambda l, n: (l, 0, n)),
        out_shape=jax.ShapeDtypeStruct((DEPTH, 8, 6 * d), F32),
        compiler_params=_params(("arbitrary", "arbitrary")),
        name="adaln",
    )(cond8, w_mod, b_mod.reshape(DEPTH, 1, 6 * d))


def _token_row_specs(x, tm):
    if not isinstance(x, tuple):
        return [x], [pl.BlockSpec((tm, x.shape[1]), lambda m: (m, 0))]
    nctx = N_CTX // tm
    x_ctx, x_lat = x
    return ([x_ctx, x_lat],
            [pl.BlockSpec((tm, x_ctx.shape[1]), lambda m: (jnp.minimum(m, nctx - 1), 0)),
             pl.BlockSpec((tm, x_lat.shape[1]), lambda m: (jnp.maximum(m - nctx, 0), 0))])


def _token_rows(x_refs, tm):
    if len(x_refs) == 1:
        return x_refs[0][...]
    return jnp.where(pl.program_id(0) < N_CTX // tm, x_refs[0][...], x_refs[1][...])


def _proj_in_kernel(*refs, tm):
    g_ref, mod_ref, w_ref, o_ref = refs[-4:]
    grp = _group_of_block(pl.program_id(0), tm)
    shift = mod_ref[pl.ds(grp, 1), 0:D_MODEL]
    scale = mod_ref[pl.ds(grp, 1), D_MODEL:2 * D_MODEL]
    h = _norm_mod(_token_rows(refs[:-4], tm), g_ref[...], shift, scale).astype(BF16)
    o_ref[...] = _dot(h, w_ref[...])


def _proj_in(x, g, mod_l, w_bf16, tm=512):
    n = w_bf16.shape[1]
    x_arrays, x_specs = _token_row_specs(x, tm)
    return pl.pallas_call(
        functools.partial(_proj_in_kernel, tm=tm),
        grid=(N_ALL // tm,),
        in_specs=x_specs + [pl.BlockSpec((1, D_MODEL), lambda m: (0, 0)),
                            pl.BlockSpec((8, 6 * D_MODEL), lambda m: (0, 0)),
                            pl.BlockSpec((D_MODEL, n), lambda m: (0, 0))],
        out_specs=pl.BlockSpec((tm, n), lambda m: (m, 0)),
        out_shape=jax.ShapeDtypeStruct((N_ALL, n), F32),
        compiler_params=_params(("arbitrary",)),
        name="proj_in",
    )(*x_arrays, g, mod_l, w_bf16)


def _router_affinities(h, wr):
    h_hi, h_lo = _split_bf16(h)
    w_hi, w_lo = _split_bf16(wr)
    logits = _dot_nt(w_hi, h_hi) + (_dot_nt(w_hi, h_lo) + _dot_nt(w_lo, h_hi))
    e = jnp.exp(logits - jnp.max(logits, axis=0, keepdims=True))
    return e / jnp.sum(e, axis=0, keepdims=True)


def _out_proj_kernel(ac_ref, al_ref, bc_ref, bl_ref, w_ref, mod_ref, g_ref, wr_ref, *refs, tm):
    x_refs, (o_ref, h_ref, aff_ref) = refs[:-3], refs[-3:]
    m = pl.program_id(0)
    grp = _group_of_block(m, tm)

    def mod_row(k):
        return mod_ref[pl.ds(grp, 1), k * D_MODEL:(k + 1) * D_MODEL]

    half = D_MODEL // 2
    is_ctx = m < N_CTX // tm
    a = jnp.where(is_ctx, ac_ref[...], al_ref[...]).astype(BF16)
    b = jnp.where(is_ctx, bc_ref[...], bl_ref[...]).astype(BF16)
    acc = _dot(a, w_ref[0:half, :]) + _dot(b, w_ref[half:, :])
    x_new = _token_rows(x_refs, tm) + mod_row(2) * acc
    o_ref[...] = x_new
    h = _norm_mod(x_new, g_ref[...], mod_row(3), mod_row(4))
    h_ref[...] = h
    aff = _router_affinities(h, wr_ref[...])
    for blk in range(tm // LANES):
        aff_ref[blk] = aff[:, blk * LANES:(blk + 1) * LANES]


def _out_proj(a_ctx, a_lat, acol, b_ctx, b_lat, bcol, w_bf16, x, mod_l, g_ffn, w_router_t, tm=512):
    half = D_MODEL // 2
    nctx = N_CTX // tm

    def ctx_map(col):
        return lambda m: (jnp.minimum(m, nctx - 1), col)

    def lat_map(col):
        return lambda m: (jnp.maximum(m - nctx, 0), col)

    x_arrays, x_specs = _token_row_specs(x, tm)
    return pl.pallas_call(
        functools.partial(_out_proj_kernel, tm=tm),
        grid=(N_ALL // tm,),
        in_specs=[pl.BlockSpec((tm, half), ctx_map(acol)),
                  pl.BlockSpec((tm, half), lat_map(acol)),
                  pl.BlockSpec((tm, half), ctx_map(bcol)),
                  pl.BlockSpec((tm, half), lat_map(bcol)),
                  pl.BlockSpec((D_MODEL, D_MODEL), lambda m: (0, 0)),
                  pl.BlockSpec((8, 6 * D_MODEL), lambda m: (0, 0)),
                  pl.BlockSpec((1, D_MODEL), lambda m: (0, 0)),
                  pl.BlockSpec((N_EXPERTS, D_MODEL), lambda m: (0, 0))] + x_specs,
        out_specs=[pl.BlockSpec((tm, D_MODEL), lambda m: (m, 0)),
                   pl.BlockSpec((tm, D_MODEL), lambda m: (m, 0)),
                   pl.BlockSpec((tm // LANES, N_EXPERTS, LANES), lambda m: (m, 0, 0))],
        out_shape=[jax.ShapeDtypeStruct((N_ALL, D_MODEL), F32),
                   jax.ShapeDtypeStruct((N_ALL, D_MODEL), F32),
                   jax.ShapeDtypeStruct((N_ALL // LANES, N_EXPERTS, LANES), F32)],
        compiler_params=_params(("arbitrary",)),
        name="out_proj",
    )(a_ctx, a_lat, b_ctx, b_lat, w_bf16, mod_l, g_ffn, w_router_t, *x_arrays)


def _ctx_na_kernel(q_ref, k_ref, v_ref, o_ref):
    for h in range(NA_HEADS):
        sl = slice(h * HEAD_DIM, (h + 1) * HEAD_DIM)
        tile = slice((h // 2) * LANES, (h // 2 + 1) * LANES)
        q = (q_ref[:, sl] * ATT_SCALE).astype(BF16)
        v = _value_with_ones(v_ref[:, tile], h % 2 == 1)
        o_ref[:, sl] = _softmax_pv(q, [k_ref[:, sl].astype(BF16)], [v], [None])


def _ctx_na(proj):
    w = NA_WIDTH
    return pl.pallas_call(
        _ctx_na_kernel,
        grid=(BATCH,),
        in_specs=[pl.BlockSpec((SEQ, w), lambda b: (b, 0)),
                  pl.BlockSpec((SEQ, w), lambda b: (b, 1)),
                  pl.BlockSpec((SEQ, w), lambda b: (b, 2))],
        out_specs=pl.BlockSpec((SEQ, w), lambda b: (b, 0)),
        out_shape=jax.ShapeDtypeStruct((N_CTX, w), F32),
        compiler_params=_params(("arbitrary",)),
        name="ctx_na",
    )(proj, proj, proj)


NA_QROWS = 4
NA_KROWS = 12
NA_ROWS = DEC_SEQ // GRID_W
NA_GROUPS = NA_ROWS // NA_QROWS


def _na_key_start(g):
    return jnp.clip(NA_QROWS * g - NA_KR // 2, 0, NA_ROWS - NA_KROWS)


def _na_row_geometry(g):
    start = int(np.clip(NA_QROWS * g - NA_KR // 2, 0, NA_ROWS - NA_KROWS))
    rows = [NA_QROWS * g + qr for qr in range(NA_QROWS)]
    return start, [(r, int(np.clip(r - NA_KR // 2, 0, NA_ROWS - NA_KR))) for r in rows]


def _na_fill_bias(ct_ref, bias_scr, g):
    start, rows = _na_row_geometry(g)
    masked = jnp.full((GRID_W, GRID_W), NEG_BIG, F32)
    for i in range(NA_HEADS):
        for qr, (r, rs) in enumerate(rows):
            for kr in range(NA_KROWS):
                keyrow = start + kr
                inside = rs <= keyrow < rs + NA_KR
                tile = ct_ref[i, keyrow - r + NA_KR - 1] if inside else masked
                bias_scr[i, qr * GRID_W:(qr + 1) * GRID_W, kr * GRID_W:(kr + 1) * GRID_W] = tile


def _lat_na_kernel(q_ref, k_ref, v_ref, ck_ref, cv_ref, ct_ref, o_ref, bias_scr, k_scr, v_scr, ck_scr, cv_scr):
    g = pl.program_id(1)
    for g_build in (0, 1, NA_GROUPS - 1):
        @pl.when(g == g_build)
        def _(g_build=g_build):
            _na_fill_bias(ct_ref, bias_scr, g_build)

    @pl.when(g == 0)
    def _():
        for i in range(NA_HEADS):
            sl = slice(i * HEAD_DIM, (i + 1) * HEAD_DIM)
            tile = slice((i // 2) * LANES, (i // 2 + 1) * LANES)
            k_scr[i] = k_ref[:, sl].astype(BF16)
            v_scr[i] = _value_with_ones(v_ref[:, tile], i % 2 == 1)
            ck_scr[i] = ck_ref[:, sl].astype(BF16)
            cv_scr[i] = _value_with_ones(cv_ref[:, tile], i % 2 == 1)

    start = pl.multiple_of(_na_key_start(g) * GRID_W, GRID_W)
    nk = NA_KROWS * GRID_W
    for i in range(NA_HEADS):
        sl = slice(i * HEAD_DIM, (i + 1) * HEAD_DIM)
        q = (q_ref[:, sl] * ATT_SCALE).astype(BF16)
        o_ref[:, sl] = _softmax_pv(q, [k_scr[i, pl.ds(start, nk), :], ck_scr[i]],
                                   [v_scr[i, pl.ds(start, nk), :], cv_scr[i]], [bias_scr[i], None])


def _na_col_table(rpb):
    w = np.arange(GRID_W)[:, None]
    cc = np.arange(GRID_W)[None, :]
    cs = np.clip(w - NA_KC // 2, 0, GRID_W - NA_KC)
    valid = (cc >= cs) & (cc < cs + NA_KC)
    dc = cc - w + NA_KC - 1
    onehot = ((dc[..., None] == np.arange(2 * NA_KC - 1)) & valid[..., None]).astype(np.float32)
    ct = jnp.einsum('hrd,wcd->hrwc', rpb, jnp.asarray(onehot), precision=HIGHEST)
    return jnp.where(jnp.asarray(valid)[None, None], ct, NEG_BIG)


def _lat_na(proj, ck, cv, col_tab):
    nq = NA_QROWS * GRID_W
    nk = NA_KROWS * GRID_W
    lat_q0 = N_CTX // nq
    lat_b0 = N_CTX // DEC_SEQ
    w = NA_WIDTH
    return pl.pallas_call(
        _lat_na_kernel,
        grid=(DEC_BATCH, NA_GROUPS),
        in_specs=[pl.BlockSpec((nq, w), lambda b, g: (lat_q0 + b * NA_GROUPS + g, 0)),
                  pl.BlockSpec((DEC_SEQ, w), lambda b, g: (lat_b0 + b, 1)),
                  pl.BlockSpec((DEC_SEQ, w), lambda b, g: (lat_b0 + b, 2)),
                  pl.BlockSpec((None, PAST_LEN, w), lambda b, g: (b, 0, 0)),
                  pl.BlockSpec((None, PAST_LEN, w), lambda b, g: (b, 0, 0)),
                  pl.BlockSpec((NA_HEADS, 2 * NA_KR - 1, GRID_W, GRID_W), lambda b, g: (0, 0, 0, 0))],
        out_specs=pl.BlockSpec((nq, w), lambda b, g: (b * NA_GROUPS + g, 0)),
        out_shape=jax.ShapeDtypeStruct((N_LAT, w), F32),
        scratch_shapes=[pltpu.VMEM((NA_HEADS, nq, nk), F32),
                        pltpu.VMEM((NA_HEADS, DEC_SEQ, HEAD_DIM), BF16),
                        pltpu.VMEM((NA_HEADS, DEC_SEQ, LANES), BF16),
                        pltpu.VMEM((NA_HEADS, PAST_LEN, HEAD_DIM), BF16),
                        pltpu.VMEM((NA_HEADS, PAST_LEN, LANES), BF16)],
        compiler_params=_params(("arbitrary", "arbitrary")),
        name="lat_na",
    )(proj, proj, proj, ck, cv, col_tab)


def _ssd_kernel(*refs, seq, has_h0, want_state):
    it = iter(refs)
    z_ref, xbc_ref, dt_ref = next(it), next(it), next(it)
    h0_ref = next(it) if has_h0 else None
    cw_ref, cb_ref, a_ref, dtb_ref, dsk_ref, ng_ref = (next(it) for _ in range(6))
    y_ref = next(it)
    st_ref = next(it) if want_state else None
    xc_scr, y_scr, ht_scr = next(it), next(it), next(it)

    nc = seq // SSD_CHUNK
    ch = SSD_CHUNK
    row = lax.broadcasted_iota(I32, (ch, ch), 0)
    col = lax.broadcasted_iota(I32, (ch, ch), 1)
    erow = lax.broadcasted_iota(I32, (LANES, SSD_D_INNER), 0)
    ecol = lax.broadcasted_iota(I32, (LANES, SSD_D_INNER), 1) // SSD_HEAD_DIM
    srow = lax.broadcasted_iota(I32, (LANES, SSD_HEADS * LANES), 0)
    scol = lax.broadcasted_iota(I32, (LANES, SSD_HEADS * LANES), 1) // LANES
    lane128 = lax.broadcasted_iota(I32, (1, LANES), 1)
    cbias = cb_ref[...]

    def conv_chunk(c, carry):
        base = pl.multiple_of(c * ch, ch)
        cur = xbc_ref[pl.ds(base, ch), :]
        pbase = pl.multiple_of(jnp.maximum(base - 8, 0), 8)
        nbase = pl.multiple_of(jnp.minimum(base + ch, seq - 8), 8)
        prev = jnp.where(c > 0, xbc_ref[pl.ds(pbase, 8), :], 0.0)
        nxt = jnp.where(c < nc - 1, xbc_ref[pl.ds(nbase, 8), :], 0.0)
        win = jnp.concatenate([prev, cur, nxt], axis=0)
        acc = jnp.broadcast_to(cbias, (ch, SSD_CONV_CH))
        pad = SSD_CONV_W // 2
        for k in range(SSD_CONV_W):
            off = 8 - pad + k
            acc = acc + win[off:off + ch, :] * cw_ref[k:k + 1, :]
        xc_scr[pl.ds(base, ch), :] = _silu(acc)
        return carry

    lax.fori_loop(0, nc, conv_chunk, 0)

    def run_direction(d):
        lane0 = d * SSD_HEADS
        expand = jnp.where(erow == ecol + lane0, 1.0, 0.0).astype(BF16)
        spread = jnp.where(srow == scol + lane0, 1.0, 0.0).astype(BF16)
        lmask = (col <= row) if d == 0 else (col >= row)
        tri = jnp.where(lmask, 1.0, 0.0).astype(BF16)

        def dot_split(lhs01, x, lhs_first):
            hi = x.astype(BF16)
            lo = (x - hi.astype(F32)).astype(BF16)
            if lhs_first:
                return _dot(lhs01, hi) + _dot(lhs01, lo)
            return _dot(hi, lhs01) + _dot(lo, lhs01)
        if has_h0:
            ht_scr[...] = h0_ref[d]
        else:
            ht_scr[...] = jnp.zeros((SSD_STATE, SSD_D_INNER), F32)

        def chunk(step, carry):
            c = step if d == 0 else nc - 1 - step
            base = pl.multiple_of(c * ch, ch)
            xs = xc_scr[pl.ds(base, ch), 0:SSD_D_INNER]
            dt = _softplus(dt_ref[pl.ds(base, ch), :] + dtb_ref[...])
            a = dt * a_ref[...]
            cum = dot_split(tri, a, True)
            tot = cum[ch - 1:ch, :] if d == 0 else cum[0:1, :]
            cum_t = cum.T
            dt_x = dot_split(expand, dt, False)
            cum_x = dot_split(expand, cum, False)
            cum_w = dot_split(spread, cum, False)
            tot_x = dot_split(expand, jnp.broadcast_to(tot, (8, LANES)), False)[0:1, :]
            xt = xs * dt_x
            xd = (xt * jnp.exp(tot_x - cum_x)).astype(BF16)
            ecum_x = jnp.exp(cum_x)
            y_parts = []
            for grp in range(SSD_GROUPS):
                bsl = slice(SSD_D_INNER + grp * SSD_STATE, SSD_D_INNER + (grp + 1) * SSD_STATE)
                csl = slice(SSD_D_INNER + SSD_BC + grp * SSD_STATE, SSD_D_INNER + SSD_BC + (grp + 1) * SSD_STATE)
                b_f = xc_scr[pl.ds(base, ch), bsl]
                b_g = b_f.astype(BF16)
                c_g = xc_scr[pl.ds(base, ch), csl].astype(BF16)
                cb = _dot_nt(c_g, b_g)
                hsl = slice(grp * 4 * SSD_HEAD_DIM, (grp + 1) * 4 * SSD_HEAD_DIM)
                ht_g = ht_scr[:, hsl]
                y_off = _dot(c_g, ht_g.astype(BF16))
                for pair in range(2):
                    h0 = grp * 4 + 2 * pair
                    decayed = []
                    for head in (h0, h0 + 1):
                        cum_col = cum_w[:, head * LANES:(head + 1) * LANES]
                        cum_row = cum_t[lane0 + head:lane0 + head + 1, :]
                        ldec = jnp.exp(jnp.where(lmask, cum_col - cum_row, NEG_BIG))
                        decayed.append((cb * ldec).astype(BF16))
                    tile = slice((h0 // 2) * LANES, (h0 // 2 + 1) * LANES)
                    x_pair = xt[:, tile]
                    x_diag = jnp.concatenate([jnp.where(lane128 < SSD_HEAD_DIM, x_pair, 0.0),
                                              jnp.where(lane128 >= SSD_HEAD_DIM, x_pair, 0.0)], axis=0)
                    y_d = _dot(jnp.concatenate(decayed, axis=1), x_diag.astype(BF16))
                    y_parts.append(y_d + y_off[:, pair * LANES:(pair + 1) * LANES] * ecum_x[:, tile])
                ht_scr[:, hsl] = ht_g * jnp.exp(tot_x[:, hsl]) + _dot(b_f.T.astype(BF16), xd[:, hsl])
            y = jnp.concatenate(y_parts, axis=1)
            if d == 0:
                y_scr[pl.ds(base, ch), :] = y
            else:
                y = y + y_scr[pl.ds(base, ch), :] + dsk_ref[...] * xs
                u = y * _silu(z_ref[pl.ds(base, ch), :])
                ms = jnp.mean(u * u, axis=-1, keepdims=True)
                y_ref[pl.ds(base, ch), :] = u * lax.rsqrt(ms + NORM_EPS) * ng_ref[...]
            return carry

        lax.fori_loop(0, nc, chunk, 0, unroll=2)
        if want_state:
            st_ref[d] = ht_scr[...]

    run_direction(0)
    run_direction(1)


def _ssd(proj, h0t, consts, *, seq, nbatch, row_blk0, want_state):
    has_h0 = h0t is not None
    in_specs = [pl.BlockSpec((seq, SSD_D_INNER), lambda b: (row_blk0 + b, 3)),
                pl.BlockSpec((seq, SSD_CONV_CH), lambda b: (row_blk0 + b, 2)),
                pl.BlockSpec((seq, LANES), lambda b: (row_blk0 + b, 24))]
    args = [proj, proj, proj]
    if has_h0:
        in_specs.append(pl.BlockSpec((None, 2, SSD_STATE, SSD_D_INNER), lambda b: (b, 0, 0, 0)))
        args.append(h0t)
    for cst in consts:
        in_specs.append(pl.BlockSpec(cst.shape, lambda b: (0, 0)))
        args.append(cst)
    out_specs = [pl.BlockSpec((seq, SSD_D_INNER), lambda b: (b, 0))]
    out_shape = [jax.ShapeDtypeStruct((nbatch * seq, SSD_D_INNER), F32)]
    if want_state:
        out_specs.append(pl.BlockSpec((None, 2, SSD_STATE, SSD_D_INNER), lambda b: (b, 0, 0, 0)))
        out_shape.append(jax.ShapeDtypeStruct((nbatch, 2, SSD_STATE, SSD_D_INNER), F32))
    res = pl.pallas_call(
        functools.partial(_ssd_kernel, seq=seq, has_h0=has_h0, want_state=want_state),
        grid=(nbatch,),
        in_specs=in_specs,
        out_specs=out_specs,
        out_shape=out_shape,
        scratch_shapes=[pltpu.VMEM((seq, SSD_CONV_CH), F32),
                        pltpu.VMEM((seq, SSD_D_INNER), F32),
                        pltpu.VMEM((SSD_STATE, SSD_D_INNER), F32)],
        compiler_params=_params(("arbitrary",)),
        name="ssd_seq%d" % seq,
    )(*args)
    return res if want_state else (res[0], None)


def _ctx_swa_kernel(q_ref, k_ref, v_ref, sink_ref, o_ref):
    grp = SWA_HEADS // SWA_KV_HEADS
    for kv in range(SWA_KV_HEADS):
        ksl = slice(kv * HEAD_DIM, (kv + 1) * HEAD_DIM)
        tile = slice((kv // 2) * LANES, (kv // 2 + 1) * LANES)
        k = k_ref[:, ksl].astype(BF16)
        v = _value_with_ones(v_ref[:, tile], kv % 2 == 1)
        q4 = jnp.concatenate(
            [q_ref[:, (kv * grp + j) * HEAD_DIM:(kv * grp + j + 1) * HEAD_DIM] for j in range(grp)],
            axis=0)
        o = _softmax_pv((q4 * ATT_SCALE).astype(BF16), [k], [v], [None], extra=sink_ref[kv])
        for j in range(grp):
            o_ref[:, (kv * grp + j) * HEAD_DIM:(kv * grp + j + 1) * HEAD_DIM] = o[j * SEQ:(j + 1) * SEQ]


def _ctx_swa(proj, sink_col):
    return pl.pallas_call(
        _ctx_swa_kernel,
        grid=(BATCH,),
        in_specs=[pl.BlockSpec((SEQ, SWA_Q), lambda b: (b, 0)),
                  pl.BlockSpec((SEQ, SWA_KV), lambda b: (b, 4)),
                  pl.BlockSpec((SEQ, SWA_KV), lambda b: (b, 5)),
                  pl.BlockSpec(sink_col.shape, lambda b: (0, 0, 0))],
        out_specs=pl.BlockSpec((SEQ, SWA_Q), lambda b: (b, 0)),
        out_shape=jax.ShapeDtypeStruct((N_CTX, SWA_Q), F32),
        compiler_params=_params(("arbitrary",)),
        name="ctx_swa",
    )(proj, proj, proj, sink_col)


SWA_NLOC = 3 * SWA_BLOCK


def _rope(x, cos, sin_signed, first):
    n = x.shape[-1]
    partner = jnp.where(first, pltpu.roll(x, n - 16, 1), pltpu.roll(x, 16, 1))
    return x * cos + partner * sin_signed


def _lat_swa_kernel(q_ref, k_ref, v_ref, ck_ref, cv_ref, cosk_ref, sink_k_ref, cosq_ref, sinq_ref, sink_ref,
                    o_ref, kr_scr, v_scr, ck_scr, cv_scr):
    i = pl.program_id(1)
    grp = SWA_HEADS // SWA_KV_HEADS
    lane = lax.broadcasted_iota(I32, (1, SWA_KV), 1)
    first = (lane % 32) < 16

    @pl.when(i == 0)
    def _():
        kr = _rope(k_ref[...], cosk_ref[...], sink_k_ref[...], first)
        for kv in range(SWA_KV_HEADS):
            ksl = slice(kv * HEAD_DIM, (kv + 1) * HEAD_DIM)
            tile = slice((kv // 2) * LANES, (kv // 2 + 1) * LANES)
            kr_scr[kv] = kr[:, ksl].astype(BF16)
            v_scr[kv] = _value_with_ones(v_ref[:, tile], kv % 2 == 1)
            ck_scr[kv] = ck_ref[:, ksl].astype(BF16)
            cv_scr[kv] = _value_with_ones(cv_ref[:, tile], kv % 2 == 1)

    kstart = pl.multiple_of(jnp.clip((i - 1) * SWA_BLOCK, 0, DEC_SEQ - SWA_NLOC), SWA_BLOCK)
    qpos = i * SWA_BLOCK + lax.broadcasted_iota(I32, (SWA_BLOCK, SWA_NLOC), 0)
    kpos = kstart + lax.broadcasted_iota(I32, (SWA_BLOCK, SWA_NLOC), 1)
    mask1 = jnp.where(jnp.abs(qpos - kpos) <= SWA_WINDOW, 0.0, NEG_BIG)
    mask = jnp.concatenate([mask1] * grp, axis=0)
    cosq = cosq_ref[...]
    sinq = sinq_ref[...]
    for kv in range(SWA_KV_HEADS):
        qr = _rope(q_ref[:, kv * SWA_KV:(kv + 1) * SWA_KV], cosq, sinq, first) * ATT_SCALE
        q4 = jnp.concatenate([qr[:, j * HEAD_DIM:(j + 1) * HEAD_DIM] for j in range(grp)], axis=0).astype(BF16)
        o = _softmax_pv(q4, [kr_scr[kv, pl.ds(kstart, SWA_NLOC), :], ck_scr[kv]],
                        [v_scr[kv, pl.ds(kstart, SWA_NLOC), :], cv_scr[kv]], [mask, None], extra=sink_ref[kv])
        for j in range(grp):
            o_ref[:, (kv * grp + j) * HEAD_DIM:(kv * grp + j + 1) * HEAD_DIM] = o[j * SWA_BLOCK:(j + 1) * SWA_BLOCK]


def _rope_tables():
    half = HEAD_DIM // 2
    quarter = half // 2
    pos = jnp.arange(DEC_SEQ)
    inv_freq = 1.0 / (ROPE_BASE ** (jnp.arange(quarter, dtype=F32) * 2.0 / half))
    d = np.arange(HEAD_DIM)
    use_col = jnp.asarray(d >= half)
    p = jnp.where(use_col[None, :], (pos % GRID_W)[:, None], (pos // GRID_W)[:, None]).astype(F32)
    ang = p * inv_freq[d % quarter][None, :]
    sign = jnp.asarray(np.where((d % half) < quarter, -1.0, 1.0), F32)
    cos = jnp.cos(ang)
    sin_signed = jnp.sin(ang) * sign[None, :]
    reps = SWA_KV // HEAD_DIM
    return jnp.tile(cos, (1, reps)), jnp.tile(sin_signed, (1, reps))


def _lat_swa(proj, ck, cv, cos_t, sin_t, sink_col):
    nb = DEC_SEQ // SWA_BLOCK
    q0 = N_CTX // SWA_BLOCK
    b0 = N_CTX // DEC_SEQ
    return pl.pallas_call(
        _lat_swa_kernel,
        grid=(DEC_BATCH, nb),
        in_specs=[pl.BlockSpec((SWA_BLOCK, SWA_Q), lambda b, i: (q0 + b * nb + i, 0)),
                  pl.BlockSpec((DEC_SEQ, SWA_KV), lambda b, i: (b0 + b, 4)),
                  pl.BlockSpec((DEC_SEQ, SWA_KV), lambda b, i: (b0 + b, 5)),
                  pl.BlockSpec((None, PAST_LEN, SWA_KV), lambda b, i: (b, 0, 0)),
                  pl.BlockSpec((None, PAST_LEN, SWA_KV), lambda b, i: (b, 0, 0)),
                  pl.BlockSpec((DEC_SEQ, SWA_KV), lambda b, i: (0, 0)),
                  pl.BlockSpec((DEC_SEQ, SWA_KV), lambda b, i: (0, 0)),
                  pl.BlockSpec((SWA_BLOCK, SWA_KV), lambda b, i: (i, 0)),
                  pl.BlockSpec((SWA_BLOCK, SWA_KV), lambda b, i: (i, 0)),
                  pl.BlockSpec(sink_col.shape, lambda b, i: (0, 0, 0))],
        out_specs=pl.BlockSpec((SWA_BLOCK, SWA_Q), lambda b, i: (b * nb + i, 0)),
        out_shape=jax.ShapeDtypeStruct((N_LAT, SWA_Q), F32),
        scratch_shapes=[pltpu.VMEM((SWA_KV_HEADS, DEC_SEQ, HEAD_DIM), BF16),
                        pltpu.VMEM((SWA_KV_HEADS, DEC_SEQ, LANES), BF16),
                        pltpu.VMEM((SWA_KV_HEADS, PAST_LEN, HEAD_DIM), BF16),
                        pltpu.VMEM((SWA_KV_HEADS, PAST_LEN, LANES), BF16)],
        compiler_params=_params(("arbitrary", "arbitrary")),
        name="lat_swa",
    )(proj, proj, proj, ck, cv, cos_t, sin_t, cos_t, sin_t, sink_col)


ROUTE_NB = N_ALL // LANES
ROUTE_SETS = ((0, N_CTX // LANES, CAP_CTX, 0), (N_CTX // LANES, N_LAT // LANES, CAP_LAT, CAP_CTX))


def _route_kernel(affb_ref, idx_ref, gate_ref, st_ref, posb_ref, aff_ref, pos_ref, t_scr, cend_scr):
    ne = N_EXPERTS
    for e in range(ne):
        aff_ref[e] = affb_ref[:, e, :]
    r = lax.broadcasted_iota(I32, (LANES, LANES), 0)
    c = lax.broadcasted_iota(I32, (LANES, LANES), 1)
    upper = jnp.where(r <= c, 1.0, 0.0).astype(BF16)
    lower_incl = jnp.where(c <= r, 1.0, 0.0).astype(BF16)
    lower_strict = jnp.where(c < r, 1.0, 0.0).astype(BF16)
    lane = lax.broadcasted_iota(I32, (8, LANES), 1)

    def cumsum_tokens(x):
        rowc = _dot(x.astype(BF16), upper)
        tot = jnp.broadcast_to(rowc[:, LANES - 1:LANES], (LANES, LANES))
        return rowc + _dot(lower_strict, tot.astype(BF16)), tot

    for set_id, (b0, nb, cap, slot0) in enumerate(ROUTE_SETS):
        capf = float(cap)
        npad = LANES - nb

        def count_ge(e, cand, b0=b0, nb=nb):
            hit = jnp.where(aff_ref[e, b0:b0 + nb, :] >= cand, 1.0, 0.0)
            return jnp.sum(jnp.sum(hit, axis=1, keepdims=True), axis=0, keepdims=True)

        def bit_step(i, ts, count_ge=count_ge, capf=capf):
            bit = jnp.left_shift(jnp.int32(1), 30 - i)
            out = []
            for e in range(ne):
                cand = ts[e] | bit
                keep = count_ge(e, lax.bitcast_convert_type(cand, F32)) >= capf
                out.append(jnp.where(keep, cand, ts[e]))
            return tuple(out)

        ts = lax.fori_loop(0, 31, bit_step, tuple(jnp.zeros((1, 1), I32) for _ in range(ne)))
        for e in range(ne):
            t_scr[e] = jnp.broadcast_to(lax.bitcast_convert_type(ts[e], F32), (8, LANES))

        slot = lax.broadcasted_iota(I32, (LANES, cap), 1).astype(F32)
        sub = lax.broadcasted_iota(I32, (LANES, cap), 0).astype(F32)

        def per_expert(e, carry, set_id=set_id, b0=b0, nb=nb, cap=cap, slot0=slot0, capf=capf, npad=npad,
                       slot=slot, sub=sub):
            a = aff_ref[e, b0:b0 + nb, :]
            if npad:
                a = jnp.concatenate([a, jnp.full((npad, LANES), -1.0, F32)], axis=0)
            thr = t_scr[e][0:1, 0:1]
            gt = a > thr
            eq = a == thr
            gtf = jnp.where(gt, 1.0, 0.0)
            eqf = jnp.where(eq, 1.0, 0.0)
            need = capf - jnp.sum(jnp.sum(gtf, axis=1, keepdims=True), axis=0, keepdims=True)
            eq_incl, _ = cumsum_tokens(eqf)
            sel = gt | (eq & (eq_incl - eqf < need))
            self = jnp.where(sel, 1.0, 0.0)
            incl, tot = cumsum_tokens(self)
            cend = _dot(lower_incl, tot.astype(BF16))
            blk = jnp.sum(jnp.where(cend[:, 0:1] <= slot, 1.0, 0.0), axis=0, keepdims=True)
            onehot = jnp.where(sub == blk, 1.0, 0.0)
            hi = jnp.floor(incl * (1.0 / LANES))
            lo = incl - hi * LANES
            a_pos = jnp.maximum(a, 0.0)
            a1 = a_pos.astype(BF16)
            r1 = a_pos - a1.astype(F32)
            a2 = r1.astype(BF16)
            a3 = (r1 - a2.astype(F32)).astype(BF16)
            stacked = jnp.concatenate([hi.T.astype(BF16), lo.T.astype(BF16), a1.T, a2.T, a3.T], axis=0)
            picked = _dot(stacked, onehot.astype(BF16))
            inc_s = picked[0:LANES] * LANES + picked[LANES:2 * LANES]
            within = jnp.sum(jnp.where(inc_s <= slot, 1.0, 0.0), axis=0, keepdims=True)
            aff_s = picked[2 * LANES:3 * LANES] + picked[3 * LANES:4 * LANES] + picked[4 * LANES:5 * LANES]
            gate = jnp.sum(jnp.where(sub == within, aff_s, 0.0), axis=0, keepdims=True)
            idx_ref[e, :, slot0:slot0 + cap] = ((blk + float(b0)) * LANES + within).astype(I32)
            gate_ref[e, :, slot0:slot0 + cap] = gate
            pos_ref[e, b0:b0 + nb, :] = jnp.where(sel, incl - 1.0 + float(slot0), -1.0)[0:nb, :]
            ends = pltpu.roll(cend.T[0:8, :] + float(slot0), b0 + 1, 1)
            if set_id == 0:
                cend_scr[e] = jnp.where(lane == 0, 0.0, ends)
            else:
                st_ref[e] = jnp.where(lane <= b0, cend_scr[e], ends).astype(I32)
            return carry

        lax.fori_loop(0, ne, per_expert, 0, unroll=2)

    for e in range(ne):
        posb_ref[:, e, :] = pos_ref[e]


def _route(aff3):
    ne = N_EXPERTS
    return pl.pallas_call(
        _route_kernel,
        grid=(1,),
        in_specs=[pl.BlockSpec((ROUTE_NB, ne, LANES), lambda i: (0, 0, 0))],
        out_specs=[pl.BlockSpec((ne, 1, CAP_ALL), lambda i: (0, 0, 0)),
                   pl.BlockSpec((ne, 1, CAP_ALL), lambda i: (0, 0, 0)),
                   pl.BlockSpec((ne, 8, LANES), lambda i: (0, 0, 0)),
                   pl.BlockSpec((ROUTE_NB, ne, LANES), lambda i: (0, 0, 0))],
        out_shape=[jax.ShapeDtypeStruct((ne, 1, CAP_ALL), I32),
                   jax.ShapeDtypeStruct((ne, 1, CAP_ALL), F32),
                   jax.ShapeDtypeStruct((ne, 8, LANES), I32),
                   jax.ShapeDtypeStruct((ROUTE_NB, ne, LANES), F32)],
        scratch_shapes=[pltpu.VMEM((ne, ROUTE_NB, LANES), F32),
                        pltpu.VMEM((ne, ROUTE_NB, LANES), F32),
                        pltpu.VMEM((ne, 8, LANES), F32),
                        pltpu.VMEM((ne, 8, LANES), F32)],
        compiler_params=_params(("arbitrary",)),
        name="route",
    )(aff3)


FF_TILE = 512
COMB_PIECE = 16
CAP_PAD = CAP_ALL + COMB_PIECE


def _moe_ffn_kernel(idx_ref, h_hbm, gate_ref, wg_ref, wu_ref, wd_ref, o_ref, land, xe, acc, g_scr, u_scr, sem):
    e = pl.program_id(0)
    f = pl.program_id(1)
    nf = EXPERT_FF // FF_TILE
    rps = CAP_ALL // nf

    def row_copy(expert, part, s):
        tok = idx_ref[expert * CAP_ALL + part * rps + s]
        return pltpu.make_async_copy(h_hbm.at[pl.ds(tok, 1), :], land.at[part, pl.ds(s, 1), :], sem)

    def wait_rows():
        for part in range(nf):
            pltpu.make_async_copy(h_hbm.at[pl.ds(0, rps), :], land.at[part], sem).wait()

    @pl.when(f == 0)
    def _():
        @pl.when(e == 0)
        def _():
            for part in range(nf):
                def issue(s, carry, part=part):
                    row_copy(0, part, s).start()
                    return carry

                lax.fori_loop(0, rps, issue, 0)

        wait_rows()
        for part in range(nf):
            xe[part * rps:(part + 1) * rps, :] = land[part].astype(BF16)
        acc[...] = jnp.zeros_like(acc)

    nxt = (e + 1) % N_EXPERTS
    third = rps // 3

    def request_rows(lo, hi):
        for s in range(lo, hi):
            row_copy(nxt, f, s).start()

    @pl.when(f >= 0)
    def _():
        request_rows(0, third)
        g_scr[...] = _dot(xe[...], wg_ref[...].astype(BF16))

    @pl.when(e >= 0)
    def _():
        request_rows(third, 2 * third)
        u_scr[...] = _dot(xe[...], wu_ref[...].astype(BF16))

    def down_projection():
        request_rows(2 * third, rps)
        hid = (_silu(g_scr[...]) * u_scr[...]).astype(BF16)
        return _dot(hid, wd_ref[...].astype(BF16))

    @pl.when(f < nf - 1)
    def _():
        acc[...] += down_projection()

    @pl.when(f == nf - 1)
    def _():
        y = (acc[...] + down_projection()) * gate_ref[...]
        hi = y.astype(BF16)
        o_ref[0:CAP_ALL, 0:D_MODEL] = hi
        o_ref[0:CAP_ALL, D_MODEL:2 * D_MODEL] = (y - hi.astype(F32)).astype(BF16)
        o_ref[CAP_ALL:CAP_PAD, :] = jnp.zeros((CAP_PAD - CAP_ALL, 2 * D_MODEL), BF16)

        @pl.when(e == N_EXPERTS - 1)
        def _():
            wait_rows()


def _moe_ffn(layer, h, idx_flat, gate_col, w_gate, w_up, w_down):
    nf = EXPERT_FF // FF_TILE
    grid_spec = pltpu.PrefetchScalarGridSpec(
        num_scalar_prefetch=1,
        grid=(N_EXPERTS, nf),
        in_specs=[pl.BlockSpec(memory_space=pl.ANY),
                  pl.BlockSpec((None, CAP_ALL, 1), lambda e, f, idx: (e, 0, 0)),
                  pl.BlockSpec((None, None, D_MODEL, FF_TILE), lambda e, f, idx: (layer, e, 0, f)),
                  pl.BlockSpec((None, None, D_MODEL, FF_TILE), lambda e, f, idx: (layer, e, 0, f)),
                  pl.BlockSpec((None, None, FF_TILE, D_MODEL), lambda e, f, idx: (layer, e, f, 0))],
        out_specs=pl.BlockSpec((None, CAP_PAD, 2 * D_MODEL), lambda e, f, idx: (e, 0, 0)),
        scratch_shapes=[pltpu.VMEM((nf, CAP_ALL // nf, D_MODEL), F32),
                        pltpu.VMEM((CAP_ALL, D_MODEL), BF16),
                        pltpu.VMEM((CAP_ALL, D_MODEL), F32),
                        pltpu.VMEM((CAP_ALL, FF_TILE), F32),
                        pltpu.VMEM((CAP_ALL, FF_TILE), F32),
                        pltpu.SemaphoreType.DMA(())],
    )
    return pl.pallas_call(
        _moe_ffn_kernel,
        grid_spec=grid_spec,
        out_shape=jax.ShapeDtypeStruct((N_EXPERTS, CAP_PAD, 2 * D_MODEL), BF16),
        compiler_params=_params(("arbitrary", "arbitrary")),
        name="moe_ffn",
    )(idx_flat, h, gate_col, w_gate, w_up, w_down)


COMB_TM = 512
COMB_NB = N_ALL // COMB_TM
COMB_HALF = N_EXPERTS // 2
COMB_CHUNK = 256
COMB_ALIGN = 16
COMB_WAIT_GROUP = 32
COMB_STAGE = -(-(COMB_HALF * (COMB_TM + COMB_PIECE + COMB_ALIGN)) // COMB_CHUNK) * COMB_CHUNK


def _combine_kernel(st_ref, ye_hbm, pos_ref, x_ref, mod_ref, o_ref, stage, acc, sem):
    b = pl.program_id(0)
    nst = LANES
    per = COMB_TM // LANES

    def layout(blk, half):
        out = []
        off = jnp.int32(0)
        for j in range(COMB_HALF):
            e = half * COMB_HALF + j
            s0 = st_ref[e * nst + per * blk]
            s1 = st_ref[e * nst + per * blk + per]
            a0 = (s0 // COMB_ALIGN) * COMB_ALIGN
            npc = jnp.where(s1 > s0, (s1 - a0 + COMB_PIECE - 1) // COMB_PIECE, 0)
            out.append((e, a0, off, npc))
            off = off + npc * COMB_PIECE
        return out, off // COMB_PIECE

    def piece_copy(e, src_row, buf, dst_row, pieces=1):
        rows = pieces * COMB_PIECE
        return pltpu.make_async_copy(ye_hbm.at[e, pl.ds(src_row, rows), :],
                                     stage.at[buf, pl.ds(dst_row, rows), :], sem.at[buf])

    def issue_unit(blk, half):
        lay, _ = layout(blk, half)
        for e, a0, off, npc in lay:
            def issue(k, carry, e=e, a0=a0, off=off):
                piece_copy(e, pl.multiple_of(a0 + k * COMB_PIECE, COMB_ALIGN), half,
                           pl.multiple_of(off + k * COMB_PIECE, COMB_ALIGN)).start()
                return carry

            lax.fori_loop(0, npc, issue, 0)

    @pl.when(b == 0)
    def _():
        stage[...] = jnp.zeros_like(stage)
        issue_unit(0, 0)

    acc[...] = jnp.zeros_like(acc)
    pos = jnp.concatenate([pos_ref[blk] for blk in range(per)], axis=1)
    row_t = lax.broadcasted_iota(I32, (COMB_CHUNK, COMB_TM), 0).astype(F32)
    row_l = lax.broadcasted_iota(I32, (COMB_CHUNK, LANES), 0).astype(F32)
    lane = lax.broadcasted_iota(I32, (1, LANES), 1)
    for half in range(2):
        lay, npieces = layout(b, half)

        def wait_big(k, carry, half=half):
            piece_copy(0, 0, half, 0, COMB_WAIT_GROUP).wait()
            return carry

        lax.fori_loop(0, npieces // COMB_WAIT_GROUP, wait_big, 0)
        small = COMB_WAIT_GROUP // 2
        while small >= 1:
            @pl.when((npieces & small) != 0)
            def _(small=small, half=half):
                piece_copy(0, 0, half, 0, small).wait()

            small //= 2
        if half == 0:
            issue_unit(b, 1)
        else:
            @pl.when(b + 1 < COMB_NB)
            def _():
                issue_unit(b + 1, 0)

        srows = []
        first = jnp.full((1, LANES), 1e9, F32)
        last = jnp.zeros((1, LANES), F32)
        for j, (e, a0, off, npc) in enumerate(lay):
            p = pos[e:e + 1, :]
            srows.append(jnp.where(p >= 0.0, p + (off - a0).astype(F32), -1.0))
            first = jnp.where(lane == j, off.astype(F32), first)
            last = jnp.where(lane == j, (off + npc * COMB_PIECE).astype(F32), last)
        srow = jnp.concatenate(srows + [jnp.zeros((LANES - COMB_HALF, COMB_TM), F32)], axis=0)
        srow_hi = jnp.floor(srow * (1.0 / 64.0))
        srow_lo = (srow - 64.0 * srow_hi).astype(BF16)
        srow_hi = srow_hi.astype(BF16)

        def select_rows(ci, srow_hi=srow_hi, srow_lo=srow_lo, first=first, last=last):
            r0f = (ci * COMB_CHUNK).astype(F32)
            rid = row_l + r0f
            owner = jnp.where((rid >= first) & (rid < last), 1.0, 0.0).astype(BF16)
            want = 64.0 * _dot(owner, srow_hi) + _dot(owner, srow_lo)
            return jnp.where(want == row_t + r0f, 1.0, 0.0).T.astype(BF16)

        def chunk(ci, sel_t, half=half, select_rows=select_rows):
            sel_next = select_rows(ci + 1)
            r0 = pl.multiple_of(ci * COMB_CHUNK, COMB_CHUNK)
            acc[...] += (_dot(sel_t, stage[half, pl.ds(r0, COMB_CHUNK), 0:D_MODEL])
                         + _dot(sel_t, stage[half, pl.ds(r0, COMB_CHUNK), D_MODEL:2 * D_MODEL]))
            return sel_next

        nchunks = (npieces * COMB_PIECE + COMB_CHUNK - 1) // COMB_CHUNK
        lax.fori_loop(0, nchunks, chunk, select_rows(jnp.int32(0)))

    grp = _group_of_block(b, COMB_TM)
    gate = mod_ref[pl.ds(grp, 1), 5 * D_MODEL:6 * D_MODEL]
    o_ref[...] = x_ref[...] + gate * acc[...]


def _combine(ye, starts_flat, pos, x, mod_l):
    grid_spec = pltpu.PrefetchScalarGridSpec(
        num_scalar_prefetch=1,
        grid=(COMB_NB,),
        in_specs=[pl.BlockSpec(memory_space=pl.ANY),
                  pl.BlockSpec((COMB_TM // LANES, N_EXPERTS, LANES), lambda b, st: (b, 0, 0)),
                  pl.BlockSpec((COMB_TM, D_MODEL), lambda b, st: (b, 0)),
                  pl.BlockSpec((8, 6 * D_MODEL), lambda b, st: (0, 0))],
        out_specs=pl.BlockSpec((COMB_TM, D_MODEL), lambda b, st: (b, 0)),
        scratch_shapes=[pltpu.VMEM((2, COMB_STAGE, 2 * D_MODEL), BF16),
                        pltpu.VMEM((COMB_TM, D_MODEL), F32),
                        pltpu.SemaphoreType.DMA((2,))],
    )
    return pl.pallas_call(
        _combine_kernel,
        grid_spec=grid_spec,
        out_shape=jax.ShapeDtypeStruct((N_ALL, D_MODEL), F32),
        compiler_params=_params(("arbitrary",)),
        name="moe_combine",
    )(starts_flat, ye, pos, x, mod_l)


def _moe_layer(layer, x, h, aff, mod_l, w_gate, w_up, w_down):
    idx, gate, st, pos = _route(aff)
    ye = _moe_ffn(layer, h, idx.reshape(-1), gate.reshape(N_EXPERTS, CAP_ALL, 1), w_gate, w_up, w_down)
    return _combine(ye, st[:, 0, :].reshape(-1), pos, x, mod_l)


def _final_norm_kernel(x_ref, g_ref, oc_ref, ol_ref, *, tm):
    x = x_ref[...]
    ms = jnp.mean(x * x, axis=-1, keepdims=True)
    y = x * lax.rsqrt(ms + NORM_EPS) * g_ref[...]
    is_ctx = pl.program_id(0) < N_CTX // tm

    @pl.when(is_ctx)
    def _():
        oc_ref[...] = y

    @pl.when(jnp.logical_not(is_ctx))
    def _():
        ol_ref[...] = y


def _final_norm(x, g, tm=1024):
    nctx = N_CTX // tm
    return pl.pallas_call(
        functools.partial(_final_norm_kernel, tm=tm),
        grid=(N_ALL // tm,),
        in_specs=[pl.BlockSpec((tm, D_MODEL), lambda m: (m, 0)),
                  pl.BlockSpec((1, D_MODEL), lambda m: (0, 0))],
        out_specs=[pl.BlockSpec((tm, D_MODEL), lambda m: (jnp.minimum(m, nctx - 1), 0)),
                   pl.BlockSpec((tm, D_MODEL), lambda m: (jnp.maximum(m - nctx, 0), 0))],
        out_shape=[jax.ShapeDtypeStruct((N_CTX, D_MODEL), F32),
                   jax.ShapeDtypeStruct((N_LAT, D_MODEL), F32)],
        compiler_params=_params(("arbitrary",)),
        name="final_norm",
    )(x, g)


def _sink_rows(sink, rows):
    grp = SWA_HEADS // SWA_KV_HEADS
    col = jnp.repeat(sink.reshape(SWA_KV_HEADS, grp), rows, axis=1).reshape(SWA_KV_HEADS, grp * rows, 1)
    return jnp.broadcast_to(col, (SWA_KV_HEADS, grp * rows, LANES))


def _lane_row(v, width=LANES):
    return jnp.zeros((1, width), F32).at[0, :v.shape[0]].set(v)


def kernel(x_prompt, x_sample, cache_na_k, cache_na_v, state_ssd, cache_swa_k, cache_swa_v, c, c_ctx, norm_mix, norm_ffn, w_mod, b_mod, w_in_even, na_rpb, ssd_conv_w, ssd_conv_b, ssd_a_log, ssd_dt_bias, ssd_d, ssd_norm, w_out_even, w_in_odd, swa_sink, w_out_odd, w_router, w_gate, w_up, w_down, final_norm):
    d = D_MODEL
    x = (x_prompt.reshape(N_CTX, d), x_sample.reshape(N_LAT, d))
    cond8 = jnp.zeros((8, d), F32).at[0].set(c_ctx).at[1:1 + DEC_BATCH].set(c)
    mod = _adaln(cond8, w_mod, b_mod)
    cos_t, sin_t = _rope_tables()
    new_na_k, new_na_v, new_ssd, new_swa_k, new_swa_v = [], [], [], [], []
    for l in range(DEPTH):
        j = l // 2
        mod_l = mod[l]
        g_mix = norm_mix[l].reshape(1, d)
        g_ffn = norm_ffn[l].reshape(1, d)
        if l % 2 == 0:
            w_in = jnp.pad(w_in_even[j], ((0, 0), (0, EVEN_IN_PAD - EVEN_IN))).astype(BF16)
            proj = _proj_in(x, g_mix, mod_l, w_in)
            o_ctx = _ctx_na(proj)
            ck = cache_na_k[:, j].reshape(DEC_BATCH, PAST_LEN, NA_WIDTH)
            cv = cache_na_v[:, j].reshape(DEC_BATCH, PAST_LEN, NA_WIDTH)
            o_lat = _lat_na(proj, ck, cv, _na_col_table(na_rpb[j]))
            consts = [jnp.pad(ssd_conv_w[j], ((0, 8 - SSD_CONV_W), (0, 0))),
                      ssd_conv_b[j].reshape(1, SSD_CONV_CH),
                      _lane_row(-jnp.exp(ssd_a_log[j].reshape(-1))),
                      _lane_row(ssd_dt_bias[j].reshape(-1)),
                      jnp.repeat(ssd_d[j], SSD_HEAD_DIM).reshape(1, SSD_D_INNER),
                      ssd_norm[j].reshape(1, SSD_D_INNER)]
            y_ctx, st = _ssd(proj, None, consts, seq=SEQ, nbatch=BATCH, row_blk0=0, want_state=True)
            h0t = state_ssd[:, j].transpose(0, 1, 4, 2, 3).reshape(DEC_BATCH, 2, SSD_STATE, SSD_D_INNER)
            y_lat, _ = _ssd(proj, h0t, consts, seq=DEC_SEQ, nbatch=DEC_BATCH, row_blk0=N_CTX // DEC_SEQ,
                            want_state=False)
            x, h, aff = _out_proj(o_ctx, o_lat, 0, y_ctx, y_lat, 0, w_out_even[j].astype(BF16), x, mod_l,
                                  g_ffn, w_router[l].T)
            new_na_k.append(proj[:N_CTX, NA_WIDTH:2 * NA_WIDTH].reshape(BATCH, SEQ, NA_WIDTH))
            new_na_v.append(proj[:N_CTX, 2 * NA_WIDTH:3 * NA_WIDTH].reshape(BATCH, SEQ, NA_WIDTH))
            new_ssd.append(st)
        else:
            proj = _proj_in(x, g_mix, mod_l, w_in_odd[j].astype(BF16))
            o_ctx = _ctx_swa(proj, _sink_rows(swa_sink[j], SEQ))
            ck = cache_swa_k[:, j].reshape(DEC_BATCH, PAST_LEN, SWA_KV)
            cv = cache_swa_v[:, j].reshape(DEC_BATCH, PAST_LEN, SWA_KV)
            o_lat = _lat_swa(proj, ck, cv, cos_t, sin_t, _sink_rows(swa_sink[j], SWA_BLOCK))
            x, h, aff = _out_proj(o_ctx, o_lat, 0, o_ctx, o_lat, 1, w_out_odd[j].astype(BF16), x, mod_l,
                                  g_ffn, w_router[l].T)
            new_swa_k.append(proj[:N_CTX, SWA_Q:SWA_Q + SWA_KV].reshape(BATCH, SEQ, SWA_KV))
            new_swa_v.append(proj[:N_CTX, SWA_Q + SWA_KV:].reshape(BATCH, SEQ, SWA_KV))
        x = _moe_layer(l, x, h, aff, mod_l, w_gate, w_up, w_down)
    y_ctx, y_lat = _final_norm(x, final_norm.reshape(1, d))
    n_even, n_odd = len(new_na_k), len(new_swa_k)
    na_shape = (BATCH, n_even, SEQ, NA_HEADS, HEAD_DIM)
    swa_shape = (BATCH, n_odd, SEQ, SWA_KV_HEADS, HEAD_DIM)
    ssd_t = jnp.stack(new_ssd, axis=1).reshape(BATCH, n_even, 2, SSD_STATE, SSD_HEADS, SSD_HEAD_DIM)
    return (y_ctx.reshape(BATCH, SEQ, d), y_lat.reshape(DEC_BATCH, DEC_SEQ, d),
            jnp.stack(new_na_k, axis=1).reshape(na_shape), jnp.stack(new_na_v, axis=1).reshape(na_shape),
            ssd_t.transpose(0, 1, 2, 4, 5, 3),
            jnp.stack(new_swa_k, axis=1).reshape(swa_shape), jnp.stack(new_swa_v, axis=1).reshape(swa_shape))
```

```python
import functools
import math

import jax
import jax.numpy as jnp
import numpy as np
from jax import lax
from jax.experimental import pallas as pl
from jax.experimental.pallas import tpu as pltpu

F32 = jnp.float32
BF16 = jnp.bfloat16
I32 = jnp.int32
HIGHEST = lax.Precision.HIGHEST

D_MODEL = 1024
BATCH = 16
SEQ = 256
DEPTH = 4
DEC_BATCH = 4
DEC_SEQ = 2048
PAST_LEN = 512
GRID_W = 64
HEAD_DIM = 64
NA_HEADS = 8
NA_KR = 8
NA_KC = 16
SSD_HEADS = 8
SSD_HEAD_DIM = 64
SSD_D_INNER = SSD_HEADS * SSD_HEAD_DIM
SSD_STATE = 128
SSD_GROUPS = 2
SSD_CHUNK = 128
SSD_CONV_W = 5
SWA_HEADS = 16
SWA_KV_HEADS = 4
SWA_WINDOW = 128
SWA_BLOCK = 128
ROPE_BASE = 10000.0
N_EXPERTS = 16
EXPERT_FF = 2048
EC_CAPACITY = 2
NORM_EPS = 1e-6

N_CTX = BATCH * SEQ
N_LAT = DEC_BATCH * DEC_SEQ
N_ALL = N_CTX + N_LAT
NA_WIDTH = NA_HEADS * HEAD_DIM
SSD_BC = SSD_GROUPS * SSD_STATE
SSD_CONV_CH = SSD_D_INNER + 2 * SSD_BC
EVEN_IN = 3 * NA_WIDTH + SSD_D_INNER + SSD_CONV_CH + 2 * SSD_HEADS
EVEN_IN_PAD = 3200
SWA_Q = SWA_HEADS * HEAD_DIM
SWA_KV = SWA_KV_HEADS * HEAD_DIM
ODD_IN = SWA_Q + 2 * SWA_KV
CAP_CTX = EC_CAPACITY * N_CTX // N_EXPERTS
CAP_LAT = EC_CAPACITY * N_LAT // N_EXPERTS
CAP_ALL = CAP_CTX + CAP_LAT
LANES = 128
NEG_BIG = -1e30
ATT_SCALE = HEAD_DIM ** -0.5
VMEM_LIMIT = 56 * 1024 * 1024


def _params(sem):
    return pltpu.CompilerParams(dimension_semantics=sem, vmem_limit_bytes=VMEM_LIMIT)


def _group_of_block(m, tm):
    ctx_blocks = N_CTX // tm
    return lax.select(m < ctx_blocks, jnp.int32(0), 1 + (m - ctx_blocks) // (DEC_SEQ // tm))


def _norm_mod(x, g, shift, scale):
    ms = jnp.mean(x * x, axis=-1, keepdims=True)
    y = x * lax.rsqrt(ms + NORM_EPS) * g
    return y * (1.0 + scale) + shift


def _silu(x):
    return x * jax.nn.sigmoid(x)


def _softplus(x):
    return jnp.maximum(x, 0.0) + jnp.log1p(jnp.exp(-jnp.abs(x)))


def _dot(a, b, **kw):
    return jnp.dot(a, b, preferred_element_type=F32, **kw)


def _dot_nt(a, b, **kw):
    return lax.dot_general(a, b, (((1,), (1,)), ((), ())), preferred_element_type=F32, **kw)


def _split_bf16(x):
    hi = x.astype(BF16)
    return hi, (x - hi.astype(F32)).astype(BF16)


def _value_with_ones(v_tile, upper_half):
    lane = lax.broadcasted_iota(I32, v_tile.shape, 1)
    v = pltpu.roll(v_tile, HEAD_DIM, 1) if upper_half else v_tile
    return jnp.where(lane < HEAD_DIM, v, 1.0).astype(BF16)


def _softmax_pv(q, keys, vals, biases, extra=None):
    logits = []
    for k, b in zip(keys, biases):
        s = _dot_nt(q, k)
        logits.append(s if b is None else s + b)
    mx = extra
    for s in logits:
        for c in range(s.shape[1] // LANES):
            t = s[:, c * LANES:(c + 1) * LANES]
            mx = t if mx is None else jnp.maximum(mx, t)
    m = jnp.max(mx, axis=-1, keepdims=True)
    acc = None
    for s, v in zip(logits, vals):
        pv = _dot(jnp.exp(s - m).astype(BF16), v)
        acc = pv if acc is None else acc + pv
    den = pltpu.roll(acc, HEAD_DIM, 1)
    if extra is not None:
        den = den + jnp.exp(extra - m)
    return (acc / den)[:, 0:HEAD_DIM]


def _adaln_kernel(c_ref, w_ref, b_ref, o_ref):
    s_hi, s_lo = _split_bf16(_silu(c_ref[...]))
    w_hi, w_lo = _split_bf16(w_ref[0])
    o_ref[0] = _dot(s_hi, w_hi) + (_dot(s_lo, w_hi) + _dot(s_hi, w_lo)) + b_ref[0]


def _adaln(cond8, w_mod, b_mod):
    d = D_MODEL
    return pl.pallas_call(
        _adaln_kernel,
        grid=(DEPTH, 6),
        in_specs=[pl.BlockSpec((8, d), lambda l, n: (0, 0)),
                  pl.BlockSpec((1, d, d), lambda l, n: (l, 0, n)),
                  pl.BlockSpec((1, 1, d), lambda l, n: (l, 0, n))],
        out_specs=pl.BlockSpec((1, 8, d), lambda l, n: (l, 0, n)),
        out_shape=jax.ShapeDtypeStruct((DEPTH, 8, 6 * d), F32),
        compiler_params=_params(("arbitrary", "arbitrary")),
        name="adaln",
    )(cond8, w_mod, b_mod.reshape(DEPTH, 1, 6 * d))


def _token_row_specs(x, tm):
    if not isinstance(x, tuple):
        return [x], [pl.BlockSpec((tm, x.shape[1]), lambda m: (m, 0))]
    nctx = N_CTX // tm
    x_ctx, x_lat = x
    return ([x_ctx, x_lat],
            [pl.BlockSpec((tm, x_ctx.shape[1]), lambda m: (jnp.minimum(m, nctx - 1), 0)),
             pl.BlockSpec((tm, x_lat.shape[1]), lambda m: (jnp.maximum(m - nctx, 0), 0))])


def _token_rows(x_refs, tm):
    if len(x_refs) == 1:
        return x_refs[0][...]
    return jnp.where(pl.program_id(0) < N_CTX // tm, x_refs[0][...], x_refs[1][...])


def _proj_in_kernel(*refs, tm):
    g_ref, mod_ref, w_ref, o_ref = refs[-4:]
    grp = _group_of_block(pl.program_id(0), tm)
    shift = mod_ref[pl.ds(grp, 1), 0:D_MODEL]
    scale = mod_ref[pl.ds(grp, 1), D_MODEL:2 * D_MODEL]
    h = _norm_mod(_token_rows(refs[:-4], tm), g_ref[...], shift, scale).astype(BF16)
    o_ref[...] = _dot(h, w_ref[...])


def _proj_in(x, g, mod_l, w_bf16, tm=512):
    n = w_bf16.shape[1]
    x_arrays, x_specs = _token_row_specs(x, tm)
    return pl.pallas_call(
        functools.partial(_proj_in_kernel, tm=tm),
        grid=(N_ALL // tm,),
        in_specs=x_specs + [pl.BlockSpec((1, D_MODEL), lambda m: (0, 0)),
                            pl.BlockSpec((8, 6 * D_MODEL), lambda m: (0, 0)),
                            pl.BlockSpec((D_MODEL, n), lambda m: (0, 0))],
        out_specs=pl.BlockSpec((tm, n), lambda m: (m, 0)),
        out_shape=jax.ShapeDtypeStruct((N_ALL, n), F32),
        compiler_params=_params(("arbitrary",)),
        name="proj_in",
    )(*x_arrays, g, mod_l, w_bf16)


def _router_affinities(h, wr):
    h_hi, h_lo = _split_bf16(h)
    w_hi, w_lo = _split_bf16(wr)
    logits = _dot_nt(w_hi, h_hi) + (_dot_nt(w_hi, h_lo) + _dot_nt(w_lo, h_hi))
    e = jnp.exp(logits - jnp.max(logits, axis=0, keepdims=True))
    return e / jnp.sum(e, axis=0, keepdims=True)


def _out_proj_kernel(ac_ref, al_ref, bc_ref, bl_ref, w_ref, mod_ref, g_ref, wr_ref, *refs, tm):
    x_refs, (o_ref, h_ref, aff_ref) = refs[:-3], refs[-3:]
    m = pl.program_id(0)
    grp = _group_of_block(m, tm)

    def mod_row(k):
        return mod_ref[pl.ds(grp, 1), k * D_MODEL:(k + 1) * D_MODEL]

    half = D_MODEL // 2
    is_ctx = m < N_CTX // tm
    a = jnp.where(is_ctx, ac_ref[...], al_ref[...]).astype(BF16)
    b = jnp.where(is_ctx, bc_ref[...], bl_ref[...]).astype(BF16)
    acc = _dot(a, w_ref[0:half, :]) + _dot(b, w_ref[half:, :])
    x_new = _token_rows(x_refs, tm) + mod_row(2) * acc
    o_ref[...] = x_new
    h = _norm_mod(x_new, g_ref[...], mod_row(3), mod_row(4))
    h_ref[...] = h
    aff_ref[...] = _router_affinities(h, wr_ref[...])


def _out_proj(a_ctx, a_lat, acol, b_ctx, b_lat, bcol, w_bf16, x, mod_l, g_ffn, w_router_t, tm=512):
    half = D_MODEL // 2
    nctx = N_CTX // tm

    def ctx_map(col):
        return lambda m: (jnp.minimum(m, nctx - 1), col)

    def lat_map(col):
        return lambda m: (jnp.maximum(m - nctx, 0), col)

    x_arrays, x_specs = _token_row_specs(x, tm)
    return pl.pallas_call(
        functools.partial(_out_proj_kernel, tm=tm),
        grid=(N_ALL // tm,),
        in_specs=[pl.BlockSpec((tm, half), ctx_map(acol)),
                  pl.BlockSpec((tm, half), lat_map(acol)),
                  pl.BlockSpec((tm, half), ctx_map(bcol)),
                  pl.BlockSpec((tm, half), lat_map(bcol)),
                  pl.BlockSpec((D_MODEL, D_MODEL), lambda m: (0, 0)),
                  pl.BlockSpec((8, 6 * D_MODEL), lambda m: (0, 0)),
                  pl.BlockSpec((1, D_MODEL), lambda m: (0, 0)),
                  pl.BlockSpec((N_EXPERTS, D_MODEL), lambda m: (0, 0))] + x_specs,
        out_specs=[pl.BlockSpec((tm, D_MODEL), lambda m: (m, 0)),
                   pl.BlockSpec((tm, D_MODEL), lambda m: (m, 0)),
                   pl.BlockSpec((N_EXPERTS, tm), lambda m: (0, m))],
        out_shape=[jax.ShapeDtypeStruct((N_ALL, D_MODEL), F32),
                   jax.ShapeDtypeStruct((N_ALL, D_MODEL), F32),
                   jax.ShapeDtypeStruct((N_EXPERTS, N_ALL), F32)],
        compiler_params=_params(("arbitrary",)),
        name="out_proj",
    )(a_ctx, a_lat, b_ctx, b_lat, w_bf16, mod_l, g_ffn, w_router_t, *x_arrays)


def _ctx_na_kernel(q_ref, k_ref, v_ref, o_ref):
    for h in range(NA_HEADS):
        sl = slice(h * HEAD_DIM, (h + 1) * HEAD_DIM)
        tile = slice((h // 2) * LANES, (h // 2 + 1) * LANES)
        q = (q_ref[:, sl] * ATT_SCALE).astype(BF16)
        v = _value_with_ones(v_ref[:, tile], h % 2 == 1)
        o_ref[:, sl] = _softmax_pv(q, [k_ref[:, sl].astype(BF16)], [v], [None])


def _ctx_na(proj):
    w = NA_WIDTH
    return pl.pallas_call(
        _ctx_na_kernel,
        grid=(BATCH,),
        in_specs=[pl.BlockSpec((SEQ, w), lambda b: (b, 0)),
                  pl.BlockSpec((SEQ, w), lambda b: (b, 1)),
                  pl.BlockSpec((SEQ, w), lambda b: (b, 2))],
        out_specs=pl.BlockSpec((SEQ, w), lambda b: (b, 0)),
        out_shape=jax.ShapeDtypeStruct((N_CTX, w), F32),
        compiler_params=_params(("arbitrary",)),
        name="ctx_na",
    )(proj, proj, proj)


NA_QROWS = 4
NA_KROWS = 12
NA_ROWS = DEC_SEQ // GRID_W
NA_GROUPS = NA_ROWS // NA_QROWS


def _na_key_start(g):
    return jnp.clip(NA_QROWS * g - NA_KR // 2, 0, NA_ROWS - NA_KROWS)


def _na_row_geometry(g):
    start = int(np.clip(NA_QROWS * g - NA_KR // 2, 0, NA_ROWS - NA_KROWS))
    rows = [NA_QROWS * g + qr for qr in range(NA_QROWS)]
    return start, [(r, int(np.clip(r - NA_KR // 2, 0, NA_ROWS - NA_KR))) for r in rows]


def _na_fill_bias(ct_ref, bias_scr, g):
    start, rows = _na_row_geometry(g)
    masked = jnp.full((GRID_W, GRID_W), NEG_BIG, F32)
    for i in range(NA_HEADS):
        for qr, (r, rs) in enumerate(rows):
            for kr in range(NA_KROWS):
                keyrow = start + kr
                inside = rs <= keyrow < rs + NA_KR
                tile = ct_ref[i, keyrow - r + NA_KR - 1] if inside else masked
                bias_scr[i, qr * GRID_W:(qr + 1) * GRID_W, kr * GRID_W:(kr + 1) * GRID_W] = tile


def _lat_na_kernel(q_ref, k_ref, v_ref, ck_ref, cv_ref, ct_ref, o_ref, bias_scr, k_scr, v_scr, ck_scr, cv_scr):
    g = pl.program_id(1)
    for g_build in (0, 1, NA_GROUPS - 1):
        @pl.when(g == g_build)
        def _(g_build=g_build):
            _na_fill_bias(ct_ref, bias_scr, g_build)

    @pl.when(g == 0)
    def _():
        for i in range(NA_HEADS):
            sl = slice(i * HEAD_DIM, (i + 1) * HEAD_DIM)
            tile = slice((i // 2) * LANES, (i // 2 + 1) * LANES)
            k_scr[i] = k_ref[:, sl].astype(BF16)
            v_scr[i] = _value_with_ones(v_ref[:, tile], i % 2 == 1)
            ck_scr[i] = ck_ref[:, sl].astype(BF16)
            cv_scr[i] = _value_with_ones(cv_ref[:, tile], i % 2 == 1)

    start = pl.multiple_of(_na_key_start(g) * GRID_W, GRID_W)
    nk = NA_KROWS * GRID_W
    for i in range(NA_HEADS):
        sl = slice(i * HEAD_DIM, (i + 1) * HEAD_DIM)
        q = (q_ref[:, sl] * ATT_SCALE).astype(BF16)
        o_ref[:, sl] = _softmax_pv(q, [k_scr[i, pl.ds(start, nk), :], ck_scr[i]],
                                   [v_scr[i, pl.ds(start, nk), :], cv_scr[i]], [bias_scr[i], None])


def _na_col_table(rpb):
    w = np.arange(GRID_W)[:, None]
    cc = np.arange(GRID_W)[None, :]
    cs = np.clip(w - NA_KC // 2, 0, GRID_W - NA_KC)
    valid = (cc >= cs) & (cc < cs + NA_KC)
    dc = cc - w + NA_KC - 1
    onehot = ((dc[..., None] == np.arange(2 * NA_KC - 1)) & valid[..., None]).astype(np.float32)
    ct = jnp.einsum('hrd,wcd->hrwc', rpb, jnp.asarray(onehot), precision=HIGHEST)
    return jnp.where(jnp.asarray(valid)[None, None], ct, NEG_BIG)


def _lat_na(proj, ck, cv, col_tab):
    nq = NA_QROWS * GRID_W
    nk = NA_KROWS * GRID_W
    lat_q0 = N_CTX // nq
    lat_b0 = N_CTX // DEC_SEQ
    w = NA_WIDTH
    return pl.pallas_call(
        _lat_na_kernel,
        grid=(DEC_BATCH, NA_GROUPS),
        in_specs=[pl.BlockSpec((nq, w), lambda b, g: (lat_q0 + b * NA_GROUPS + g, 0)),
                  pl.BlockSpec((DEC_SEQ, w), lambda b, g: (lat_b0 + b, 1)),
                  pl.BlockSpec((DEC_SEQ, w), lambda b, g: (lat_b0 + b, 2)),
                  pl.BlockSpec((None, PAST_LEN, w), lambda b, g: (b, 0, 0)),
                  pl.BlockSpec((None, PAST_LEN, w), lambda b, g: (b, 0, 0)),
                  pl.BlockSpec((NA_HEADS, 2 * NA_KR - 1, GRID_W, GRID_W), lambda b, g: (0, 0, 0, 0))],
        out_specs=pl.BlockSpec((nq, w), lambda b, g: (b * NA_GROUPS + g, 0)),
        out_shape=jax.ShapeDtypeStruct((N_LAT, w), F32),
        scratch_shapes=[pltpu.VMEM((NA_HEADS, nq, nk), F32),
                        pltpu.VMEM((NA_HEADS, DEC_SEQ, HEAD_DIM), BF16),
                        pltpu.VMEM((NA_HEADS, DEC_SEQ, LANES), BF16),
                        pltpu.VMEM((NA_HEADS, PAST_LEN, HEAD_DIM), BF16),
                        pltpu.VMEM((NA_HEADS, PAST_LEN, LANES), BF16)],
        compiler_params=_params(("arbitrary", "arbitrary")),
        name="lat_na",
    )(proj, proj, proj, ck, cv, col_tab)


def _ssd_kernel(*refs, seq, has_h0, want_state):
    it = iter(refs)
    z_ref, xbc_ref, dt_ref = next(it), next(it), next(it)
    h0_ref = next(it) if has_h0 else None
    cw_ref, cb_ref, a_ref, dtb_ref, dsk_ref, ng_ref = (next(it) for _ in range(6))
    y_ref = next(it)
    st_ref = next(it) if want_state else None
    xc_scr, y_scr, ht_scr = next(it), next(it), next(it)

    nc = seq // SSD_CHUNK
    ch = SSD_CHUNK
    row = lax.broadcasted_iota(I32, (ch, ch), 0)
    col = lax.broadcasted_iota(I32, (ch, ch), 1)
    erow = lax.broadcasted_iota(I32, (LANES, SSD_D_INNER), 0)
    ecol = lax.broadcasted_iota(I32, (LANES, SSD_D_INNER), 1) // SSD_HEAD_DIM
    srow = lax.broadcasted_iota(I32, (LANES, SSD_HEADS * LANES), 0)
    scol = lax.broadcasted_iota(I32, (LANES, SSD_HEADS * LANES), 1) // LANES
    lane128 = lax.broadcasted_iota(I32, (1, LANES), 1)
    cbias = cb_ref[...]

    def conv_chunk(c, carry):
        base = pl.multiple_of(c * ch, ch)
        cur = xbc_ref[pl.ds(base, ch), :]
        pbase = pl.multiple_of(jnp.maximum(base - 8, 0), 8)
        nbase = pl.multiple_of(jnp.minimum(base + ch, seq - 8), 8)
        prev = jnp.where(c > 0, xbc_ref[pl.ds(pbase, 8), :], 0.0)
        nxt = jnp.where(c < nc - 1, xbc_ref[pl.ds(nbase, 8), :], 0.0)
        win = jnp.concatenate([prev, cur, nxt], axis=0)
        acc = jnp.broadcast_to(cbias, (ch, SSD_CONV_CH))
        pad = SSD_CONV_W // 2
        for k in range(SSD_CONV_W):
            off = 8 - pad + k
            acc = acc + win[off:off + ch, :] * cw_ref[k:k + 1, :]
        xc_scr[pl.ds(base, ch), :] = _silu(acc)
        return carry

    lax.fori_loop(0, nc, conv_chunk, 0)

    def run_direction(d):
        lane0 = d * SSD_HEADS
        expand = jnp.where(erow == ecol + lane0, 1.0, 0.0).astype(BF16)
        spread = jnp.where(srow == scol + lane0, 1.0, 0.0).astype(BF16)
        lmask = (col <= row) if d == 0 else (col >= row)
        tri = jnp.where(lmask, 1.0, 0.0).astype(BF16)

        def dot_split(lhs01, x, lhs_first):
            hi = x.astype(BF16)
            lo = (x - hi.astype(F32)).astype(BF16)
            if lhs_first:
                return _dot(lhs01, hi) + _dot(lhs01, lo)
            return _dot(hi, lhs01) + _dot(lo, lhs01)
        if has_h0:
            ht_scr[...] = h0_ref[d]
        else:
            ht_scr[...] = jnp.zeros((SSD_STATE, SSD_D_INNER), F32)

        def chunk(step, carry):
            c = step if d == 0 else nc - 1 - step
            base = pl.multiple_of(c * ch, ch)
            xs = xc_scr[pl.ds(base, ch), 0:SSD_D_INNER]
            dt = _softplus(dt_ref[pl.ds(base, ch), :] + dtb_ref[...])
            a = dt * a_ref[...]
            cum = dot_split(tri, a, True)
            tot = cum[ch - 1:ch, :] if d == 0 else cum[0:1, :]
            cum_t = cum.T
            dt_x = dot_split(expand, dt, False)
            cum_x = dot_split(expand, cum, False)
            cum_w = dot_split(spread, cum, False)
            tot_x = dot_split(expand, jnp.broadcast_to(tot, (8, LANES)), False)[0:1, :]
            xt = xs * dt_x
            xd = (xt * jnp.exp(tot_x - cum_x)).astype(BF16)
            ecum_x = jnp.exp(cum_x)
            y_parts = []
            for grp in range(SSD_GROUPS):
                bsl = slice(SSD_D_INNER + grp * SSD_STATE, SSD_D_INNER + (grp + 1) * SSD_STATE)
                csl = slice(SSD_D_INNER + SSD_BC + grp * SSD_STATE, SSD_D_INNER + SSD_BC + (grp + 1) * SSD_STATE)
                b_f = xc_scr[pl.ds(base, ch), bsl]
                b_g = b_f.astype(BF16)
                c_g = xc_scr[pl.ds(base, ch), csl].astype(BF16)
                cb = _dot_nt(c_g, b_g)
                hsl = slice(grp * 4 * SSD_HEAD_DIM, (grp + 1) * 4 * SSD_HEAD_DIM)
                ht_g = ht_scr[:, hsl]
                y_off = _dot(c_g, ht_g.astype(BF16))
                for pair in range(2):
                    h0 = grp * 4 + 2 * pair
                    decayed = []
                    for head in (h0, h0 + 1):
                        cum_col = cum_w[:, head * LANES:(head + 1) * LANES]
                        cum_row = cum_t[lane0 + head:lane0 + head + 1, :]
                        ldec = jnp.exp(jnp.where(lmask, cum_col - cum_row, NEG_BIG))
                        decayed.append((cb * ldec).astype(BF16))
                    tile = slice((h0 // 2) * LANES, (h0 // 2 + 1) * LANES)
                    x_pair = xt[:, tile]
                    x_diag = jnp.concatenate([jnp.where(lane128 < SSD_HEAD_DIM, x_pair, 0.0),
                                              jnp.where(lane128 >= SSD_HEAD_DIM, x_pair, 0.0)], axis=0)
                    y_d = _dot(jnp.concatenate(decayed, axis=1), x_diag.astype(BF16))
                    y_parts.append(y_d + y_off[:, pair * LANES:(pair + 1) * LANES] * ecum_x[:, tile])
                ht_scr[:, hsl] = ht_g * jnp.exp(tot_x[:, hsl]) + _dot(b_f.T.astype(BF16), xd[:, hsl])
            y = jnp.concatenate(y_parts, axis=1)
            if d == 0:
                y_scr[pl.ds(base, ch), :] = y
            else:
                y = y + y_scr[pl.ds(base, ch), :] + dsk_ref[...] * xs
                u = y * _silu(z_ref[pl.ds(base, ch), :])
                ms = jnp.mean(u * u, axis=-1, keepdims=True)
                y_ref[pl.ds(base, ch), :] = u * lax.rsqrt(ms + NORM_EPS) * ng_ref[...]
            return carry

        lax.fori_loop(0, nc, chunk, 0, unroll=2)
        if want_state:
            st_ref[d] = ht_scr[...]

    run_direction(0)
    run_direction(1)


def _ssd(proj, h0t, consts, *, seq, nbatch, row_blk0, want_state):
    has_h0 = h0t is not None
    in_specs = [pl.BlockSpec((seq, SSD_D_INNER), lambda b: (row_blk0 + b, 3)),
                pl.BlockSpec((seq, SSD_CONV_CH), lambda b: (row_blk0 + b, 2)),
                pl.BlockSpec((seq, LANES), lambda b: (row_blk0 + b, 24))]
    args = [proj, proj, proj]
    if has_h0:
        in_specs.append(pl.BlockSpec((None, 2, SSD_STATE, SSD_D_INNER), lambda b: (b, 0, 0, 0)))
        args.append(h0t)
    for cst in consts:
        in_specs.append(pl.BlockSpec(cst.shape, lambda b: (0, 0)))
        args.append(cst)
    out_specs = [pl.BlockSpec((seq, SSD_D_INNER), lambda b: (b, 0))]
    out_shape = [jax.ShapeDtypeStruct((nbatch * seq, SSD_D_INNER), F32)]
    if want_state:
        out_specs.append(pl.BlockSpec((None, 2, SSD_STATE, SSD_D_INNER), lambda b: (b, 0, 0, 0)))
        out_shape.append(jax.ShapeDtypeStruct((nbatch, 2, SSD_STATE, SSD_D_INNER), F32))
    res = pl.pallas_call(
        functools.partial(_ssd_kernel, seq=seq, has_h0=has_h0, want_state=want_state),
        grid=(nbatch,),
        in_specs=in_specs,
        out_specs=out_specs,
        out_shape=out_shape,
        scratch_shapes=[pltpu.VMEM((seq, SSD_CONV_CH), F32),
                        pltpu.VMEM((seq, SSD_D_INNER), F32),
                        pltpu.VMEM((SSD_STATE, SSD_D_INNER), F32)],
        compiler_params=_params(("arbitrary",)),
        name="ssd_seq%d" % seq,
    )(*args)
    return res if want_state else (res[0], None)


def _ctx_swa_kernel(q_ref, k_ref, v_ref, sink_ref, o_ref):
    grp = SWA_HEADS // SWA_KV_HEADS
    for kv in range(SWA_KV_HEADS):
        ksl = slice(kv * HEAD_DIM, (kv + 1) * HEAD_DIM)
        tile = slice((kv // 2) * LANES, (kv // 2 + 1) * LANES)
        k = k_ref[:, ksl].astype(BF16)
        v = _value_with_ones(v_ref[:, tile], kv % 2 == 1)
        q4 = jnp.concatenate(
            [q_ref[:, (kv * grp + j) * HEAD_DIM:(kv * grp + j + 1) * HEAD_DIM] for j in range(grp)],
            axis=0)
        o = _softmax_pv((q4 * ATT_SCALE).astype(BF16), [k], [v], [None], extra=sink_ref[kv])
        for j in range(grp):
            o_ref[:, (kv * grp + j) * HEAD_DIM:(kv * grp + j + 1) * HEAD_DIM] = o[j * SEQ:(j + 1) * SEQ]


def _ctx_swa(proj, sink_col):
    return pl.pallas_call(
        _ctx_swa_kernel,
        grid=(BATCH,),
        in_specs=[pl.BlockSpec((SEQ, SWA_Q), lambda b: (b, 0)),
                  pl.BlockSpec((SEQ, SWA_KV), lambda b: (b, 4)),
                  pl.BlockSpec((SEQ, SWA_KV), lambda b: (b, 5)),
                  pl.BlockSpec(sink_col.shape, lambda b: (0, 0, 0))],
        out_specs=pl.BlockSpec((SEQ, SWA_Q), lambda b: (b, 0)),
        out_shape=jax.ShapeDtypeStruct((N_CTX, SWA_Q), F32),
        compiler_params=_params(("arbitrary",)),
        name="ctx_swa",
    )(proj, proj, proj, sink_col)


SWA_NLOC = 3 * SWA_BLOCK


def _rope(x, cos, sin_signed, first):
    n = x.shape[-1]
    partner = jnp.where(first, pltpu.roll(x, n - 16, 1), pltpu.roll(x, 16, 1))
    return x * cos + partner * sin_signed


def _lat_swa_kernel(q_ref, k_ref, v_ref, ck_ref, cv_ref, cosk_ref, sink_k_ref, cosq_ref, sinq_ref, sink_ref,
                    o_ref, kr_scr, v_scr, ck_scr, cv_scr):
    i = pl.program_id(1)
    grp = SWA_HEADS // SWA_KV_HEADS
    lane = lax.broadcasted_iota(I32, (1, SWA_KV), 1)
    first = (lane % 32) < 16

    @pl.when(i == 0)
    def _():
        kr = _rope(k_ref[...], cosk_ref[...], sink_k_ref[...], first)
        for kv in range(SWA_KV_HEADS):
            ksl = slice(kv * HEAD_DIM, (kv + 1) * HEAD_DIM)
            tile = slice((kv // 2) * LANES, (kv // 2 + 1) * LANES)
            kr_scr[kv] = kr[:, ksl].astype(BF16)
            v_scr[kv] = _value_with_ones(v_ref[:, tile], kv % 2 == 1)
            ck_scr[kv] = ck_ref[:, ksl].astype(BF16)
            cv_scr[kv] = _value_with_ones(cv_ref[:, tile], kv % 2 == 1)

    kstart = pl.multiple_of(jnp.clip((i - 1) * SWA_BLOCK, 0, DEC_SEQ - SWA_NLOC), SWA_BLOCK)
    qpos = i * SWA_BLOCK + lax.broadcasted_iota(I32, (SWA_BLOCK, SWA_NLOC), 0)
    kpos = kstart + lax.broadcasted_iota(I32, (SWA_BLOCK, SWA_NLOC), 1)
    mask1 = jnp.where(jnp.abs(qpos - kpos) <= SWA_WINDOW, 0.0, NEG_BIG)
    mask = jnp.concatenate([mask1] * grp, axis=0)
    cosq = cosq_ref[...]
    sinq = sinq_ref[...]
    for kv in range(SWA_KV_HEADS):
        qr = _rope(q_ref[:, kv * SWA_KV:(kv + 1) * SWA_KV], cosq, sinq, first) * ATT_SCALE
        q4 = jnp.concatenate([qr[:, j * HEAD_DIM:(j + 1) * HEAD_DIM] for j in range(grp)], axis=0).astype(BF16)
        o = _softmax_pv(q4, [kr_scr[kv, pl.ds(kstart, SWA_NLOC), :], ck_scr[kv]],
                        [v_scr[kv, pl.ds(kstart, SWA_NLOC), :], cv_scr[kv]], [mask, None], extra=sink_ref[kv])
        for j in range(grp):
            o_ref[:, (kv * grp + j) * HEAD_DIM:(kv * grp + j + 1) * HEAD_DIM] = o[j * SWA_BLOCK:(j + 1) * SWA_BLOCK]


def _rope_tables():
    half = HEAD_DIM // 2
    quarter = half // 2
    pos = jnp.arange(DEC_SEQ)
    inv_freq = 1.0 / (ROPE_BASE ** (jnp.arange(quarter, dtype=F32) * 2.0 / half))
    d = np.arange(HEAD_DIM)
    use_col = jnp.asarray(d >= half)
    p = jnp.where(use_col[None, :], (pos % GRID_W)[:, None], (pos // GRID_W)[:, None]).astype(F32)
    ang = p * inv_freq[d % quarter][None, :]
    sign = jnp.asarray(np.where((d % half) < quarter, -1.0, 1.0), F32)
    cos = jnp.cos(ang)
    sin_signed = jnp.sin(ang) * sign[None, :]
    reps = SWA_KV // HEAD_DIM
    return jnp.tile(cos, (1, reps)), jnp.tile(sin_signed, (1, reps))


def _lat_swa(proj, ck, cv, cos_t, sin_t, sink_col):
    nb = DEC_SEQ // SWA_BLOCK
    q0 = N_CTX // SWA_BLOCK
    b0 = N_CTX // DEC_SEQ
    return pl.pallas_call(
        _lat_swa_kernel,
        grid=(DEC_BATCH, nb),
        in_specs=[pl.BlockSpec((SWA_BLOCK, SWA_Q), lambda b, i: (q0 + b * nb + i, 0)),
                  pl.BlockSpec((DEC_SEQ, SWA_KV), lambda b, i: (b0 + b, 4)),
                  pl.BlockSpec((DEC_SEQ, SWA_KV), lambda b, i: (b0 + b, 5)),
                  pl.BlockSpec((None, PAST_LEN, SWA_KV), lambda b, i: (b, 0, 0)),
                  pl.BlockSpec((None, PAST_LEN, SWA_KV), lambda b, i: (b, 0, 0)),
                  pl.BlockSpec((DEC_SEQ, SWA_KV), lambda b, i: (0, 0)),
                  pl.BlockSpec((DEC_SEQ, SWA_KV), lambda b, i: (0, 0)),
                  pl.BlockSpec((SWA_BLOCK, SWA_KV), lambda b, i: (i, 0)),
                  pl.BlockSpec((SWA_BLOCK, SWA_KV), lambda b, i: (i, 0)),
                  pl.BlockSpec(sink_col.shape, lambda b, i: (0, 0, 0))],
        out_specs=pl.BlockSpec((SWA_BLOCK, SWA_Q), lambda b, i: (b * nb + i, 0)),
        out_shape=jax.ShapeDtypeStruct((N_LAT, SWA_Q), F32),
        scratch_shapes=[pltpu.VMEM((SWA_KV_HEADS, DEC_SEQ, HEAD_DIM), BF16),
                        pltpu.VMEM((SWA_KV_HEADS, DEC_SEQ, LANES), BF16),
                        pltpu.VMEM((SWA_KV_HEADS, PAST_LEN, HEAD_DIM), BF16),
                        pltpu.VMEM((SWA_KV_HEADS, PAST_LEN, LANES), BF16)],
        compiler_params=_params(("arbitrary", "arbitrary")),
        name="lat_swa",
    )(proj, proj, proj, ck, cv, cos_t, sin_t, cos_t, sin_t, sink_col)


ROUTE_NB = N_ALL // LANES
ROUTE_SETS = ((0, N_CTX // LANES, CAP_CTX, 0), (N_CTX // LANES, N_LAT // LANES, CAP_LAT, CAP_CTX))


def _route_kernel(aff_ref, idx_ref, gate_ref, st_ref, pos_ref, t_scr, cend_scr):
    ne = N_EXPERTS
    r = lax.broadcasted_iota(I32, (LANES, LANES), 0)
    c = lax.broadcasted_iota(I32, (LANES, LANES), 1)
    upper = jnp.where(r <= c, 1.0, 0.0).astype(BF16)
    lower_incl = jnp.where(c <= r, 1.0, 0.0).astype(BF16)
    lower_strict = jnp.where(c < r, 1.0, 0.0).astype(BF16)
    lane = lax.broadcasted_iota(I32, (8, LANES), 1)

    def cumsum_tokens(x):
        rowc = _dot(x.astype(BF16), upper)
        tot = jnp.broadcast_to(rowc[:, LANES - 1:LANES], (LANES, LANES))
        return rowc + _dot(lower_strict, tot.astype(BF16)), tot

    for set_id, (b0, nb, cap, slot0) in enumerate(ROUTE_SETS):
        capf = float(cap)
        npad = LANES - nb

        def count_ge(e, cand, b0=b0, nb=nb):
            hit = jnp.where(aff_ref[e, b0:b0 + nb, :] >= cand, 1.0, 0.0)
            return jnp.sum(jnp.sum(hit, axis=1, keepdims=True), axis=0, keepdims=True)

        def bit_step(i, ts, count_ge=count_ge, capf=capf):
            bit = jnp.left_shift(jnp.int32(1), 30 - i)
            out = []
            for e in range(ne):
                cand = ts[e] | bit
                keep = count_ge(e, lax.bitcast_convert_type(cand, F32)) >= capf
                out.append(jnp.where(keep, cand, ts[e]))
            return tuple(out)

        ts = lax.fori_loop(0, 31, bit_step, tuple(jnp.zeros((1, 1), I32) for _ in range(ne)))
        for e in range(ne):
            t_scr[e] = jnp.broadcast_to(lax.bitcast_convert_type(ts[e], F32), (8, LANES))

        slot = lax.broadcasted_iota(I32, (LANES, cap), 1).astype(F32)
        sub = lax.broadcasted_iota(I32, (LANES, cap), 0).astype(F32)

        def per_expert(e, carry, set_id=set_id, b0=b0, nb=nb, cap=cap, slot0=slot0, capf=capf, npad=npad,
                       slot=slot, sub=sub):
            a = aff_ref[e, b0:b0 + nb, :]
            if npad:
                a = jnp.concatenate([a, jnp.full((npad, LANES), -1.0, F32)], axis=0)
            thr = t_scr[e][0:1, 0:1]
            gt = a > thr
            eq = a == thr
            gtf = jnp.where(gt, 1.0, 0.0)
            eqf = jnp.where(eq, 1.0, 0.0)
            need = capf - jnp.sum(jnp.sum(gtf, axis=1, keepdims=True), axis=0, keepdims=True)
            eq_incl, _ = cumsum_tokens(eqf)
            sel = gt | (eq & (eq_incl - eqf < need))
            self = jnp.where(sel, 1.0, 0.0)
            incl, tot = cumsum_tokens(self)
            cend = _dot(lower_incl, tot.astype(BF16))
            blk = jnp.sum(jnp.where(cend[:, 0:1] <= slot, 1.0, 0.0), axis=0, keepdims=True)
            onehot = jnp.where(sub == blk, 1.0, 0.0)
            hi = jnp.floor(incl * (1.0 / LANES))
            lo = incl - hi * LANES
            a_pos = jnp.maximum(a, 0.0)
            a1 = a_pos.astype(BF16)
            r1 = a_pos - a1.astype(F32)
            a2 = r1.astype(BF16)
            a3 = (r1 - a2.astype(F32)).astype(BF16)
            stacked = jnp.concatenate([hi.T.astype(BF16), lo.T.astype(BF16), a1.T, a2.T, a3.T], axis=0)
            picked = _dot(stacked, onehot.astype(BF16))
            inc_s = picked[0:LANES] * LANES + picked[LANES:2 * LANES]
            within = jnp.sum(jnp.where(inc_s <= slot, 1.0, 0.0), axis=0, keepdims=True)
            aff_s = picked[2 * LANES:3 * LANES] + picked[3 * LANES:4 * LANES] + picked[4 * LANES:5 * LANES]
            gate = jnp.sum(jnp.where(sub == within, aff_s, 0.0), axis=0, keepdims=True)
            idx_ref[e, :, slot0:slot0 + cap] = ((blk + float(b0)) * LANES + within).astype(I32)
            gate_ref[e, :, slot0:slot0 + cap] = gate
            pos_ref[e, b0:b0 + nb, :] = jnp.where(sel, incl - 1.0 + float(slot0), -1.0)[0:nb, :]
            ends = pltpu.roll(cend.T[0:8, :] + float(slot0), b0 + 1, 1)
            if set_id == 0:
                cend_scr[e] = jnp.where(lane == 0, 0.0, ends)
            else:
                st_ref[e] = jnp.where(lane <= b0, cend_scr[e], ends).astype(I32)
            return carry

        lax.fori_loop(0, ne, per_expert, 0, unroll=2)


def _route(aff3):
    ne = N_EXPERTS
    return pl.pallas_call(
        _route_kernel,
        grid=(1,),
        in_specs=[pl.BlockSpec((ne, ROUTE_NB, LANES), lambda i: (0, 0, 0))],
        out_specs=[pl.BlockSpec((ne, 1, CAP_ALL), lambda i: (0, 0, 0)),
                   pl.BlockSpec((ne, 1, CAP_ALL), lambda i: (0, 0, 0)),
                   pl.BlockSpec((ne, 8, LANES), lambda i: (0, 0, 0)),
                   pl.BlockSpec((ne, ROUTE_NB, LANES), lambda i: (0, 0, 0))],
        out_shape=[jax.ShapeDtypeStruct((ne, 1, CAP_ALL), I32),
                   jax.ShapeDtypeStruct((ne, 1, CAP_ALL), F32),
                   jax.ShapeDtypeStruct((ne, 8, LANES), I32),
                   jax.ShapeDtypeStruct((ne, ROUTE_NB, LANES), F32)],
        scratch_shapes=[pltpu.VMEM((ne, 8, LANES), F32),
                        pltpu.VMEM((ne, 8, LANES), F32)],
        compiler_params=_params(("arbitrary",)),
        name="route",
    )(aff3)


FF_TILE = 512
COMB_PIECE = 32
CAP_PAD = CAP_ALL + COMB_PIECE


def _moe_ffn_kernel(idx_ref, h_hbm, gate_ref, wg_ref, wu_ref, wd_ref, o_ref, land, xe, acc, g_scr, u_scr, sem):
    e = pl.program_id(0)
    f = pl.program_id(1)
    nf = EXPERT_FF // FF_TILE
    rps = CAP_ALL // nf

    def row_copy(expert, part, s):
        tok = idx_ref[expert * CAP_ALL + part * rps + s]
        return pltpu.make_async_copy(h_hbm.at[pl.ds(tok, 1), :], land.at[part, pl.ds(s, 1), :], sem)

    def wait_rows():
        for part in range(nf):
            pltpu.make_async_copy(h_hbm.at[pl.ds(0, rps), :], land.at[part], sem).wait()

    @pl.when(f == 0)
    def _():
        @pl.when(e == 0)
        def _():
            for part in range(nf):
                def issue(s, carry, part=part):
                    row_copy(0, part, s).start()
                    return carry

                lax.fori_loop(0, rps, issue, 0)

        wait_rows()
        for part in range(nf):
            xe[part * rps:(part + 1) * rps, :] = land[part].astype(BF16)
        acc[...] = jnp.zeros_like(acc)

    nxt = (e + 1) % N_EXPERTS
    third = rps // 3

    def request_rows(lo, hi):
        for s in range(lo, hi):
            row_copy(nxt, f, s).start(priority=s % 2)

    @pl.when(f >= 0)
    def _():
        request_rows(0, third)
        g_scr[...] = _dot(xe[...], wg_ref[...].astype(BF16))

    @pl.when(e >= 0)
    def _():
        request_rows(third, 2 * third)
        u_scr[...] = _dot(xe[...], wu_ref[...].astype(BF16))

    request_rows(2 * third, rps)
    hid = (_silu(g_scr[...]) * u_scr[...]).astype(BF16)
    acc[...] += _dot(hid, wd_ref[...].astype(BF16))

    @pl.when(f == nf - 1)
    def _():
        y = acc[...] * gate_ref[...]
        hi = y.astype(BF16)
        o_ref[0:CAP_ALL, 0:D_MODEL] = hi
        o_ref[0:CAP_ALL, D_MODEL:2 * D_MODEL] = (y - hi.astype(F32)).astype(BF16)
        o_ref[CAP_ALL:CAP_PAD, :] = jnp.zeros((CAP_PAD - CAP_ALL, 2 * D_MODEL), BF16)

        @pl.when(e == N_EXPERTS - 1)
        def _():
            wait_rows()


def _moe_ffn(layer, h, idx_flat, gate_col, w_gate, w_up, w_down):
    nf = EXPERT_FF // FF_TILE
    grid_spec = pltpu.PrefetchScalarGridSpec(
        num_scalar_prefetch=1,
        grid=(N_EXPERTS, nf),
        in_specs=[pl.BlockSpec(memory_space=pl.ANY),
                  pl.BlockSpec((None, CAP_ALL, 1), lambda e, f, idx: (e, 0, 0)),
                  pl.BlockSpec((None, None, D_MODEL, FF_TILE), lambda e, f, idx: (layer, e, 0, f)),
                  pl.BlockSpec((None, None, D_MODEL, FF_TILE), lambda e, f, idx: (layer, e, 0, f)),
                  pl.BlockSpec((None, None, FF_TILE, D_MODEL), lambda e, f, idx: (layer, e, f, 0))],
        out_specs=pl.BlockSpec((None, CAP_PAD, 2 * D_MODEL), lambda e, f, idx: (e, 0, 0)),
        scratch_shapes=[pltpu.VMEM((nf, CAP_ALL // nf, D_MODEL), F32),
                        pltpu.VMEM((CAP_ALL, D_MODEL), BF16),
                        pltpu.VMEM((CAP_ALL, D_MODEL), F32),
                        pltpu.VMEM((CAP_ALL, FF_TILE), F32),
                        pltpu.VMEM((CAP_ALL, FF_TILE), F32),
                        pltpu.SemaphoreType.DMA(())],
    )
    return pl.pallas_call(
        _moe_ffn_kernel,
        grid_spec=grid_spec,
        out_shape=jax.ShapeDtypeStruct((N_EXPERTS, CAP_PAD, 2 * D_MODEL), BF16),
        compiler_params=_params(("arbitrary", "arbitrary")),
        name="moe_ffn",
    )(idx_flat, h, gate_col, w_gate, w_up, w_down)


COMB_TM = 512
COMB_NB = N_ALL // COMB_TM
COMB_HALF = N_EXPERTS // 2
COMB_CHUNK = 256
COMB_ALIGN = 16
COMB_WAIT_GROUP = 32
COMB_STAGE = -(-(COMB_HALF * (COMB_TM + COMB_PIECE + COMB_ALIGN)) // COMB_CHUNK) * COMB_CHUNK


def _combine_kernel(st_ref, ye_hbm, pos_ref, x_ref, mod_ref, o_ref, stage, acc, sem):
    b = pl.program_id(0)
    nst = LANES
    per = COMB_TM // LANES

    def layout(blk, half):
        out = []
        off = jnp.int32(0)
        for j in range(COMB_HALF):
            e = half * COMB_HALF + j
            s0 = st_ref[e * nst + per * blk]
            s1 = st_ref[e * nst + per * blk + per]
            a0 = (s0 // COMB_ALIGN) * COMB_ALIGN
            npc = jnp.where(s1 > s0, (s1 - a0 + COMB_PIECE - 1) // COMB_PIECE, 0)
            out.append((e, a0, off, npc))
            off = off + npc * COMB_PIECE
        return out, off // COMB_PIECE

    def piece_copy(e, src_row, buf, dst_row, pieces=1):
        rows = pieces * COMB_PIECE
        return pltpu.make_async_copy(ye_hbm.at[e, pl.ds(src_row, rows), :],
                                     stage.at[buf, pl.ds(dst_row, rows), :], sem.at[buf])

    def issue_unit(blk, half):
        lay, _ = layout(blk, half)
        for e, a0, off, npc in lay:
            def issue(k, carry, e=e, a0=a0, off=off):
                piece_copy(e, pl.multiple_of(a0 + k * COMB_PIECE, COMB_ALIGN), half,
                           pl.multiple_of(off + k * COMB_PIECE, COMB_ALIGN)).start()
                return carry

            lax.fori_loop(0, npc, issue, 0)

    @pl.when(b == 0)
    def _():
        stage[...] = jnp.zeros_like(stage)
        issue_unit(0, 0)

    acc[...] = jnp.zeros_like(acc)
    pos = pos_ref[...]
    row_t = lax.broadcasted_iota(I32, (COMB_CHUNK, COMB_TM), 0).astype(F32)
    row_l = lax.broadcasted_iota(I32, (COMB_CHUNK, LANES), 0).astype(F32)
    lane = lax.broadcasted_iota(I32, (1, LANES), 1)
    for half in range(2):
        lay, npieces = layout(b, half)

        def wait_big(k, carry, half=half):
            piece_copy(0, 0, half, 0, COMB_WAIT_GROUP).wait()
            return carry

        lax.fori_loop(0, npieces // COMB_WAIT_GROUP, wait_big, 0)
        small = COMB_WAIT_GROUP // 2
        while small >= 1:
            @pl.when((npieces & small) != 0)
            def _(small=small, half=half):
                piece_copy(0, 0, half, 0, small).wait()

            small //= 2
        if half == 0:
            issue_unit(b, 1)
        else:
            @pl.when(b + 1 < COMB_NB)
            def _():
                issue_unit(b + 1, 0)

        srows = []
        first = jnp.full((1, LANES), 1e9, F32)
        last = jnp.zeros((1, LANES), F32)
        for j, (e, a0, off, npc) in enumerate(lay):
            p = pos[e:e + 1, :]
            srows.append(jnp.where(p >= 0.0, p + (off - a0).astype(F32), -1.0))
            first = jnp.where(lane == j, off.astype(F32), first)
            last = jnp.where(lane == j, (off + npc * COMB_PIECE).astype(F32), last)
        srow = jnp.concatenate(srows + [jnp.zeros((LANES - COMB_HALF, COMB_TM), F32)], axis=0)
        srow_hi = jnp.floor(srow * (1.0 / 64.0))
        srow_lo = (srow - 64.0 * srow_hi).astype(BF16)
        srow_hi = srow_hi.astype(BF16)

        def select_rows(ci, srow_hi=srow_hi, srow_lo=srow_lo, first=first, last=last):
            r0f = (ci * COMB_CHUNK).astype(F32)
            rid = row_l + r0f
            owner = jnp.where((rid >= first) & (rid < last), 1.0, 0.0).astype(BF16)
            want = 64.0 * _dot(owner, srow_hi) + _dot(owner, srow_lo)
            return jnp.where(want == row_t + r0f, 1.0, 0.0).T.astype(BF16)

        def chunk(ci, sel_t, half=half, select_rows=select_rows):
            sel_next = select_rows(ci + 1)
            r0 = pl.multiple_of(ci * COMB_CHUNK, COMB_CHUNK)
            acc[...] += (_dot(sel_t, stage[half, pl.ds(r0, COMB_CHUNK), 0:D_MODEL])
                         + _dot(sel_t, stage[half, pl.ds(r0, COMB_CHUNK), D_MODEL:2 * D_MODEL]))
            return sel_next

        nchunks = (npieces * COMB_PIECE + COMB_CHUNK - 1) // COMB_CHUNK
        lax.fori_loop(0, nchunks, chunk, select_rows(jnp.int32(0)))

    grp = _group_of_block(b, COMB_TM)
    gate = mod_ref[pl.ds(grp, 1), 5 * D_MODEL:6 * D_MODEL]
    o_ref[...] = x_ref[...] + gate * acc[...]


def _combine(ye, starts_flat, pos, x, mod_l):
    grid_spec = pltpu.PrefetchScalarGridSpec(
        num_scalar_prefetch=1,
        grid=(COMB_NB,),
        in_specs=[pl.BlockSpec(memory_space=pl.ANY),
                  pl.BlockSpec((N_EXPERTS, COMB_TM), lambda b, st: (0, b)),
                  pl.BlockSpec((COMB_TM, D_MODEL), lambda b, st: (b, 0)),
                  pl.BlockSpec((8, 6 * D_MODEL), lambda b, st: (0, 0))],
        out_specs=pl.BlockSpec((COMB_TM, D_MODEL), lambda b, st: (b, 0)),
        scratch_shapes=[pltpu.VMEM((2, COMB_STAGE, 2 * D_MODEL), BF16),
                        pltpu.VMEM((COMB_TM, D_MODEL), F32),
                        pltpu.SemaphoreType.DMA((2,))],
    )
    return pl.pallas_call(
        _combine_kernel,
        grid_spec=grid_spec,
        out_shape=jax.ShapeDtypeStruct((N_ALL, D_MODEL), F32),
        compiler_params=_params(("arbitrary",)),
        name="moe_combine",
    )(starts_flat, ye, pos, x, mod_l)


def _moe_layer(layer, x, h, aff, mod_l, w_gate, w_up, w_down):
    idx, gate, st, pos = _route(aff.reshape(N_EXPERTS, ROUTE_NB, LANES))
    ye = _moe_ffn(layer, h, idx.reshape(-1), gate.reshape(N_EXPERTS, CAP_ALL, 1), w_gate, w_up, w_down)
    return _combine(ye, st[:, 0, :].reshape(-1), pos.reshape(N_EXPERTS, N_ALL), x, mod_l)


def _final_norm_kernel(x_ref, g_ref, oc_ref, ol_ref, *, tm):
    x = x_ref[...]
    ms = jnp.mean(x * x, axis=-1, keepdims=True)
    y = x * lax.rsqrt(ms + NORM_EPS) * g_ref[...]
    is_ctx = pl.program_id(0) < N_CTX // tm

    @pl.when(is_ctx)
    def _():
        oc_ref[...] = y

    @pl.when(jnp.logical_not(is_ctx))
    def _():
        ol_ref[...] = y


def _final_norm(x, g, tm=1024):
    nctx = N_CTX // tm
    return pl.pallas_call(
        functools.partial(_final_norm_kernel, tm=tm),
        grid=(N_ALL // tm,),
        in_specs=[pl.BlockSpec((tm, D_MODEL), lambda m: (m, 0)),
                  pl.BlockSpec((1, D_MODEL), lambda m: (0, 0))],
        out_specs=[pl.BlockSpec((tm, D_MODEL), lambda m: (jnp.minimum(m, nctx - 1), 0)),
                   pl.BlockSpec((tm, D_MODEL), lambda m: (jnp.maximum(m - nctx, 0), 0))],
        out_shape=[jax.ShapeDtypeStruct((N_CTX, D_MODEL), F32),
                   jax.ShapeDtypeStruct((N_LAT, D_MODEL), F32)],
        compiler_params=_params(("arbitrary",)),
        name="final_norm",
    )(x, g)


def _sink_rows(sink, rows):
    grp = SWA_HEADS // SWA_KV_HEADS
    col = jnp.repeat(sink.reshape(SWA_KV_HEADS, grp), rows, axis=1).reshape(SWA_KV_HEADS, grp * rows, 1)
    return jnp.broadcast_to(col, (SWA_KV_HEADS, grp * rows, LANES))


def _lane_row(v, width=LANES):
    return jnp.zeros((1, width), F32).at[0, :v.shape[0]].set(v)


def kernel(x_prompt, x_sample, cache_na_k, cache_na_v, state_ssd, cache_swa_k, cache_swa_v, c, c_ctx, norm_mix, norm_ffn, w_mod, b_mod, w_in_even, na_rpb, ssd_conv_w, ssd_conv_b, ssd_a_log, ssd_dt_bias, ssd_d, ssd_norm, w_out_even, w_in_odd, swa_sink, w_out_odd, w_router, w_gate, w_up, w_down, final_norm):
    d = D_MODEL
    x = (x_prompt.reshape(N_CTX, d), x_sample.reshape(N_LAT, d))
    cond8 = jnp.zeros((8, d), F32).at[0].set(c_ctx).at[1:1 + DEC_BATCH].set(c)
    mod = _adaln(cond8, w_mod, b_mod)
    cos_t, sin_t = _rope_tables()
    new_na_k, new_na_v, new_ssd, new_swa_k, new_swa_v = [], [], [], [], []
    for l in range(DEPTH):
        j = l // 2
        mod_l = mod[l]
        g_mix = norm_mix[l].reshape(1, d)
        g_ffn = norm_ffn[l].reshape(1, d)
        if l % 2 == 0:
            w_in = jnp.pad(w_in_even[j], ((0, 0), (0, EVEN_IN_PAD - EVEN_IN))).astype(BF16)
            proj = _proj_in(x, g_mix, mod_l, w_in)
            o_ctx = _ctx_na(proj)
            ck = cache_na_k[:, j].reshape(DEC_BATCH, PAST_LEN, NA_WIDTH)
            cv = cache_na_v[:, j].reshape(DEC_BATCH, PAST_LEN, NA_WIDTH)
            o_lat = _lat_na(proj, ck, cv, _na_col_table(na_rpb[j]))
            consts = [jnp.pad(ssd_conv_w[j], ((0, 8 - SSD_CONV_W), (0, 0))),
                      ssd_conv_b[j].reshape(1, SSD_CONV_CH),
                      _lane_row(-jnp.exp(ssd_a_log[j].reshape(-1))),
                      _lane_row(ssd_dt_bias[j].reshape(-1)),
                      jnp.repeat(ssd_d[j], SSD_HEAD_DIM).reshape(1, SSD_D_INNER),
                      ssd_norm[j].reshape(1, SSD_D_INNER)]
            y_ctx, st = _ssd(proj, None, consts, seq=SEQ, nbatch=BATCH, row_blk0=0, want_state=True)
            h0t = state_ssd[:, j].transpose(0, 1, 4, 2, 3).reshape(DEC_BATCH, 2, SSD_STATE, SSD_D_INNER)
            y_lat, _ = _ssd(proj, h0t, consts, seq=DEC_SEQ, nbatch=DEC_BATCH, row_blk0=N_CTX // DEC_SEQ,
                            want_state=False)
            x, h, aff = _out_proj(o_ctx, o_lat, 0, y_ctx, y_lat, 0, w_out_even[j].astype(BF16), x, mod_l,
                                  g_ffn, w_router[l].T)
            new_na_k.append(proj[:N_CTX, NA_WIDTH:2 * NA_WIDTH].reshape(BATCH, SEQ, NA_WIDTH))
            new_na_v.append(proj[:N_CTX, 2 * NA_WIDTH:3 * NA_WIDTH].reshape(BATCH, SEQ, NA_WIDTH))
            new_ssd.append(st)
        else:
            proj = _proj_in(x, g_mix, mod_l, w_in_odd[j].astype(BF16))
            o_ctx = _ctx_swa(proj, _sink_rows(swa_sink[j], SEQ))
            ck = cache_swa_k[:, j].reshape(DEC_BATCH, PAST_LEN, SWA_KV)
            cv = cache_swa_v[:, j].reshape(DEC_BATCH, PAST_LEN, SWA_KV)
            o_lat = _lat_swa(proj, ck, cv, cos_t, sin_t, _sink_rows(swa_sink[j], SWA_BLOCK))
            x, h, aff = _out_proj(o_ctx, o_lat, 0, o_ctx, o_lat, 1, w_out_odd[j].astype(BF16), x, mod_l,
                                  g_ffn, w_router[l].T)
            new_swa_k.append(proj[:N_CTX, SWA_Q:SWA_Q + SWA_KV].reshape(BATCH, SEQ, SWA_KV))
            new_swa_v.append(proj[:N_CTX, SWA_Q + SWA_KV:].reshape(BATCH, SEQ, SWA_KV))
        x = _moe_layer(l, x, h, aff, mod_l, w_gate, w_up, w_down)
    y_ctx, y_lat = _final_norm(x, final_norm.reshape(1, d))
    n_even, n_odd = len(new_na_k), len(new_swa_k)
    na_shape = (BATCH, n_even, SEQ, NA_HEADS, HEAD_DIM)
    swa_shape = (BATCH, n_odd, SEQ, SWA_KV_HEADS, HEAD_DIM)
    ssd_t = jnp.stack(new_ssd, axis=1).reshape(BATCH, n_even, 2, SSD_STATE, SSD_HEADS, SSD_HEAD_DIM)
    return (y_ctx.reshape(BATCH, SEQ, d), y_lat.reshape(DEC_BATCH, DEC_SEQ, d),
            jnp.stack(new_na_k, axis=1).reshape(na_shape), jnp.stack(new_na_v, axis=1).reshape(na_shape),
            ssd_t.transpose(0, 1, 2, 4, 5, 3),
            jnp.stack(new_swa_k, axis=1).reshape(swa_shape), jnp.stack(new_swa_v, axis=1).reshape(swa_shape))
```
